```python
import jax, jax.numpy as jnp
from jax import lax
import numpy as np

D_MODEL = 1024
BATCH = 2
SEQ = 8192
DEPTH = 2

D_MIX = D_MODEL
C_CONV = D_MIX // 4
NSA_HEADS = 4
HEAD_DIM = 64
C_NSA = NSA_HEADS * HEAD_DIM
C_SGU = D_MIX // 4
C_POOL = D_MIX // 4
CONV_K = 31
CMP_LEN = 32
CMP_STRIDE = 16
SEL_LEN = 64
SEL_TOPK = 16
WINDOW = 512
Q_BLOCK = 128
SGU_CHUNK = 128
SGU_GROUPS = 4
POOL_WINDOWS = (2, 4, 8, 16)
FFN_DIM = 2816
FFN_CONV_K = 3
ROPE_THETA = 10000.0
RMS_EPS = 1e-6
LN_EPS = 1e-5
NEG_INF = -1e30
FORCE_SCORE = 1e6
N_IN = 2 * C_CONV + C_NSA + 6 * HEAD_DIM + 3 * NSA_HEADS + 2 * C_SGU + C_POOL

kernel_name = 'hybrid_parallel_nsa_conv_sgu_pool'


def rms_norm(x, g):
    xf = x.astype(jnp.float32)
    y = xf * lax.rsqrt(jnp.mean(xf * xf, axis=-1, keepdims=True) + RMS_EPS)
    return y.astype(x.dtype) * g


def layer_norm(x, g, b):
    xf = x.astype(jnp.float32)
    mu = jnp.mean(xf, axis=-1, keepdims=True)
    var = jnp.mean(jnp.square(xf - mu), axis=-1, keepdims=True)
    return ((xf - mu) * lax.rsqrt(var + LN_EPS)).astype(x.dtype) * g + b


def causal_dwconv(x, w, b):
    k, c = w.shape
    y = lax.conv_general_dilated(x, w[:, None, :].astype(x.dtype), window_strides=(1,),
                                 padding=[(k - 1, 0)], dimension_numbers=('NWC', 'WIO', 'NWC'),
                                 feature_group_count=c)
    return y + b


def rope_tables(seq):
    half = HEAD_DIM // 2
    inv = ROPE_THETA ** (-jnp.arange(half, dtype=jnp.float32) * 2.0 / HEAD_DIM)
    ang = jnp.arange(seq, dtype=jnp.float32)[:, None] * inv[None, :]
    return jnp.cos(ang), jnp.sin(ang)


def apply_rope(x, cos, sin):
    half = HEAD_DIM // 2
    shp = (cos.shape[0],) + (1,) * (x.ndim - 3) + (half,)
    c = cos.reshape(shp).astype(x.dtype)
    s = sin.reshape(shp).astype(x.dtype)
    x1, x2 = x[..., :half], x[..., half:]
    return jnp.concatenate([x1 * c - x2 * s, x2 * c + x1 * s], axis=-1)


def masked_softmax(s, mask):
    p = jax.nn.softmax(jnp.where(mask, s, NEG_INF), axis=-1)
    return jnp.where(mask, p, 0.0)


def conformer_conv(z, w, b, ln_g, ln_b):
    a, gate = jnp.split(z, 2, axis=-1)
    y = causal_dwconv(a * jax.nn.sigmoid(gate), w, b)
    return jax.nn.silu(layer_norm(y, ln_g, ln_b))


def nsa_attention(q, k_cmp, v_cmp, k_slc, v_slc, k_win, v_win, gates,
                  pe_k, pe_v, ck_w1, ck_w2, cv_w1, cv_w2, cos, sin):
    bsz, seq = q.shape[0], q.shape[1]
    scale = HEAD_DIM ** -0.5
    n_cmp = (seq - CMP_LEN) // CMP_STRIDE + 1
    n_sel = seq // SEL_LEN
    top_n = min(SEL_TOPK, n_sel)
    ratio = SEL_LEN // CMP_STRIDE
    n_ov = CMP_LEN // CMP_STRIDE
    front = n_ov - 1
    right = ratio * n_sel + ratio - front - n_cmp

    blk_idx = jnp.arange(n_cmp)[:, None] * CMP_STRIDE + jnp.arange(CMP_LEN)[None, :]

    def compress(t, pe, w1, w2):
        blocks = (t[:, blk_idx] + pe).reshape(bsz, n_cmp, CMP_LEN * HEAD_DIM)
        return jax.nn.gelu(blocks @ w1) @ w2

    kc = compress(k_cmp, pe_k, ck_w1, ck_w2)
    vc = compress(v_cmp, pe_v, cv_w1, cv_w2)
    cmp_end = jnp.arange(n_cmp) * CMP_STRIDE + CMP_LEN - 1

    q_cmp = q * scale
    q_rot = apply_rope(q, cos, sin) * scale
    ks_blocks = apply_rope(k_slc, cos, sin).reshape(bsz, n_sel, SEL_LEN, HEAD_DIM)
    vs_blocks = v_slc.reshape(bsz, n_sel, SEL_LEN, HEAD_DIM)
    kw_pad = jnp.pad(apply_rope(k_win, cos, sin), ((0, 0), (WINDOW, 0), (0, 0)))
    vw_pad = jnp.pad(v_win, ((0, 0), (WINDOW, 0), (0, 0)))
    sel_j = jnp.arange(n_sel)
    gather = jax.vmap(lambda blocks, idx: blocks[idx])

    def query_block(i):
        t0 = i * Q_BLOCK
        pos = t0 + jnp.arange(Q_BLOCK)
        qc = lax.dynamic_slice_in_dim(q_cmp, t0, Q_BLOCK, axis=1)
        qr = lax.dynamic_slice_in_dim(q_rot, t0, Q_BLOCK, axis=1)
        g = lax.dynamic_slice_in_dim(gates, t0, Q_BLOCK, axis=1)

        s = jnp.einsum('bthd,bnd->bhtn', qc, kc, preferred_element_type=jnp.float32)
        p_cmp = masked_softmax(s, cmp_end[None, :] <= pos[:, None])
        o_cmp = jnp.einsum('bhtn,bnd->bthd', p_cmp.astype(vc.dtype), vc)

        pp = jnp.pad(p_cmp.sum(axis=1), ((0, 0), (0, 0), (front, right)))
        imp = jnp.zeros((bsz, Q_BLOCK, n_sel), jnp.float32)
        for m in range(ratio):
            for n in range(n_ov):
                imp = imp + pp[..., m - n + front::ratio][..., :n_sel]
        cur = (pos // SEL_LEN)[:, None]
        forced = (sel_j[None, :] == 0) | (sel_j[None, :] == cur) | (sel_j[None, :] == cur - 1)
        valid = sel_j[None, :] * SEL_LEN <= pos[:, None]
        imp = jnp.where(valid, jnp.where(forced, FORCE_SCORE, imp), NEG_INF)
        top_val, top_idx = lax.top_k(imp, top_n)

        ksel = gather(ks_blocks, top_idx).reshape(bsz, Q_BLOCK, top_n * SEL_LEN, HEAD_DIM)
        vsel = gather(vs_blocks, top_idx).reshape(bsz, Q_BLOCK, top_n * SEL_LEN, HEAD_DIM)
        tok = (top_idx[..., None] * SEL_LEN + jnp.arange(SEL_LEN)).reshape(bsz, Q_BLOCK, top_n * SEL_LEN)
        sel_ok = jnp.repeat(top_val > 0.5 * NEG_INF, SEL_LEN, axis=-1)
        mask_sel = sel_ok & (tok <= pos[None, :, None])
        s = jnp.einsum('bthd,btkd->bhtk', qr, ksel, preferred_element_type=jnp.float32)
        p = masked_softmax(s, mask_sel[:, None])
        o_slc = jnp.einsum('bhtk,btkd->bthd', p.astype(vsel.dtype), vsel)

        kw = lax.dynamic_slice_in_dim(kw_pad, t0, Q_BLOCK + WINDOW, axis=1)
        vw = lax.dynamic_slice_in_dim(vw_pad, t0, Q_BLOCK + WINDOW, axis=1)
        kpos = t0 - WINDOW + jnp.arange(Q_BLOCK + WINDOW)
        mask_win = ((kpos[None, :] >= 0) & (kpos[None, :] <= pos[:, None])
                    & (kpos[None, :] > pos[:, None] - WINDOW))
        s = jnp.einsum('bthd,bkd->bhtk', qr, kw, preferred_element_type=jnp.float32)
        p = masked_softmax(s, mask_win)
        o_win = jnp.einsum('bhtk,bkd->bthd', p.astype(vw.dtype), vw)

        return g[..., 0:1] * o_cmp + g[..., 1:2] * o_slc + g[..., 2:3] * o_win

    out = lax.map(query_block, jnp.arange(seq // Q_BLOCK))
    return out.transpose(1, 0, 2, 3, 4).reshape(bsz, seq, C_NSA)


def spatial_gating(z, ln_g, ln_b, w_s, b_s):
    z = jax.nn.gelu(z)
    u, v = jnp.split(z, 2, axis=-1)
    v = layer_norm(v, ln_g, ln_b)
    bsz, seq, c = v.shape
    vr = v.reshape(bsz, seq // SGU_CHUNK, SGU_CHUNK, SGU_GROUPS, c // SGU_GROUPS)
    causal = jnp.tril(jnp.ones((SGU_CHUNK, SGU_CHUNK), dtype=bool))
    ws = jnp.where(causal, w_s, 0.0).astype(v.dtype)
    f = jnp.einsum('gts,bcsgd->bctgd', ws, vr) + b_s.T[:, :, None]
    return u * f.reshape(bsz, seq, c)


def multiscale_pool(z, w_p, scale):
    bsz, seq, c = z.shape
    cg = c // len(POOL_WINDOWS)
    zf = z.astype(jnp.float32)
    cs = jnp.pad(jnp.cumsum(zf, axis=1), ((0, 0), (1, 0), (0, 0)))
    t = jnp.arange(seq)
    outs = []
    for gi, w in enumerate(POOL_WINDOWS):
        c_g = cs[..., gi * cg:(gi + 1) * cg]
        lag = jnp.pad(c_g[:, :seq + 1 - w], ((0, 0), (w - 1, 0), (0, 0)))
        cnt = jnp.minimum(t + 1, w).astype(jnp.float32)[:, None]
        outs.append((c_g[:, 1:] - lag) / cnt - zf[..., gi * cg:(gi + 1) * cg])
    pooled = jnp.stack(outs, axis=2).astype(z.dtype)
    y = jnp.einsum('bsgc,gcd->bsgd', pooled, w_p).reshape(bsz, seq, c)
    return y * scale


def conv_ffn(h, w_up, cw, cb, w_down):
    gu = causal_dwconv(h @ w_up, cw, cb)
    g, u = jnp.split(gu, 2, axis=-1)
    return (jax.nn.gelu(g, approximate=True) * u) @ w_down


def setup_inputs(seed: int = 0) -> dict:
    key = jax.random.key(seed)
    ks = jax.random.split(key, 32)
    L = DEPTH
    hd = HEAD_DIM

    def nrm(k, shape, s):
        return jax.random.normal(k, shape, jnp.float32) * s

    def gain(k, shape):
        return 1.0 + nrm(k, shape, 0.05)

    return {
        'x': nrm(ks[0], (BATCH, SEQ, D_MODEL), 1.0),
        'norm_mix_pre': gain(ks[1], (L, D_MODEL)),
        'norm_mix_post': gain(ks[2], (L, D_MODEL)),
        'norm_ffn_pre': gain(ks[3], (L, D_MODEL)),
        'norm_ffn_post': gain(ks[4], (L, D_MODEL)),
        'w_in': nrm(ks[5], (L, D_MODEL, N_IN), D_MODEL ** -0.5),
        'w_out': nrm(ks[6], (L, D_MIX, D_MODEL), D_MIX ** -0.5),
        'conv_dw_w': nrm(ks[7], (L, CONV_K, C_CONV), CONV_K ** -0.5),
        'conv_dw_b': nrm(ks[8], (L, C_CONV), 0.02),
        'conv_ln_g': gain(ks[9], (L, C_CONV)),
        'conv_ln_b': nrm(ks[10], (L, C_CONV), 0.02),
        'nsa_pe_k': nrm(ks[11], (L, CMP_LEN, hd), 0.1),
        'nsa_pe_v': nrm(ks[12], (L, CMP_LEN, hd), 0.1),
        'nsa_ck_w1': nrm(ks[13], (L, CMP_LEN * hd, hd), (CMP_LEN * hd) ** -0.5),
        'nsa_ck_w2': nrm(ks[14], (L, hd, hd), hd ** -0.5),
        'nsa_cv_w1': nrm(ks[15], (L, CMP_LEN * hd, hd), (CMP_LEN * hd) ** -0.5),
        'nsa_cv_w2': nrm(ks[16], (L, hd, hd), hd ** -0.5),
        'sgu_ln_g': gain(ks[17], (L, C_SGU)),
        'sgu_ln_b': nrm(ks[18], (L, C_SGU), 0.02),
        'sgu_w': nrm(ks[19], (L, SGU_GROUPS, SGU_CHUNK, SGU_CHUNK), SGU_CHUNK ** -0.5),
        'sgu_b': 1.0 + nrm(ks[20], (L, SGU_GROUPS, SGU_CHUNK), 0.1),
        'pool_w': nrm(ks[21], (L, len(POOL_WINDOWS), C_POOL // 4, C_POOL // 4), (C_POOL // 4) ** -0.5),
        'pool_scale': gain(ks[22], (L, C_POOL)),
        'ffn_up': nrm(ks[23], (L, D_MODEL, 2 * FFN_DIM), D_MODEL ** -0.5),
        'ffn_conv_w': nrm(ks[24], (L, FFN_CONV_K, 2 * FFN_DIM), FFN_CONV_K ** -0.5),
        'ffn_conv_b': nrm(ks[25], (L, 2 * FFN_DIM), 0.02),
        'ffn_down': nrm(ks[26], (L, FFN_DIM, D_MODEL), FFN_DIM ** -0.5),
    }


def reference(x, norm_mix_pre, norm_mix_post, norm_ffn_pre, norm_ffn_post, w_in, w_out,
              conv_dw_w, conv_dw_b, conv_ln_g, conv_ln_b,
              nsa_pe_k, nsa_pe_v, nsa_ck_w1, nsa_ck_w2, nsa_cv_w1, nsa_cv_w2,
              sgu_ln_g, sgu_ln_b, sgu_w, sgu_b, pool_w, pool_scale,
              ffn_up, ffn_conv_w, ffn_conv_b, ffn_down):
    bsz, seq, _ = x.shape
    cos, sin = rope_tables(seq)
    sizes = (2 * C_CONV, C_NSA, 6 * HEAD_DIM, 3 * NSA_HEADS, 2 * C_SGU, C_POOL)
    offsets = [int(o) for o in np.cumsum(sizes)[:-1]]
    for l in range(DEPTH):
        h = rms_norm(x, norm_mix_pre[l])
        z = h @ w_in[l]
        za, zq, zkv, zg, zc, zd = jnp.split(z, offsets, axis=-1)
        y_conv = conformer_conv(za, conv_dw_w[l], conv_dw_b[l], conv_ln_g[l], conv_ln_b[l])
        q = zq.reshape(bsz, seq, NSA_HEADS, HEAD_DIM)
        kv = zkv.reshape(bsz, seq, 6, HEAD_DIM)
        gates = jax.nn.sigmoid(zg).reshape(bsz, seq, NSA_HEADS, 3)
        y_nsa = nsa_attention(q, kv[:, :, 0], kv[:, :, 1], kv[:, :, 2], kv[:, :, 3], kv[:, :, 4], kv[:, :, 5],
                              gates, nsa_pe_k[l], nsa_pe_v[l], nsa_ck_w1[l], nsa_ck_w2[l],
                              nsa_cv_w1[l], nsa_cv_w2[l], cos, sin)
        y_sgu = spatial_gating(zc, sgu_ln_g[l], sgu_ln_b[l], sgu_w[l], sgu_b[l])
        y_pool = multiscale_pool(zd, pool_w[l], pool_scale[l])
        y = jnp.concatenate([y_conv, y_nsa, y_sgu, y_pool], axis=-1) @ w_out[l]
        x = x + rms_norm(y, norm_mix_post[l])
        h = rms_norm(x, norm_ffn_pre[l])
        y = conv_ffn(h, ffn_up[l], ffn_conv_w[l], ffn_conv_b[l], ffn_down[l])
        x = x + rms_norm(y, norm_ffn_post[l])
    return x
```

```python
import functools

import numpy as np
import jax
import jax.numpy as jnp
from jax import lax
from jax.experimental import pallas as pl
from jax.experimental.pallas import tpu as pltpu

HEAD_DIM = 64
NSA_HEADS = 4
CONV_K = 31
CMP_LEN = 32
CMP_STRIDE = 16
SEL_LEN = 64
SEL_TOPK = 16
WINDOW = 512
SGU_CHUNK = 128
SGU_GROUPS = 4
POOL_WINDOWS = (2, 4, 8, 16)
FFN_CONV_K = 3
ROPE_THETA = 10000.0
RMS_EPS = 1e-6
LN_EPS = 1e-5
NEG_INF = -1e30
FORCE_SCORE = 1e6
REMOVED = -3e38

LANES = 128
MXU_DTYPE = jnp.bfloat16
VMEM_LIMIT = 56 * 1024 * 1024

TM_PROJ = 512
TQ = 128
TK = 256
FFN_CHUNK = 256
CONV_HALO = 32
POOL_HALO = 16
FFN_HALO = 8

_NT = (((1,), (1,)), ((), ()))


def _cparams(n_axes):
    return pltpu.CompilerParams(dimension_semantics=("arbitrary",) * n_axes,
                                vmem_limit_bytes=VMEM_LIMIT)


def _rms(x, g):
    return x * lax.rsqrt(jnp.mean(x * x, axis=-1, keepdims=True) + RMS_EPS) * g


def _layer_norm(x, g, b):
    mu = jnp.mean(x, axis=-1, keepdims=True)
    var = jnp.mean(jnp.square(x - mu), axis=-1, keepdims=True)
    return (x - mu) * lax.rsqrt(var + LN_EPS) * g + b


def _bf(x):
    return x.astype(MXU_DTYPE)


def _swap64(x):
    return pltpu.roll(x, 64, axis=1)


def _in_proj_kernel(x_ref, g_ref, w_ref, cos_ref, sin_ref, cs_ref,
                    za_ref, zc_ref, zd_ref, qc_ref, qr_ref, kv_ref,
                    ks_ref, vs_ref, kw_ref, vw_ref, gate_ref, *, tm):
    i = pl.program_id(1)
    h = _bf(_rms(x_ref[...], g_ref[...]))

    def proj(lo, hi):
        return jnp.dot(h, w_ref[:, lo:hi], preferred_element_type=jnp.float32)

    za_ref[...] = proj(0, 512)
    zc_ref[...] = proj(512, 1024)
    zd_ref[...] = proj(1024, 1280)
    q = proj(1280, 1536)
    q_sw = proj(1536, 1792)
    kv_ref[...] = proj(1792, 1920)
    ksl = proj(1920, 2048)
    kwn = proj(2048, 2176)
    vsw = proj(2176, 2304)
    gate_ref[...] = jax.nn.sigmoid(proj(2304, 2432))

    scale = HEAD_DIM ** -0.5
    qc_ref[...] = _bf(q * scale)
    qr_ref[...] = _bf((q * cos_ref[...] + q_sw * sin_ref[...]) * scale)

    cs = cs_ref[...]
    ts = ksl * cs
    k_rot2 = ts + _swap64(ts)
    tw = kwn * cs
    kw_rot2 = tw + _swap64(tw)
    pos = i * tm + lax.broadcasted_iota(jnp.int32, (tm, LANES), 0)
    lane = lax.broadcasted_iota(jnp.int32, (tm, LANES), 1)
    onehot = jnp.where((pos >> 6) == lane, 1.0, 0.0)
    ks_ref[:, 0:LANES] = _bf(onehot)
    ks_ref[:, LANES:2 * LANES] = _bf(k_rot2)
    kw_ref[...] = _bf(kw_rot2)
    ones_col = jnp.where(lane == 64, 1.0, 0.0)
    vs_ref[...] = _bf(jnp.where(lane < 64, vsw, ones_col))
    vw_ref[...] = _bf(jnp.where(lane < 64, _swap64(vsw), ones_col))


def _in_proj(x, g, w_ext, cos4, sin4, cs):
    bsz, seq, d = x.shape
    tm = min(TM_PROJ, seq)
    n_ext = w_ext.shape[1]
    tok = lambda c: pl.BlockSpec((None, tm, c), lambda b, i: (b, i, 0))
    f32, bf16 = jnp.float32, MXU_DTYPE
    outs = [(512, f32), (512, f32), (256, f32), (256, bf16), (256, bf16), (128, f32),
            (256, bf16), (128, bf16), (128, bf16), (128, bf16), (128, f32)]
    return pl.pallas_call(
        functools.partial(_in_proj_kernel, tm=tm),
        grid=(bsz, seq // tm),
        in_specs=[tok(d),
                  pl.BlockSpec((1, d), lambda b, i: (0, 0)),
                  pl.BlockSpec((d, n_ext), lambda b, i: (0, 0)),
                  pl.BlockSpec((tm, 256), lambda b, i: (i, 0)),
                  pl.BlockSpec((tm, 256), lambda b, i: (i, 0)),
                  pl.BlockSpec((tm, LANES), lambda b, i: (i, 0))],
        out_specs=[tok(c) for c, _ in outs],
        out_shape=[jax.ShapeDtypeStruct((bsz, seq, c), dt) for c, dt in outs],
        compiler_params=_cparams(2),
        name="in_proj",
    )(x, g, w_ext, cos4, sin4, cs)


def _compress_kernel(a_ref, pe_lo_ref, pe_hi_ref, w_lo_ref, w_hi_ref, w2_ref, kk_ref, vv_ref):
    a = a_ref[...]
    u = jnp.dot(_bf(a + pe_lo_ref[...]), w_lo_ref[...], preferred_element_type=jnp.float32)
    v = jnp.dot(_bf(a + pe_hi_ref[...]), w_hi_ref[...], preferred_element_type=jnp.float32)
    n = a.shape[0]
    pre = u + pltpu.roll(v, n - 1, axis=0)
    act = jax.nn.gelu(pre, approximate=True)
    kv = jnp.dot(_bf(act), w2_ref[...], preferred_element_type=jnp.float32)
    vk = _swap64(kv)
    lane = lax.broadcasted_iota(jnp.int32, (n, LANES), 1)
    kk_ref[...] = _bf(jnp.where(lane < 64, kv, vk))
    vv_ref[...] = _bf(jnp.where(lane < 64, vk, jnp.where(lane == 64, 1.0, 0.0)))


def _compress(a, pe_lo, pe_hi, w_lo, w_hi, w2):
    bsz, n, width = a.shape
    full = lambda shape: pl.BlockSpec(shape, lambda b: (0,) * len(shape))
    per_b = lambda c: pl.BlockSpec((None, n, c), lambda b: (b, 0, 0))
    return pl.pallas_call(
        _compress_kernel,
        grid=(bsz,),
        in_specs=[per_b(width), full((1, width)), full((1, width)),
                  full((width, LANES)), full((width, LANES)), full((LANES, LANES))],
        out_specs=[per_b(LANES), per_b(LANES)],
        out_shape=[jax.ShapeDtypeStruct((bsz, n, LANES), MXU_DTYPE)] * 2,
        compiler_params=_cparams(1),
        name="compress",
    )(a, pe_lo, pe_hi, w_lo, w_hi, w2)


def _nsa_kernel(qc_ref, qr_ref, gate_ref, kk_ref, vv_ref, mt_ref, ks_ref, vs_ref, kw_ref, vw_ref,
                out_ref, m_ref, acc_ref, *, tq, tk, seq):
    qi = pl.program_id(1)
    t0 = qi * tq
    rows = NSA_HEADS * tq
    n_cmp_pad = kk_ref.shape[0]
    top_n = min(SEL_TOPK, seq // SEL_LEN)
    n_sel = LANES
    win_keys = min(WINDOW + tq, seq)

    lane =lax.broadcasted_iota(jnp.int32, (tq, LANES), 1)

    def head_rows(q_ref):
        parts = []
        for hd in range(NSA_HEADS):
            pair = q_ref[:, (hd // 2) * LANES:(hd // 2 + 1) * LANES]
            keep = (lane < 64) if hd % 2 == 0 else (lane >= 64)
            parts.append(jnp.where(keep, pair, jnp.zeros_like(pair)))
        return jnp.concatenate(parts, axis=0)

    q_c = head_rows(qc_ref)
    q_r = head_rows(qr_ref)
    pos_r = t0 + (lax.broadcasted_iota(jnp.int32, (rows, 1), 0) & (tq - 1))

    s = lax.dot_general(q_c, kk_ref[...], _NT, preferred_element_type=jnp.float32)
    n_idx = lax.broadcasted_iota(jnp.int32, (rows, n_cmp_pad), 1)
    ok = (n_idx * CMP_STRIDE + (CMP_LEN - 1)) <= pos_r
    sm = jnp.where(ok, s, NEG_INF)
    mx = jnp.max(sm, axis=1, keepdims=True)
    e = jnp.where(ok, jnp.exp(sm - mx), 0.0)
    den = jnp.sum(e, axis=1, keepdims=True)
    p_cmp = e / jnp.where(den > 0.0, den, 1.0)
    o_cmp = jnp.dot(_bf(p_cmp), vv_ref[...], preferred_element_type=jnp.float32)

    p_sum = p_cmp[0:tq] + p_cmp[tq:2 * tq] + p_cmp[2 * tq:3 * tq] + p_cmp[3 * tq:4 * tq]
    hi = _bf(p_sum)
    r1 = p_sum - hi.astype(jnp.float32)
    mid = _bf(r1)
    lo = _bf(r1 - mid.astype(jnp.float32))
    mt = mt_ref[...]
    imp = (lax.dot_general(mt, hi, _NT, preferred_element_type=jnp.float32)
           + lax.dot_general(mt, mid, _NT, preferred_element_type=jnp.float32)
           + lax.dot_general(mt, lo, _NT, preferred_element_type=jnp.float32))

    blk = lax.broadcasted_iota(jnp.int32, (n_sel, tq), 0)
    blk_f = blk.astype(jnp.float32)
    pos_t = t0 + lax.broadcasted_iota(jnp.int32, (n_sel, tq), 1)
    cur = pos_t >> 6
    forced = (blk == 0) | (blk == cur) | (blk == cur - 1)
    valid = (blk * SEL_LEN) <= pos_t
    val = jnp.where(valid, jnp.where(forced, FORCE_SCORE, imp), NEG_INF)
    picked = jnp.zeros((n_sel, tq), jnp.float32)
    for _ in range(top_n):
        best = jnp.max(val, axis=0, keepdims=True)
        first = jnp.min(jnp.where(val == best, blk_f, float(n_sel)), axis=0, keepdims=True)
        hit = blk_f == first
        picked = jnp.where(hit, 1.0, picked)
        val = jnp.where(hit, REMOVED, val)
    bias_t = jnp.where((picked > 0.5) & valid, 0.0, NEG_INF)
    bias = _bf(bias_t.T)

    l_sel = jnp.concatenate(
        [jnp.concatenate([bias, q_r[hd * tq:(hd + 1) * tq]], axis=1) for hd in range(NSA_HEADS)], axis=0)
    m_ref[...] = jnp.full((rows, 1), NEG_INF, jnp.float32)
    acc_ref[...] = jnp.zeros((rows, LANES), jnp.float32)

    def flash_step(kt, causal):
        k0 = pl.multiple_of(kt * tk, tk)
        sc = lax.dot_general(l_sel, ks_ref[pl.ds(k0, tk), :], _NT, preferred_element_type=jnp.float32)
        if causal:
            kpos = k0 + lax.broadcasted_iota(jnp.int32, (rows, tk), 1)
            sc = jnp.where(kpos <= pos_r, sc, NEG_INF)
        m_old = m_ref[...]
        m_new = jnp.maximum(m_old, jnp.max(sc, axis=1, keepdims=True))
        alpha = jnp.exp(m_old - m_new)
        p = jnp.exp(sc - m_new)
        acc_ref[...] = alpha * acc_ref[...] + jnp.dot(_bf(p), vs_ref[pl.ds(k0, tk), :],
                                                      preferred_element_type=jnp.float32)
        m_ref[...] = m_new

    n_full = t0 // tk

    def body(kt, carry):
        flash_step(kt, False)
        return carry

    lax.fori_loop(0, n_full, body, 0)
    flash_step(n_full, True)
    acc = acc_ref[...]
    o_slc = acc / acc[:, 64:65]

    start = pl.multiple_of(jnp.maximum(t0 + tq - win_keys, 0), tq)
    sw = lax.dot_general(q_r, kw_ref[pl.ds(start, win_keys), :], _NT, preferred_element_type=jnp.float32)
    kpos = start + lax.broadcasted_iota(jnp.int32, (rows, win_keys), 1)
    okw = (kpos <= pos_r) & (kpos > pos_r - WINDOW)
    sw = jnp.where(okw, sw, NEG_INF)
    mw = jnp.max(sw, axis=1, keepdims=True)
    pw = jnp.exp(sw - mw)
    ow = jnp.dot(_bf(pw), vw_ref[pl.ds(start, win_keys), :], preferred_element_type=jnp.float32)
    o_win = ow / ow[:, 64:65]

    gate = gate_ref[...]
    outs = []
    for hd in range(NSA_HEADS):
        r = slice(hd * tq, (hd + 1) * tq)
        outs.append(gate[:, 3 * hd:3 * hd + 1] * o_cmp[r]
                    + gate[:, 3 * hd + 1:3 * hd + 2] * o_slc[r]
                    + gate[:, 3 * hd + 2:3 * hd + 3] * o_win[r])
    out_ref[:, 0:LANES] = jnp.where(lane < 64, outs[0], _swap64(outs[1]))
    out_ref[:, LANES:2 * LANES] = jnp.where(lane < 64, outs[2], _swap64(outs[3]))


def _nsa(qc, qr, gate, kk, vv, mt, ks, vs, kw, vw):
    bsz, seq, _ = qc.shape
    tq = min(TQ, seq)
    tk = min(TK, seq)
    n_cmp_pad = kk.shape[1]
    rows = NSA_HEADS * tq
    tile = lambda c: pl.BlockSpec((None, tq, c), lambda b, i: (b, i, 0))
    per_b = lambda r, c: pl.BlockSpec((None, r, c), lambda b, i: (b, 0, 0))
    return pl.pallas_call(
        functools.partial(_nsa_kernel, tq=tq, tk=tk, seq=seq),
        grid=(bsz, seq // tq),
        in_specs=[tile(256), tile(256), tile(LANES),
                  per_b(n_cmp_pad, LANES), per_b(n_cmp_pad, LANES),
                  pl.BlockSpec(mt.shape, lambda b, i: (0, 0)),
                  per_b(seq, 256), per_b(seq, LANES), per_b(seq, LANES), per_b(seq, LANES)],
        out_specs=tile(256),
        out_shape=jax.ShapeDtypeStruct((bsz, seq, 256), jnp.float32),
        scratch_shapes=[pltpu.VMEM((rows, 1), jnp.float32), pltpu.VMEM((rows, LANES), jnp.float32)],
        compiler_params=_cparams(2),
        name="nsa",
    )(qc, qr, gate, kk, vv, mt, ks, vs, kw, vw)


def _mix_out_kernel(x_ref, za_ref, zah_ref, zc_ref, zd_ref, zdh_ref, ynsa_ref,
                    cw_ref, cb_ref, clg_ref, clb_ref, slg_ref, slb_ref, sw_ref, sb_ref,
                    pw_ref, ps_ref, wout_ref, gpost_ref, out_ref,
                    uext_ref, yconv_ref, pext_ref, *, tm):
    i = pl.program_id(1)
    not_first = jnp.where(i > 0, 1.0, 0.0)

    def glu(z):
        return z[:, :256] * jax.nn.sigmoid(z[:, 256:])

    uext_ref[0:CONV_HALO, :] = glu(zah_ref[...]) * not_first
    uext_ref[CONV_HALO:, :] = glu(za_ref[...])
    rc = 64
    for r0 in range(0, tm, rc):
        acc = jnp.zeros((rc, 256), jnp.float32)
        for k in range(CONV_K):
            acc = acc + cw_ref[k:k + 1, :] * uext_ref[pl.ds(r0 + CONV_HALO - (CONV_K - 1) + k, rc), :]
        yc = _layer_norm(acc + cb_ref[...], clg_ref[...], clb_ref[...])
        yconv_ref[r0:r0 + rc, :] = _bf(yc * jax.nn.sigmoid(yc))

    zc = jax.nn.gelu(zc_ref[...], approximate=True)
    u_s = zc[:, :256]
    v_s = _layer_norm(zc[:, 256:], slg_ref[...], slb_ref[...])
    t_i = lax.broadcasted_iota(jnp.int32, (SGU_CHUNK, SGU_GROUPS * SGU_CHUNK), 0)
    s_i = lax.rem(lax.broadcasted_iota(jnp.int32, (SGU_CHUNK, SGU_GROUPS * SGU_CHUNK), 1), SGU_CHUNK)
    w_cat = _bf(jnp.where(s_i <= t_i, sw_ref[...], 0.0))
    grp = lax.broadcasted_iota(jnp.int32, (SGU_CHUNK, 256), 1) >> 6
    ysgu = []
    for c in range(tm // SGU_CHUNK):
        vc = v_s[c * SGU_CHUNK:(c + 1) * SGU_CHUNK]
        stacked = _bf(jnp.concatenate([jnp.where(grp == g, vc, 0.0) for g in range(SGU_GROUPS)], axis=0))
        f = jnp.dot(w_cat, stacked, preferred_element_type=jnp.float32) + sb_ref[...]
        ysgu.append(u_s[c * SGU_CHUNK:(c + 1) * SGU_CHUNK] * f)
    y_sgu = _bf(jnp.concatenate(ysgu, axis=0))

    pext_ref[0:POOL_HALO, :] = zdh_ref[...] * not_first
    pext_ref[POOL_HALO:, :] = zd_ref[...]
    n_ext = tm + POOL_HALO
    sums = []
    for shift in (1, 2, 4, 8):
        n = n_ext - shift
        nxt = pext_ref[pl.ds(shift, n), :] + pext_ref[pl.ds(0, n), :]
        pext_ref[pl.ds(shift, n), :] = nxt
        sums.append(pext_ref[pl.ds(POOL_HALO, tm), :])
    lane_grp = lax.broadcasted_iota(jnp.int32, (tm, 256), 1) >> 6
    pos = i * tm + lax.broadcasted_iota(jnp.int32, (tm, 256), 0)
    wsum = jnp.where(lane_grp == 0, sums[0], jnp.where(lane_grp == 1, sums[1],
                     jnp.where(lane_grp == 2, sums[2], sums[3])))
    width = jnp.where(lane_grp == 0, POOL_WINDOWS[0], jnp.where(lane_grp == 1, POOL_WINDOWS[1],
                      jnp.where(lane_grp == 2, POOL_WINDOWS[2], POOL_WINDOWS[3])))
    cnt = jnp.minimum(pos + 1, width).astype(jnp.float32)
    pooled = wsum / cnt - zd_ref[...]
    y_pool = _bf(jnp.dot(_bf(pooled), pw_ref[...], preferred_element_type=jnp.float32) * ps_ref[...])

    y = (jnp.dot(yconv_ref[...], wout_ref[0:256, :], preferred_element_type=jnp.float32)
         + jnp.dot(_bf(ynsa_ref[...]), wout_ref[256:512, :], preferred_element_type=jnp.float32)
         + jnp.dot(y_sgu, wout_ref[512:768, :], preferred_element_type=jnp.float32)
         + jnp.dot(y_pool, wout_ref[768:1024, :], preferred_element_type=jnp.float32))
    out_ref[...] = x_ref[...] + _rms(y, gpost_ref[...])


def _mix_out(x, za, zc, zd, ynsa, cw, cb, clg, clb, slg, slb, sw_cat, sb_exp, pw_bd, ps, wout, gpost):
    bsz, seq, d = x.shape
    tm = min(TM_PROJ, seq)
    tok = lambda c: pl.BlockSpec((None, tm, c), lambda b, i: (b, i, 0))
    halo = lambda c, h: pl.BlockSpec((None, h, c), lambda b, i: (b, jnp.maximum(i * (tm // h) - 1, 0), 0))
    full = lambda a: pl.BlockSpec(a.shape, lambda b, i: (0,) * a.ndim)
    weights = [cw, cb, clg, clb, slg, slb, sw_cat, sb_exp, pw_bd, ps, wout, gpost]
    return pl.pallas_call(
        functools.partial(_mix_out_kernel, tm=tm),
        grid=(bsz, seq // tm),
        in_specs=[tok(d), tok(512), halo(512, CONV_HALO), tok(512), tok(256), halo(256, POOL_HALO), tok(256)]
                 + [full(w) for w in weights],
        out_specs=tok(d),
        out_shape=jax.ShapeDtypeStruct((bsz, seq, d), jnp.float32),
        scratch_shapes=[pltpu.VMEM((tm + CONV_HALO, 256), jnp.float32),
                        pltpu.VMEM((tm, 256), MXU_DTYPE),
                        pltpu.VMEM((tm + POOL_HALO, 256), jnp.float32)],
        compiler_params=_cparams(2),
        name="mix_out",
    )(x, za, za, zc, zd, zd, ynsa, *weights)


def _ffn_kernel(x_ref, xh_ref, gpre_ref, wup_ref, cw_ref, cb_ref, wdn_ref, gpost_ref, out_ref,
                gext_ref, uext_ref, acc_ref, *, tm, n_chunks):
    i = pl.program_id(1)
    not_first = jnp.where(i > 0, 1.0, 0.0)
    x = x_ref[...]
    h_ext = _bf(jnp.concatenate([_rms(xh_ref[...], gpre_ref[...]), _rms(x, gpre_ref[...])], axis=0))
    row = lax.broadcasted_iota(jnp.int32, (tm + FFN_HALO, 1), 0)
    keep = jnp.where(row >= FFN_HALO, 1.0, not_first)

    def conv(ext_ref, c):
        w = cw_ref[c]
        return (w[0:1] * ext_ref[pl.ds(FFN_HALO - 2, tm), :] + w[1:2] * ext_ref[pl.ds(FFN_HALO - 1, tm), :]
                + w[2:3] * ext_ref[pl.ds(FFN_HALO, tm), :] + cb_ref[c])

    def chunk(c, carry):
        gext_ref[...] = jnp.dot(h_ext, wup_ref[c], preferred_element_type=jnp.float32) * keep
        uext_ref[...] = jnp.dot(h_ext, wup_ref[n_chunks + c], preferred_element_type=jnp.float32) * keep
        act = jax.nn.gelu(conv(gext_ref, c), approximate=True) * conv(uext_ref, n_chunks + c)
        acc_ref[...] += jnp.dot(_bf(act), wdn_ref[c], preferred_element_type=jnp.float32)
        return carry

    acc_ref[...] = jnp.zeros_like(acc_ref)
    lax.fori_loop(0, n_chunks, chunk, 0)
    out_ref[...] = x + _rms(acc_ref[...], gpost_ref[...])


def _ffn(x, gpre, wup, cw, cb, wdn, gpost):
    bsz, seq, d = x.shape
    tm = min(TM_PROJ, seq)
    n_chunks = wdn.shape[0]
    chunk = wdn.shape[1]
    tok = pl.BlockSpec((None, tm, d), lambda b, i: (b, i, 0))
    halo = pl.BlockSpec((None, FFN_HALO, d), lambda b, i: (b, jnp.maximum(i * (tm // FFN_HALO) - 1, 0), 0))
    full = lambda a: pl.BlockSpec(a.shape, lambda b, i: (0,) * a.ndim)
    weights = [gpre, wup, cw, cb, wdn, gpost]
    return pl.pallas_call(
        functools.partial(_ffn_kernel, tm=tm, n_chunks=n_chunks),
        grid=(bsz, seq // tm),
        in_specs=[tok, halo] + [full(w) for w in weights],
        out_specs=tok,
        out_shape=jax.ShapeDtypeStruct((bsz, seq, d), jnp.float32),
        scratch_shapes=[pltpu.VMEM((tm + FFN_HALO, chunk), jnp.float32),
                        pltpu.VMEM((tm + FFN_HALO, chunk), jnp.float32),
                        pltpu.VMEM((tm, d), jnp.float32)],
        compiler_params=_cparams(2),
        name="ffn",
    )(x, x, *weights)


def _swap_halves(w):
    k = w.shape[1] // HEAD_DIM
    w3 = w.reshape(w.shape[0], k, HEAD_DIM)
    half = HEAD_DIM // 2
    return jnp.concatenate([-w3[..., half:], w3[..., :half]], axis=-1).reshape(w.shape)


def _prep_w_in(w):
    d_mix = 1024
    c_conv, c_nsa, c_sgu, c_pool = d_mix // 4, NSA_HEADS * HEAD_DIM, d_mix // 4, d_mix // 4
    sizes = (2 * c_conv, c_nsa, 6 * HEAD_DIM, 3 * NSA_HEADS, 2 * c_sgu, c_pool)
    offs = np.cumsum((0,) + sizes)
    wa, wq, wkv, wg, wc, wd = [w[:, offs[j]:offs[j + 1]] for j in range(6)]
    kv = [wkv[:, j * HEAD_DIM:(j + 1) * HEAD_DIM] for j in range(6)]
    wg_pad = jnp.pad(wg, ((0, 0), (0, LANES - wg.shape[1])))
    cols = [wa, wc, wd, wq, _swap_halves(wq), kv[0], kv[1],
            kv[2], _swap_halves(kv[2]), kv[4], _swap_halves(kv[4]), kv[3], kv[5], wg_pad]
    return _bf(jnp.concatenate(cols, axis=1))


def _prep_compress(pe_k, pe_v, ck_w1, ck_w2, cv_w1, cv_w2):
    half = CMP_LEN // 2

    def w1_part(w1k, w1v, lo):
        wk = w1k.reshape(CMP_LEN, HEAD_DIM, HEAD_DIM)[lo:lo + half]
        wv = w1v.reshape(CMP_LEN, HEAD_DIM, HEAD_DIM)[lo:lo + half]
        z = jnp.zeros_like(wk)
        top = jnp.concatenate([wk, z], axis=-1)
        bot = jnp.concatenate([z, wv], axis=-1)
        return jnp.concatenate([top, bot], axis=1).reshape(half * 2 * HEAD_DIM, 2 * HEAD_DIM)

    def pe_part(lo):
        return jnp.concatenate([pe_k[lo:lo + half], pe_v[lo:lo + half]], axis=1).reshape(1, half * 2 * HEAD_DIM)

    z = jnp.zeros_like(ck_w2)
    w2 = jnp.concatenate([jnp.concatenate([ck_w2, z], axis=1), jnp.concatenate([z, cv_w2], axis=1)], axis=0)
    return (pe_part(0), pe_part(half), _bf(w1_part(ck_w1, cv_w1, 0)), _bf(w1_part(ck_w1, cv_w1, half)), _bf(w2))


def _importance_matrix(seq):
    n_cmp = (seq - CMP_LEN) // CMP_STRIDE + 1
    n_sel = seq // SEL_LEN
    ratio = SEL_LEN // CMP_STRIDE
    n_ov = CMP_LEN // CMP_STRIDE
    n_pad = -(-n_cmp // LANES) * LANES
    assert n_sel <= LANES, "selection blocks must fit one lane group"
    mt = np.zeros((LANES, n_pad), np.float32)
    for j in range(n_sel):
        for m in range(ratio):
            for n in range(n_ov):
                c = ratio * j + m - n
                if 0 <= c < n_cmp:
                    mt[j, c] += 1.0
    return jnp.asarray(mt, MXU_DTYPE)


def _rope_tables(seq):
    half = HEAD_DIM // 2
    inv = ROPE_THETA ** (-jnp.arange(half, dtype=jnp.float32) * 2.0 / HEAD_DIM)
    ang = jnp.arange(seq, dtype=jnp.float32)[:, None] * inv[None, :]
    cos, sin = jnp.cos(ang), jnp.sin(ang)
    cos4 = jnp.concatenate([cos] * (2 * NSA_HEADS), axis=1)
    sin4 = jnp.concatenate([sin] * (2 * NSA_HEADS), axis=1)
    cs = jnp.concatenate([cos, cos, sin, sin], axis=1)
    return cos4, sin4, cs


def _block_diag(w):
    g, c, d = w.shape
    out = jnp.zeros((g * c, g * d), w.dtype)
    for j in range(g):
        out = out.at[j * c:(j + 1) * c, j * d:(j + 1) * d].set(w[j])
    return out


def kernel(x, norm_mix_pre, norm_mix_post, norm_ffn_pre, norm_ffn_post, w_in, w_out, conv_dw_w, conv_dw_b, conv_ln_g, conv_ln_b, nsa_pe_k, nsa_pe_v, nsa_ck_w1, nsa_ck_w2, nsa_cv_w1, nsa_cv_w2, sgu_ln_g, sgu_ln_b, sgu_w, sgu_b, pool_w, pool_scale, ffn_up, ffn_conv_w, ffn_conv_b, ffn_down):
    bsz, seq, d = x.shape
    depth = w_in.shape[0]
    ffn_dim = ffn_down.shape[1]
    n_chunks = ffn_dim // FFN_CHUNK
    cos4, sin4, cs = _rope_tables(seq)
    mt = _importance_matrix(seq)
    row = lambda v: v.reshape(1, -1)

    for l in range(depth):
        w_ext = _prep_w_in(w_in[l])
        za, zc, zd, qc, qr, kvc, ks, vs, kw, vw, gate = _in_proj(x, row(norm_mix_pre[l]), w_ext, cos4, sin4, cs)

        n_rows = seq // (CMP_LEN // 2)
        a_cmp = kvc.reshape(bsz, n_rows, (CMP_LEN // 2) * 2 * HEAD_DIM)
        kk, vv = _compress(a_cmp, *_prep_compress(nsa_pe_k[l], nsa_pe_v[l], nsa_ck_w1[l], nsa_ck_w2[l],
                                                   nsa_cv_w1[l], nsa_cv_w2[l]))
        y_nsa = _nsa(qc, qr, gate, kk, vv, mt, ks, vs, kw, vw)

        sw_cat = jnp.transpose(sgu_w[l], (1, 0, 2)).reshape(SGU_CHUNK, SGU_GROUPS * SGU_CHUNK)
        sb_exp = jnp.repeat(sgu_b[l].T, 256 // SGU_GROUPS, axis=1)
        x = _mix_out(x, za, zc, zd, y_nsa,
                     conv_dw_w[l], row(conv_dw_b[l]), row(conv_ln_g[l]), row(conv_ln_b[l]),
                     row(sgu_ln_g[l]), row(sgu_ln_b[l]), sw_cat, sb_exp,
                     _bf(_block_diag(pool_w[l])), row(pool_scale[l]), _bf(w_out[l]), row(norm_mix_post[l]))

        wup = _bf(ffn_up[l]).reshape(d, 2 * n_chunks, FFN_CHUNK).transpose(1, 0, 2)
        cw = ffn_conv_w[l].reshape(FFN_CONV_K, 2 * n_chunks, FFN_CHUNK).transpose(1, 0, 2)
        cb = ffn_conv_b[l].reshape(2 * n_chunks, 1, FFN_CHUNK)
        wdn = _bf(ffn_down[l]).reshape(n_chunks, FFN_CHUNK, d)
        x = _ffn(x, row(norm_ffn_pre[l]), wup, cw, cb, wdn, row(norm_ffn_post[l]))
    return x
```

```python
import functools

import numpy as np
import jax
import jax.numpy as jnp
from jax import lax
from jax.experimental import pallas as pl
from jax.experimental.pallas import tpu as pltpu

HEAD_DIM = 64
NSA_HEADS = 4
CONV_K = 31
CMP_LEN = 32
CMP_STRIDE = 16
SEL_LEN = 64
SEL_TOPK = 16
WINDOW = 512
SGU_CHUNK = 128
SGU_GROUPS = 4
POOL_WINDOWS = (2, 4, 8, 16)
FFN_CONV_K = 3
ROPE_THETA = 10000.0
RMS_EPS = 1e-6
LN_EPS = 1e-5
NEG_INF = -1e30
FORCE_SCORE = 1e6
REMOVED = -3e38

LANES = 128
MXU_DTYPE = jnp.bfloat16
VMEM_LIMIT = 56 * 1024 * 1024

TM_PROJ = 512
TQ = 128
TK = 512
FFN_CHUNK = 256
CONV_HALO = 32
POOL_HALO = 16
FFN_HALO = 8

_NT = (((1,), (1,)), ((), ()))


def _cparams(n_axes):
    return pltpu.CompilerParams(dimension_semantics=("arbitrary",) * n_axes,
                                vmem_limit_bytes=VMEM_LIMIT)


def _rms(x, g):
    return x * lax.rsqrt(jnp.mean(x * x, axis=-1, keepdims=True) + RMS_EPS) * g


def _layer_norm(x, g, b):
    mu = jnp.mean(x, axis=-1, keepdims=True)
    var = jnp.mean(jnp.square(x - mu), axis=-1, keepdims=True)
    return (x - mu) * lax.rsqrt(var + LN_EPS) * g + b


def _bf(x):
    return x.astype(MXU_DTYPE)


def _swap64(x):
    return pltpu.roll(x, 64, axis=1)


def _in_proj_kernel(x_ref, g_ref, w_ref, cos_ref, sin_ref, cs_ref,
                    za_ref, zc_ref, zd_ref, qc_ref, qr_ref, kv_ref,
                    ks_ref, vs_ref, kw_ref, vw_ref, gate_ref, *, tm, tk):
    i = pl.program_id(1)
    h = _bf(_rms(x_ref[...], g_ref[...]))

    def proj(lo, hi):
        return jnp.dot(h, w_ref[:, lo:hi], preferred_element_type=jnp.float32)

    za_ref[...] = proj(0, 512)
    zc_ref[...] = proj(512, 1024)
    zd_ref[...] = proj(1024, 1280)
    q = proj(1280, 1536)
    q_sw = proj(1536, 1792)
    kv_ref[...] = proj(1792, 1920)
    ksl = proj(1920, 2048)
    kwn = proj(2048, 2176)
    vsw = proj(2176, 2304)
    gate_ref[...] = jax.nn.sigmoid(proj(2304, 2432))

    scale = HEAD_DIM ** -0.5
    qc_ref[...] = _bf(q * scale)
    qr_ref[...] = _bf((q * cos_ref[...] + q_sw * sin_ref[...]) * scale)

    cs = cs_ref[...]
    ts = ksl * cs
    k_rot2 = ts + _swap64(ts)
    tw = kwn * cs
    kw_rot2 = tw + _swap64(tw)
    pos = i * tm + lax.broadcasted_iota(jnp.int32, (tm, LANES), 0)
    lane = lax.broadcasted_iota(jnp.int32, (tm, LANES), 1)
    onehot = jnp.where((pos >> 6) == lane, 1.0, 0.0)
    ks_ref[:, 0:LANES] = _bf(onehot)
    ks_ref[:, LANES:2 * LANES] = _bf(k_rot2)
    kw_ref[...] = _bf(kw_rot2)
    ones_col = jnp.where(lane == 64, 1.0, 0.0)
    vs = jnp.where(lane < 64, vsw, ones_col)
    vw = jnp.where(lane < 64, _swap64(vsw), ones_col)
    for j in range(tm // tk):
        vs_ref[j] = _bf(vs[j * tk:(j + 1) * tk].T)
    for j in range(tm // TQ):
        vw_ref[j] = _bf(vw[j * TQ:(j + 1) * TQ].T)


def _in_proj(x, g, w_ext, cos4, sin4, cs):
    bsz, seq, d = x.shape
    tm = min(TM_PROJ, seq)
    n_ext = w_ext.shape[1]
    tk = min(TK, seq)
    tok = lambda c: pl.BlockSpec((None, tm, c), lambda b, i: (b, i, 0))
    f32, bf16 = jnp.float32, MXU_DTYPE

    def tok_out(c, dt):
        return tok(c), jax.ShapeDtypeStruct((bsz, seq, c), dt)

    def keytile_out(t):
        return (pl.BlockSpec((None, tm // t, LANES, t), lambda b, i: (b, i, 0, 0)),
                jax.ShapeDtypeStruct((bsz, seq // t, LANES, t), bf16))

    outs = [tok_out(512, f32), tok_out(512, f32), tok_out(256, f32), tok_out(256, bf16), tok_out(256, bf16),
            tok_out(128, f32), tok_out(256, bf16), keytile_out(tk), tok_out(128, bf16), keytile_out(TQ),
            tok_out(128, f32)]
    return pl.pallas_call(
        functools.partial(_in_proj_kernel, tm=tm, tk=tk),
        grid=(bsz, seq // tm),
        in_specs=[tok(d),
                  pl.BlockSpec((1, d), lambda b, i: (0, 0)),
                  pl.BlockSpec((d, n_ext), lambda b, i: (0, 0)),
                  pl.BlockSpec((tm, 256), lambda b, i: (i, 0)),
                  pl.BlockSpec((tm, 256), lambda b, i: (i, 0)),
                  pl.BlockSpec((tm, LANES), lambda b, i: (i, 0))],
        out_specs=[spec for spec, _ in outs],
        out_shape=[shape for _, shape in outs],
        compiler_params=_cparams(2),
        name="in_proj",
    )(x, g, w_ext, cos4, sin4, cs)


def _compress_kernel(a_ref, pe_lo_ref, pe_hi_ref, w_lo_ref, w_hi_ref, w2_ref, kk_ref, vv_ref):
    a = a_ref[...]
    u = jnp.dot(_bf(a + pe_lo_ref[...]), w_lo_ref[...], preferred_element_type=jnp.float32)
    v = jnp.dot(_bf(a + pe_hi_ref[...]), w_hi_ref[...], preferred_element_type=jnp.float32)
    n = a.shape[0]
    pre = u + pltpu.roll(v, n - 1, axis=0)
    act = jax.nn.gelu(pre, approximate=True)
    kv = jnp.dot(_bf(act), w2_ref[...], preferred_element_type=jnp.float32)
    vk = _swap64(kv)
    lane = lax.broadcasted_iota(jnp.int32, (n, LANES), 1)
    kk_ref[...] = _bf(jnp.where(lane < 64, kv, vk))
    vv = jnp.where(lane < 64, vk, jnp.where(lane == 64, 1.0, 0.0))
    vv_ref[...] = _bf(vv.T)


def _compress(a, pe_lo, pe_hi, w_lo, w_hi, w2):
    bsz, n, width = a.shape
    full = lambda shape: pl.BlockSpec(shape, lambda b: (0,) * len(shape))
    per_b = lambda c: pl.BlockSpec((None, n, c), lambda b: (b, 0, 0))
    return pl.pallas_call(
        _compress_kernel,
        grid=(bsz,),
        in_specs=[per_b(width), full((1, width)), full((1, width)),
                  full((width, LANES)), full((width, LANES)), full((LANES, LANES))],
        out_specs=[per_b(LANES), pl.BlockSpec((None, LANES, n), lambda b: (b, 0, 0))],
        out_shape=[jax.ShapeDtypeStruct((bsz, n, LANES), MXU_DTYPE),
                   jax.ShapeDtypeStruct((bsz, LANES, n), MXU_DTYPE)],
        compiler_params=_cparams(1),
        name="compress",
    )(a, pe_lo, pe_hi, w_lo, w_hi, w2)


def _nsa_kernel(qc_ref, qr_ref, gate_ref, kk_ref, vv_ref, mt_ref, ks_ref, vs_ref, kw_ref, vw_ref,
                out_ref, m_ref, acc_ref, s0_ref, s1_ref, *, tq, tk, seq):
    qi = pl.program_id(1)
    t0 = qi * tq
    rows = NSA_HEADS * tq
    n_cmp_pad = kk_ref.shape[0]
    top_n = min(SEL_TOPK, seq // SEL_LEN)
    n_sel = LANES
    win_keys = min(WINDOW + tq, seq)

    lane =lax.broadcasted_iota(jnp.int32, (tq, LANES), 1)

    def head_rows(q_ref):
        parts = []
        for hd in range(NSA_HEADS):
            pair = q_ref[:, (hd // 2) * LANES:(hd // 2 + 1) * LANES]
            keep = (lane < 64) if hd % 2 == 0 else (lane >= 64)
            parts.append(jnp.where(keep, pair, jnp.zeros_like(pair)))
        return jnp.concatenate(parts, axis=0)

    q_c = head_rows(qc_ref)
    q_r = head_rows(qr_ref)
    pos_l = t0 + (lax.broadcasted_iota(jnp.int32, (1, rows), 1) & (tq - 1))

    s = lax.dot_general(kk_ref[...], q_c, _NT, preferred_element_type=jnp.float32)
    n_idx = lax.broadcasted_iota(jnp.int32, (n_cmp_pad, rows), 0)
    last_blk = (pos_l - (CMP_LEN - 1)) >> 4
    sm = jnp.where(n_idx <= last_blk, s, NEG_INF)
    mx = jnp.max(sm, axis=0, keepdims=True)
    e = jnp.exp(sm - mx)
    den = jnp.sum(e, axis=0, keepdims=True)
    p_cmp = e * jnp.where(last_blk >= 0, 1.0 / den, 0.0)
    o_cmp = jnp.dot(vv_ref[...], _bf(p_cmp), preferred_element_type=jnp.float32)

    start = pl.multiple_of(jnp.maximum(t0 + tq - win_keys, 0), tq)
    sw = lax.dot_general(kw_ref[pl.ds(start, win_keys), :], q_r, _NT, preferred_element_type=jnp.float32)
    back = pos_l - (start + lax.broadcasted_iota(jnp.int32, (win_keys, rows), 0))
    in_win = lax.bitcast_convert_type(back, jnp.uint32) < WINDOW
    sw = jnp.where(in_win, sw, NEG_INF)
    mw = jnp.max(sw, axis=0, keepdims=True)
    pw = _bf(jnp.exp(sw - mw))
    wt0 = start // tq
    ow = jnp.dot(vw_ref[wt0], pw[0:tq], preferred_element_type=jnp.float32)
    for j in range(1, win_keys // tq):
        ow = ow + jnp.dot(vw_ref[wt0 + j], pw[j * tq:(j + 1) * tq], preferred_element_type=jnp.float32)
    o_win = ow / ow[64:65, :]

    p_sum = p_cmp[:, 0:tq] + p_cmp[:, tq:2 * tq] + p_cmp[:, 2 * tq:3 * tq] + p_cmp[:, 3 * tq:4 * tq]
    hi = _bf(p_sum)
    r1 = p_sum - hi.astype(jnp.float32)
    mid = _bf(r1)
    lo = _bf(r1 - mid.astype(jnp.float32))
    mt = mt_ref[...]
    imp = (jnp.dot(mt, hi, preferred_element_type=jnp.float32)
           + jnp.dot(mt, mid, preferred_element_type=jnp.float32)
           + jnp.dot(mt, lo, preferred_element_type=jnp.float32))

    blk = lax.broadcasted_iota(jnp.int32, (n_sel, tq), 0)
    blk_f = blk.astype(jnp.float32)
    pos_t = t0 + lax.broadcasted_iota(jnp.int32, (n_sel, tq), 1)
    cur = pos_t >> 6
    forced = (blk == 0) | (blk == cur) | (blk == cur - 1)
    valid = (blk * SEL_LEN) <= pos_t
    val = jnp.where(valid & jnp.logical_not(forced), imp, NEG_INF)
    picked = jnp.where(forced, 1.0, 0.0)
    for _ in range(top_n - 3):
        best = jnp.max(val, axis=0, keepdims=True)
        first = jnp.min(jnp.where(val == best, blk_f, float(n_sel)), axis=0, keepdims=True)
        hit = blk_f == first
        picked = jnp.where(hit, 1.0, picked)
        val = jnp.where(hit, REMOVED, val)
    bias_t = jnp.where((picked > 0.5) & valid, 0.0, NEG_INF)
    bias = _bf(bias_t.T)

    l_sel = jnp.concatenate(
        [jnp.concatenate([bias, q_r[hd * tq:(hd + 1) * tq]], axis=1) for hd in range(NSA_HEADS)], axis=0)
    m_ref[...] = jnp.full((1, rows), NEG_INF, jnp.float32)
    acc_ref[...] = jnp.zeros((LANES, rows), jnp.float32)

    def scores_to(s_ref, kt, causal):
        k0 = pl.multiple_of(kt * tk, tk)
        sc = lax.dot_general(ks_ref[pl.ds(k0, tk), :], l_sel, _NT, preferred_element_type=jnp.float32)
        if causal:
            kpos = k0 + lax.broadcasted_iota(jnp.int32, (tk, rows), 0)
            sc = jnp.where(kpos <= pos_l, sc, NEG_INF)
        s_ref[...] = sc

    def softmax_from(s_ref, kt):
        sc = s_ref[...]
        m_old = m_ref[...]
        m_new = jnp.maximum(m_old, jnp.max(sc, axis=0, keepdims=True))
        alpha = jnp.exp(m_old - m_new)
        p = jnp.exp(sc - m_new)
        acc_ref[...] = alpha * acc_ref[...] + jnp.dot(vs_ref[kt], _bf(p), preferred_element_type=jnp.float32)
        m_ref[...] = m_new

    n_full = t0 // tk
    n_pairs = n_full // 2
    scores_to(s0_ref, n_full, True)

    def pair(j, carry):
        k = 2 * j
        scores_to(s1_ref, k, False)
        softmax_from(s0_ref, jnp.where(j == 0, n_full, k - 1))
        scores_to(s0_ref, k + 1, False)
        softmax_from(s1_ref, k)
        return carry

    lax.fori_loop(0, n_pairs, pair, 0)
    pending = jnp.where(n_pairs == 0, n_full, 2 * n_pairs - 1)

    @pl.when(n_full % 2 == 1)
    def _():
        scores_to(s1_ref, n_full - 1, False)
        softmax_from(s0_ref, pending)
        softmax_from(s1_ref, n_full - 1)

    @pl.when(n_full % 2 == 0)
    def _():
        softmax_from(s0_ref, pending)

    acc = acc_ref[...]
    o_slc = acc / acc[64:65, :]

    gate_t = gate_ref[...].T
    y_t = []
    for hd in range(NSA_HEADS):
        c = slice(hd * tq, (hd + 1) * tq)
        y_t.append(gate_t[3 * hd:3 * hd + 1] * o_cmp[0:64, c]
                   + gate_t[3 * hd + 1:3 * hd + 2] * o_slc[0:64, c]
                   + gate_t[3 * hd + 2:3 * hd + 3] * o_win[0:64, c])
    out_ref[:, 0:LANES] = jnp.concatenate(y_t[0:2], axis=0).T
    out_ref[:, LANES:2 * LANES] = jnp.concatenate(y_t[2:4], axis=0).T


def _nsa(qc, qr, gate, kk, vv, mt, ks, vs, kw, vw):
    bsz, seq, _ = qc.shape
    tq = min(TQ, seq)
    tk = min(TK, seq)
    n_cmp_pad = kk.shape[1]
    rows = NSA_HEADS * tq
    tile = lambda c: pl.BlockSpec((None, tq, c), lambda b, i: (b, i, 0))
    per_b = lambda r, c: pl.BlockSpec((None, r, c), lambda b, i: (b, 0, 0))
    per_b4 = lambda a: pl.BlockSpec((None,) + a.shape[1:], lambda b, i: (b, 0, 0, 0))
    return pl.pallas_call(
        functools.partial(_nsa_kernel, tq=tq, tk=tk, seq=seq),
        grid=(bsz, seq // tq),
        in_specs=[tile(256), tile(256), tile(LANES),
                  per_b(n_cmp_pad, LANES), per_b(LANES, n_cmp_pad),
                  pl.BlockSpec(mt.shape, lambda b, i: (0, 0)),
                  per_b(seq, 256), per_b4(vs), per_b(seq, LANES), per_b4(vw)],
        out_specs=tile(256),
        out_shape=jax.ShapeDtypeStruct((bsz, seq, 256), jnp.float32),
        scratch_shapes=[pltpu.VMEM((1, rows), jnp.float32), pltpu.VMEM((LANES, rows), jnp.float32),
                        pltpu.VMEM((tk, rows), jnp.float32), pltpu.VMEM((tk, rows), jnp.float32)],
        compiler_params=_cparams(2),
        name="nsa",
    )(qc, qr, gate, kk, vv, mt, ks, vs, kw, vw)


def _mix_out_kernel(x_ref, za_ref, zah_ref, zc_ref, zd_ref, zdh_ref, ynsa_ref,
                    cw_ref, cb_ref, clg_ref, clb_ref, slg_ref, slb_ref, sw_ref, sb_ref,
                    pw_ref, ps_ref, wout_ref, gpost_ref, out_ref,
                    uext_ref, yconv_ref, pext_ref, *, tm):
    i = pl.program_id(1)
    not_first = jnp.where(i > 0, 1.0, 0.0)

    def glu(z):
        return z[:, :256] * jax.nn.sigmoid(z[:, 256:])

    uext_ref[0:CONV_HALO, :] = glu(zah_ref[...]) * not_first
    uext_ref[CONV_HALO:, :] = glu(za_ref[...])
    rc = 64
    for r0 in range(0, tm, rc):
        acc = jnp.zeros((rc, 256), jnp.float32)
        for k in range(CONV_K):
            acc = acc + cw_ref[k:k + 1, :] * uext_ref[pl.ds(r0 + CONV_HALO - (CONV_K - 1) + k, rc), :]
        yc = _layer_norm(acc + cb_ref[...], clg_ref[...], clb_ref[...])
        yconv_ref[r0:r0 + rc, :] = _bf(yc * jax.nn.sigmoid(yc))

    zc = jax.nn.gelu(zc_ref[...], approximate=True)
    u_s = zc[:, :256]
    v_s = _layer_norm(zc[:, 256:], slg_ref[...], slb_ref[...])
    t_i = lax.broadcasted_iota(jnp.int32, (SGU_CHUNK, SGU_GROUPS * SGU_CHUNK), 0)
    s_i = lax.rem(lax.broadcasted_iota(jnp.int32, (SGU_CHUNK, SGU_GROUPS * SGU_CHUNK), 1), SGU_CHUNK)
    w_cat = _bf(jnp.where(s_i <= t_i, sw_ref[...], 0.0))
    grp = lax.broadcasted_iota(jnp.int32, (SGU_CHUNK, 256), 1) >> 6
    ysgu = []
    for c in range(tm // SGU_CHUNK):
        vc = v_s[c * SGU_CHUNK:(c + 1) * SGU_CHUNK]
        stacked = _bf(jnp.concatenate([jnp.where(grp == g, vc, 0.0) for g in range(SGU_GROUPS)], axis=0))
        f = jnp.dot(w_cat, stacked, preferred_element_type=jnp.float32) + sb_ref[...]
        ysgu.append(u_s[c * SGU_CHUNK:(c + 1) * SGU_CHUNK] * f)
    y_sgu = _bf(jnp.concatenate(ysgu, axis=0))

    pext_ref[0:POOL_HALO, :] = zdh_ref[...] * not_first
    pext_ref[POOL_HALO:, :] = zd_ref[...]
    n_ext = tm + POOL_HALO
    sums = []
    for shift in (1, 2, 4, 8):
        n = n_ext - shift
        nxt = pext_ref[pl.ds(shift, n), :] + pext_ref[pl.ds(0, n), :]
        pext_ref[pl.ds(shift, n), :] = nxt
        sums.append(pext_ref[pl.ds(POOL_HALO, tm), :])
    lane_grp = lax.broadcasted_iota(jnp.int32, (tm, 256), 1) >> 6
    pos = i * tm + lax.broadcasted_iota(jnp.int32, (tm, 256), 0)
    wsum = jnp.where(lane_grp == 0, sums[0], jnp.where(lane_grp == 1, sums[1],
                     jnp.where(lane_grp == 2, sums[2], sums[3])))
    width = jnp.where(lane_grp == 0, POOL_WINDOWS[0], jnp.where(lane_grp == 1, POOL_WINDOWS[1],
                      jnp.where(lane_grp == 2, POOL_WINDOWS[2], POOL_WINDOWS[3])))
    cnt = jnp.minimum(pos + 1, width).astype(jnp.float32)
    pooled = wsum / cnt - zd_ref[...]
    y_pool = _bf(jnp.dot(_bf(pooled), pw_ref[...], preferred_element_type=jnp.float32) * ps_ref[...])

    y = (jnp.dot(yconv_ref[...], wout_ref[0:256, :], preferred_element_type=jnp.float32)
         + jnp.dot(_bf(ynsa_ref[...]), wout_ref[256:512, :], preferred_element_type=jnp.float32)
         + jnp.dot(y_sgu, wout_ref[512:768, :], preferred_element_type=jnp.float32)
         + jnp.dot(y_pool, wout_ref[768:1024, :], preferred_element_type=jnp.float32))
    out_ref[...] = x_ref[...] + _rms(y, gpost_ref[...])


def _mix_out(x, za, zc, zd, ynsa, cw, cb, clg, clb, slg, slb, sw_cat, sb_exp, pw_bd, ps, wout, gpost):
    bsz, seq, d = x.shape
    tm = min(TM_PROJ, seq)
    tok = lambda c: pl.BlockSpec((None, tm, c), lambda b, i: (b, i, 0))
    halo = lambda c, h: pl.BlockSpec((None, h, c), lambda b, i: (b, jnp.maximum(i * (tm // h) - 1, 0), 0))
    full = lambda a: pl.BlockSpec(a.shape, lambda b, i: (0,) * a.ndim)
    weights = [cw, cb, clg, clb, slg, slb, sw_cat, sb_exp, pw_bd, ps, wout, gpost]
    return pl.pallas_call(
        functools.partial(_mix_out_kernel, tm=tm),
        grid=(bsz, seq // tm),
        in_specs=[tok(d), tok(512), halo(512, CONV_HALO), tok(512), tok(256), halo(256, POOL_HALO), tok(256)]
                 + [full(w) for w in weights],
        out_specs=tok(d),
        out_shape=jax.ShapeDtypeStruct((bsz, seq, d), jnp.float32),
        scratch_shapes=[pltpu.VMEM((tm + CONV_HALO, 256), jnp.float32),
                        pltpu.VMEM((tm, 256), MXU_DTYPE),
                        pltpu.VMEM((tm + POOL_HALO, 256), jnp.float32)],
        compiler_params=_cparams(2),
        name="mix_out",
    )(x, za, za, zc, zd, zd, ynsa, *weights)


def _ffn_kernel(x_ref, xh_ref, gpre_ref, wup_ref, cw_ref, cb_ref, wdn_ref, gpost_ref, out_ref,
                gext_ref, uext_ref, acc_ref, *, tm, n_chunks):
    i = pl.program_id(1)
    not_first = jnp.where(i > 0, 1.0, 0.0)
    x = x_ref[...]
    h_ext = _bf(jnp.concatenate([_rms(xh_ref[...], gpre_ref[...]), _rms(x, gpre_ref[...])], axis=0))
    row = lax.broadcasted_iota(jnp.int32, (tm + FFN_HALO, 1), 0)
    keep = jnp.where(row >= FFN_HALO, 1.0, not_first)

    def conv(ext_ref, c):
        w = cw_ref[c]
        return (w[0:1] * ext_ref[pl.ds(FFN_HALO - 2, tm), :] + w[1:2] * ext_ref[pl.ds(FFN_HALO - 1, tm), :]
                + w[2:3] * ext_ref[pl.ds(FFN_HALO, tm), :] + cb_ref[c])

    def chunk(c, carry):
        gext_ref[...] = jnp.dot(h_ext, wup_ref[c], preferred_element_type=jnp.float32) * keep
        uext_ref[...] = jnp.dot(h_ext, wup_ref[n_chunks + c], preferred_element_type=jnp.float32) * keep
        act = jax.nn.gelu(conv(gext_ref, c), approximate=True) * conv(uext_ref, n_chunks + c)
        acc_ref[...] += jnp.dot(_bf(act), wdn_ref[c], preferred_element_type=jnp.float32)
        return carry

    acc_ref[...] = jnp.zeros_like(acc_ref)
    lax.fori_loop(0, n_chunks, chunk, 0)
    out_ref[...] = x + _rms(acc_ref[...], gpost_ref[...])


def _ffn(x, gpre, wup, cw, cb, wdn, gpost):
    bsz, seq, d = x.shape
    tm = min(TM_PROJ, seq)
    n_chunks = wdn.shape[0]
    chunk = wdn.shape[1]
    tok = pl.BlockSpec((None, tm, d), lambda b, i: (b, i, 0))
    halo = pl.BlockSpec((None, FFN_HALO, d), lambda b, i: (b, jnp.maximum(i * (tm // FFN_HALO) - 1, 0), 0))
    full = lambda a: pl.BlockSpec(a.shape, lambda b, i: (0,) * a.ndim)
    weights = [gpre, wup, cw, cb, wdn, gpost]
    return pl.pallas_call(
        functools.partial(_ffn_kernel, tm=tm, n_chunks=n_chunks),
        grid=(bsz, seq // tm),
        in_specs=[tok, halo] + [full(w) for w in weights],
        out_specs=tok,
        out_shape=jax.ShapeDtypeStruct((bsz, seq, d), jnp.float32),
        scratch_shapes=[pltpu.VMEM((tm + FFN_HALO, chunk), jnp.float32),
                        pltpu.VMEM((tm + FFN_HALO, chunk), jnp.float32),
                        pltpu.VMEM((tm, d), jnp.float32)],
        compiler_params=_cparams(2),
        name="ffn",
    )(x, x, *weights)


def _swap_halves(w):
    k = w.shape[1] // HEAD_DIM
    w3 = w.reshape(w.shape[0], k, HEAD_DIM)
    half = HEAD_DIM // 2
    return jnp.concatenate([-w3[..., half:], w3[..., :half]], axis=-1).reshape(w.shape)


def _prep_w_in(w):
    d_mix = 1024
    c_conv, c_nsa, c_sgu, c_pool = d_mix // 4, NSA_HEADS * HEAD_DIM, d_mix // 4, d_mix // 4
    sizes = (2 * c_conv, c_nsa, 6 * HEAD_DIM, 3 * NSA_HEADS, 2 * c_sgu, c_pool)
    offs = np.cumsum((0,) + sizes)
    wa, wq, wkv, wg, wc, wd = [w[:, offs[j]:offs[j + 1]] for j in range(6)]
    kv = [wkv[:, j * HEAD_DIM:(j + 1) * HEAD_DIM] for j in range(6)]
    wg_pad = jnp.pad(wg, ((0, 0), (0, LANES - wg.shape[1])))
    cols = [wa, wc, wd, wq, _swap_halves(wq), kv[0], kv[1],
            kv[2], _swap_halves(kv[2]), kv[4], _swap_halves(kv[4]), kv[3], kv[5], wg_pad]
    return _bf(jnp.concatenate(cols, axis=1))


def _prep_compress(pe_k, pe_v, ck_w1, ck_w2, cv_w1, cv_w2):
    half = CMP_LEN // 2

    def w1_part(w1k, w1v, lo):
        wk = w1k.reshape(CMP_LEN, HEAD_DIM, HEAD_DIM)[lo:lo + half]
        wv = w1v.reshape(CMP_LEN, HEAD_DIM, HEAD_DIM)[lo:lo + half]
        z = jnp.zeros_like(wk)
        top = jnp.concatenate([wk, z], axis=-1)
        bot = jnp.concatenate([z, wv], axis=-1)
        return jnp.concatenate([top, bot], axis=1).reshape(half * 2 * HEAD_DIM, 2 * HEAD_DIM)

    def pe_part(lo):
        return jnp.concatenate([pe_k[lo:lo + half], pe_v[lo:lo + half]], axis=1).reshape(1, half * 2 * HEAD_DIM)

    z = jnp.zeros_like(ck_w2)
    w2 = jnp.concatenate([jnp.concatenate([ck_w2, z], axis=1), jnp.concatenate([z, cv_w2], axis=1)], axis=0)
    return (pe_part(0), pe_part(half), _bf(w1_part(ck_w1, cv_w1, 0)), _bf(w1_part(ck_w1, cv_w1, half)), _bf(w2))


def _importance_matrix(seq):
    n_cmp = (seq - CMP_LEN) // CMP_STRIDE + 1
    n_sel = seq // SEL_LEN
    ratio = SEL_LEN // CMP_STRIDE
    n_ov = CMP_LEN // CMP_STRIDE
    n_pad = -(-n_cmp // LANES) * LANES
    assert n_sel <= LANES, "selection blocks must fit one lane group"
    mt = np.zeros((LANES, n_pad), np.float32)
    for j in range(n_sel):
        for m in range(ratio):
            for n in range(n_ov):
                c = ratio * j + m - n
                if 0 <= c < n_cmp:
                    mt[j, c] += 1.0
    return jnp.asarray(mt, MXU_DTYPE)


def _rope_tables(seq):
    half = HEAD_DIM // 2
    inv = ROPE_THETA ** (-jnp.arange(half, dtype=jnp.float32) * 2.0 / HEAD_DIM)
    ang = jnp.arange(seq, dtype=jnp.float32)[:, None] * inv[None, :]
    cos, sin = jnp.cos(ang), jnp.sin(ang)
    cos4 = jnp.concatenate([cos] * (2 * NSA_HEADS), axis=1)
    sin4 = jnp.concatenate([sin] * (2 * NSA_HEADS), axis=1)
    cs = jnp.concatenate([cos, cos, sin, sin], axis=1)
    return cos4, sin4, cs


def _block_diag(w):
    g, c, d = w.shape
    out = jnp.zeros((g * c, g * d), w.dtype)
    for j in range(g):
        out = out.at[j * c:(j + 1) * c, j * d:(j + 1) * d].set(w[j])
    return out


def kernel(x, norm_mix_pre, norm_mix_post, norm_ffn_pre, norm_ffn_post, w_in, w_out, conv_dw_w, conv_dw_b, conv_ln_g, conv_ln_b, nsa_pe_k, nsa_pe_v, nsa_ck_w1, nsa_ck_w2, nsa_cv_w1, nsa_cv_w2, sgu_ln_g, sgu_ln_b, sgu_w, sgu_b, pool_w, pool_scale, ffn_up, ffn_conv_w, ffn_conv_b, ffn_down):
    bsz, seq, d = x.shape
    depth = w_in.shape[0]
    ffn_dim = ffn_down.shape[1]
    n_chunks = ffn_dim // FFN_CHUNK
    cos4, sin4, cs = _rope_tables(seq)
    mt = _importance_matrix(seq)
    row = lambda v: v.reshape(1, -1)

    for l in range(depth):
        w_ext = _prep_w_in(w_in[l])
        za, zc, zd, qc, qr, kvc, ks, vs, kw, vw, gate = _in_proj(x, row(norm_mix_pre[l]), w_ext, cos4, sin4, cs)

        n_rows = seq // (CMP_LEN // 2)
        a_cmp = kvc.reshape(bsz, n_rows, (CMP_LEN // 2) * 2 * HEAD_DIM)
        kk, vv = _compress(a_cmp, *_prep_compress(nsa_pe_k[l], nsa_pe_v[l], nsa_ck_w1[l], nsa_ck_w2[l],
                                                   nsa_cv_w1[l], nsa_cv_w2[l]))
        y_nsa = _nsa(qc, qr, gate, kk, vv, mt, ks, vs, kw, vw)

        sw_cat = jnp.transpose(sgu_w[l], (1, 0, 2)).reshape(SGU_CHUNK, SGU_GROUPS * SGU_CHUNK)
        sb_exp = jnp.repeat(sgu_b[l].T, 256 // SGU_GROUPS, axis=1)
        x = _mix_out(x, za, zc, zd, y_nsa,
                     conv_dw_w[l], row(conv_dw_b[l]), row(conv_ln_g[l]), row(conv_ln_b[l]),
                     row(sgu_ln_g[l]), row(sgu_ln_b[l]), sw_cat, sb_exp,
                     _bf(_block_diag(pool_w[l])), row(pool_scale[l]), _bf(w_out[l]), row(norm_mix_post[l]))

        wup = _bf(ffn_up[l]).reshape(d, 2 * n_chunks, FFN_CHUNK).transpose(1, 0, 2)
        cw = ffn_conv_w[l].reshape(FFN_CONV_K, 2 * n_chunks, FFN_CHUNK).transpose(1, 0, 2)
        cb = ffn_conv_b[l].reshape(2 * n_chunks, 1, FFN_CHUNK)
        wdn = _bf(ffn_down[l]).reshape(n_chunks, FFN_CHUNK, d)
        x = _ffn(x, row(norm_ffn_pre[l]), wup, cw, cb, wdn, row(norm_ffn_post[l]))
    return x
```

```python
import functools

import numpy as np
import jax
import jax.numpy as jnp
from jax import lax
from jax.experimental import pallas as pl
from jax.experimental.pallas import tpu as pltpu

HEAD_DIM = 64
NSA_HEADS = 4
CONV_K = 31
CMP_LEN = 32
CMP_STRIDE = 16
SEL_LEN = 64
SEL_TOPK = 16
WINDOW = 512
SGU_CHUNK = 128
SGU_GROUPS = 4
POOL_WINDOWS = (2, 4, 8, 16)
FFN_CONV_K = 3
ROPE_THETA = 10000.0
RMS_EPS = 1e-6
LN_EPS = 1e-5
NEG_INF = -1e30
FORCE_SCORE = 1e6
REMOVED = -3e38

LANES = 128
SUBLANES = 8
MXU_DTYPE = jnp.bfloat16
VMEM_LIMIT = 56 * 1024 * 1024

TM_PROJ = 512
TQ = 256
TK = 512
FFN_CHUNK = 256
CONV_HALO = 32
POOL_HALO = 16
FFN_HALO = 16

_NT = (((1,), (1,)), ((), ()))


def _cparams(n_axes, flags=None):
    return pltpu.CompilerParams(dimension_semantics=("arbitrary",) * n_axes,
                                vmem_limit_bytes=VMEM_LIMIT, flags=flags)


def _layer_spec(a, l):
    return pl.BlockSpec((None,) + a.shape[1:], lambda *_: (l,) + (0,) * (a.ndim - 1))


def _rms(x, g):
    return x * lax.rsqrt(jnp.mean(x * x, axis=-1, keepdims=True) + RMS_EPS) * g


def _layer_norm(x, g, b):
    mu = jnp.mean(x, axis=-1, keepdims=True)
    var = jnp.mean(jnp.square(x - mu), axis=-1, keepdims=True)
    return (x - mu) * lax.rsqrt(var + LN_EPS) * g + b


def _bf(x):
    return x.astype(MXU_DTYPE)


def _swap64(x):
    return pltpu.roll(x, 64, axis=1)


def _in_proj_kernel(x_ref, g_ref, w_ref, cs_ref,
                    za_ref, zc_ref, zd_ref, qc_ref, qr_ref, kv_ref,
                    ks_ref, vs_ref, kw_ref, vw_ref, gate_ref, *, tm, tk):
    i = pl.program_id(1)
    h = _bf(_rms(x_ref[...], g_ref[...]))

    def proj(lo, hi):
        return jnp.dot(h, w_ref[:, lo:hi], preferred_element_type=jnp.float32)

    za_ref[...] = proj(0, 512)
    zc_ref[...] = proj(512, 1024)
    zd_ref[...] = proj(1024, 1280)
    q = proj(1280, 1536)
    q_sw = proj(1536, 1792)
    kv_ksl = proj(1792, 2048)
    kwn_vsw = proj(2048, 2304)
    kv_ref[...] = kv_ksl[:, :LANES]
    ksl = kv_ksl[:, LANES:]
    kwn = kwn_vsw[:, :LANES]
    vsw = kwn_vsw[:, LANES:]
    gate_ref[...] = jax.nn.sigmoid(proj(2304, 2432))

    cs = cs_ref[...]
    lane = lax.broadcasted_iota(jnp.int32, (tm, LANES), 1)
    sc = _swap64(cs)
    cos2 = jnp.where(lane < 64, cs, sc)
    sin2 = jnp.where(lane < 64, sc, cs)
    scale = HEAD_DIM ** -0.5
    qc_ref[...] = _bf(q * scale)
    for pair in range(NSA_HEADS // 2):
        c = slice(pair * LANES, (pair + 1) * LANES)
        qr_ref[:, c] = _bf((q[:, c] * cos2 + q_sw[:, c] * sin2) * scale)

    ts = ksl * cs
    k_rot2 = ts + _swap64(ts)
    tw = kwn * cs
    kw_rot2 = tw + _swap64(tw)
    pos = i * tm + lax.broadcasted_iota(jnp.int32, (tm, LANES), 0)
    onehot =jnp.where((pos >> 6) == lane, 1.0, 0.0)
    ks_ref[:, 0:LANES] = _bf(onehot)
    ks_ref[:, LANES:2 * LANES] = _bf(k_rot2)
    kw_ref[...] = _bf(kw_rot2)
    ones_col = jnp.where(lane == 64, 1.0, 0.0)
    vs = jnp.where(lane < 64, vsw, ones_col)
    vw = jnp.where(lane < 64, _swap64(vsw), ones_col)
    for j in range(tm // tk):
        vs_ref[j] = _bf(vs[j * tk:(j + 1) * tk].T)
    for j in range(tm // TQ):
        vw_ref[j] = _bf(vw[j * TQ:(j + 1) * TQ].T)


def _in_proj(l, x, g, w_ext, cs):
    bsz, seq, d = x.shape
    tm = min(TM_PROJ, seq)
    tk = min(TK, seq)
    tok = lambda c: pl.BlockSpec((None, tm, c), lambda b, i: (b, i, 0))
    f32, bf16 = jnp.float32, MXU_DTYPE

    def tok_out(c, dt):
        return tok(c), jax.ShapeDtypeStruct((bsz, seq, c), dt)

    def keytile_out(t):
        return (pl.BlockSpec((None, tm // t, LANES, t), lambda b, i: (b, i, 0, 0)),
                jax.ShapeDtypeStruct((bsz, seq // t, LANES, t), bf16))

    outs = [tok_out(512, f32), tok_out(512, f32), tok_out(256, f32), tok_out(256, bf16), tok_out(256, bf16),
            tok_out(128, f32), tok_out(256, bf16), keytile_out(tk), tok_out(128, bf16), keytile_out(TQ),
            tok_out(128, f32)]
    return pl.pallas_call(
        functools.partial(_in_proj_kernel, tm=tm, tk=tk),
        grid=(bsz, seq // tm),
        in_specs=[tok(d), _layer_spec(g, l), _layer_spec(w_ext, l),
                  pl.BlockSpec((tm, LANES), lambda b, i: (i, 0))],
        out_specs=[spec for spec, _ in outs],
        out_shape=[shape for _, shape in outs],
        compiler_params=_cparams(2),
        name="in_proj",
    )(x, g, w_ext, cs)


def _compress_kernel(a_ref, pe_lo_ref, pe_hi_ref, w_lo_ref, w_hi_ref, w2_ref, kk_ref, vv_ref):
    a = a_ref[...]
    u = jnp.dot(_bf(a + pe_lo_ref[...]), w_lo_ref[...], preferred_element_type=jnp.float32)
    v = jnp.dot(_bf(a + pe_hi_ref[...]), w_hi_ref[...], preferred_element_type=jnp.float32)
    n = a.shape[0]
    pre = u + pltpu.roll(v, n - 1, axis=0)
    act = jax.nn.gelu(pre, approximate=True)
    kv = jnp.dot(_bf(act), w2_ref[...], preferred_element_type=jnp.float32)
    vk = _swap64(kv)
    lane = lax.broadcasted_iota(jnp.int32, (n, LANES), 1)
    kk_ref[...] = _bf(jnp.where(lane < 64, kv, vk))
    vv = jnp.where(lane < 64, vk, jnp.where(lane == 64, 1.0, 0.0))
    vv_ref[...] = _bf(vv.T)


def _compress(l, a, pe_lo, pe_hi, w_lo, w_hi, w2):
    bsz, n, width = a.shape
    per_b = lambda c: pl.BlockSpec((None, n, c), lambda b: (b, 0, 0))
    return pl.pallas_call(
        _compress_kernel,
        grid=(bsz,),
        in_specs=[per_b(width)] + [_layer_spec(w, l) for w in (pe_lo, pe_hi, w_lo, w_hi, w2)],
        out_specs=[per_b(LANES), pl.BlockSpec((None, LANES, n), lambda b: (b, 0, 0))],
        out_shape=[jax.ShapeDtypeStruct((bsz, n, LANES), MXU_DTYPE),
                   jax.ShapeDtypeStruct((bsz, LANES, n), MXU_DTYPE)],
        compiler_params=_cparams(1),
        name="compress",
    )(a, pe_lo, pe_hi, w_lo, w_hi, w2)


def _nsa_kernel(qc_ref, qr_ref, gate_ref, kk_ref, vv_ref, mt_ref, ks_ref, vs_ref, kw_ref, vw_ref,
                out_ref, m_ref, acc_ref, s0_ref, s1_ref, *, tq, tk, seq):
    qi = pl.program_id(1)
    t0 = qi * tq
    rows = NSA_HEADS * tq
    n_cmp_pad = kk_ref.shape[0]
    top_n = min(SEL_TOPK, seq // SEL_LEN)
    n_sel = LANES
    win_keys = min(WINDOW + tq, seq)

    lane =lax.broadcasted_iota(jnp.int32, (tq, LANES), 1)

    def head_rows(q_ref):
        parts = []
        for hd in range(NSA_HEADS):
            pair = q_ref[:, (hd // 2) * LANES:(hd // 2 + 1) * LANES]
            keep = (lane < 64) if hd % 2 == 0 else (lane >= 64)
            parts.append(jnp.where(keep, pair, jnp.zeros_like(pair)))
        return jnp.concatenate(parts, axis=0)

    q_c = head_rows(qc_ref)
    q_r = head_rows(qr_ref)
    pos_l = t0 + (lax.broadcasted_iota(jnp.int32, (1, rows), 1) & (tq - 1))

    s = lax.dot_general(kk_ref[...], q_c, _NT, preferred_element_type=jnp.float32)
    n_idx = lax.broadcasted_iota(jnp.int32, (n_cmp_pad, rows), 0)
    last_blk = (pos_l - (CMP_LEN - 1)) >> 4
    sm = jnp.where(n_idx <= last_blk, s, NEG_INF)
    mx = jnp.max(sm, axis=0, keepdims=True)
    e = jnp.exp(sm - mx)
    den = jnp.sum(e, axis=0, keepdims=True)
    p_cmp = e * jnp.where(last_blk >= 0, 1.0 / den, 0.0)
    o_cmp = jnp.dot(vv_ref[...], _bf(p_cmp), preferred_element_type=jnp.float32)

    start = pl.multiple_of(jnp.maximum(t0 + tq - win_keys, 0), tq)
    sw = lax.dot_general(kw_ref[pl.ds(start, win_keys), :], q_r, _NT, preferred_element_type=jnp.float32)
    back = pos_l - (start + lax.broadcasted_iota(jnp.int32, (win_keys, rows), 0))
    in_win = lax.bitcast_convert_type(back, jnp.uint32) < WINDOW
    sw = jnp.where(in_win, sw, NEG_INF)
    mw = jnp.max(sw, axis=0, keepdims=True)
    pw = _bf(jnp.exp(sw - mw))
    wt0 = start // tq
    ow = jnp.dot(vw_ref[wt0], pw[0:tq], preferred_element_type=jnp.float32)
    for j in range(1, win_keys // tq):
        ow = ow + jnp.dot(vw_ref[wt0 + j], pw[j * tq:(j + 1) * tq], preferred_element_type=jnp.float32)
    o_win = ow / ow[64:65, :]

    p_sum = p_cmp[:, 0:tq] + p_cmp[:, tq:2 * tq] + p_cmp[:, 2 * tq:3 * tq] + p_cmp[:, 3 * tq:4 * tq]
    hi = _bf(p_sum)
    r1 = p_sum - hi.astype(jnp.float32)
    mid = _bf(r1)
    lo = _bf(r1 - mid.astype(jnp.float32))
    mt = mt_ref[...]
    imp = (jnp.dot(mt, hi, preferred_element_type=jnp.float32)
           + jnp.dot(mt, mid, preferred_element_type=jnp.float32)
           + jnp.dot(mt, lo, preferred_element_type=jnp.float32))

    blk = lax.broadcasted_iota(jnp.int32, (n_sel, tq), 0)
    blk_f = blk.astype(jnp.float32)
    pos_t = t0 + lax.broadcasted_iota(jnp.int32, (n_sel, tq), 1)
    cur = pos_t >> 6
    forced = (blk == 0) | (blk == cur) | (blk == cur - 1)
    valid = (blk * SEL_LEN) <= pos_t
    val = jnp.where(valid & jnp.logical_not(forced), imp, NEG_INF)
    picked = jnp.where(forced, 1.0, 0.0)
    for _ in range(top_n - 3):
        best = jnp.max(val, axis=0, keepdims=True)
        first = jnp.min(jnp.where(val == best, blk_f, float(n_sel)), axis=0, keepdims=True)
        hit = blk_f == first
        picked = jnp.where(hit, 1.0, picked)
        val = jnp.where(hit, REMOVED, val)
    bias_t = jnp.where((picked > 0.5) & valid, 0.0, NEG_INF)
    bias = _bf(bias_t.T)

    l_sel = jnp.concatenate(
        [jnp.concatenate([bias, q_r[hd * tq:(hd + 1) * tq]], axis=1) for hd in range(NSA_HEADS)], axis=0)
    m_ref[...] = jnp.full((1, rows), NEG_INF, jnp.float32)
    acc_ref[...] = jnp.zeros((LANES, rows), jnp.float32)

    def scores_to(s_ref, kt, causal):
        k0 = pl.multiple_of(kt * tk, tk)
        sc = lax.dot_general(ks_ref[pl.ds(k0, tk), :], l_sel, _NT, preferred_element_type=jnp.float32)
        if causal:
            kpos = k0 + lax.broadcasted_iota(jnp.int32, (tk, rows), 0)
            sc = jnp.where(kpos <= pos_l, sc, NEG_INF)
        s_ref[...] = sc

    def softmax_from(s_ref, kt):
        sc = s_ref[...]
        m_old = m_ref[...]
        m_new = jnp.maximum(m_old, jnp.max(sc, axis=0, keepdims=True))
        alpha = jnp.exp(m_old - m_new)
        p = jnp.exp(sc - m_new)
        acc_ref[...] = alpha * acc_ref[...] + jnp.dot(vs_ref[kt], _bf(p), preferred_element_type=jnp.float32)
        m_ref[...] = m_new

    n_full = t0 // tk
    n_pairs = n_full // 2
    scores_to(s0_ref, n_full, True)

    def pair(j, carry):
        k = 2 * j
        scores_to(s1_ref, k, False)
        softmax_from(s0_ref, jnp.where(j == 0, n_full, k - 1))
        scores_to(s0_ref, k + 1, False)
        softmax_from(s1_ref, k)
        return carry

    lax.fori_loop(0, n_pairs, pair, 0)
    pending = jnp.where(n_pairs == 0, n_full, 2 * n_pairs - 1)

    @pl.when(n_full % 2 == 1)
    def _():
        scores_to(s1_ref, n_full - 1, False)
        softmax_from(s0_ref, pending)
        softmax_from(s1_ref, n_full - 1)

    @pl.when(n_full % 2 == 0)
    def _():
        softmax_from(s0_ref, pending)

    acc = acc_ref[...]
    o_slc = acc / acc[64:65, :]

    gate_t = gate_ref[...].T
    y_t = []
    for hd in range(NSA_HEADS):
        c = slice(hd * tq, (hd + 1) * tq)
        y_t.append(gate_t[3 * hd:3 * hd + 1] * o_cmp[0:64, c]
                   + gate_t[3 * hd + 1:3 * hd + 2] * o_slc[0:64, c]
                   + gate_t[3 * hd + 2:3 * hd + 3] * o_win[0:64, c])
    out_ref[:, 0:LANES] = jnp.concatenate(y_t[0:2], axis=0).T
    out_ref[:, LANES:2 * LANES] = jnp.concatenate(y_t[2:4], axis=0).T


def _nsa(qc, qr, gate, kk, vv, mt, ks, vs, kw, vw):
    bsz, seq, _ = qc.shape
    tq = min(TQ, seq)
    tk = min(TK, seq)
    n_cmp_pad = kk.shape[1]
    rows = NSA_HEADS * tq
    tile = lambda c: pl.BlockSpec((None, tq, c), lambda b, i: (b, i, 0))
    per_b = lambda r, c: pl.BlockSpec((None, r, c), lambda b, i: (b, 0, 0))
    per_b4 = lambda a: pl.BlockSpec((None,) + a.shape[1:], lambda b, i: (b, 0, 0, 0))
    return pl.pallas_call(
        functools.partial(_nsa_kernel, tq=tq, tk=tk, seq=seq),
        grid=(bsz, seq // tq),
        in_specs=[tile(256), tile(256), tile(LANES),
                  per_b(n_cmp_pad, LANES), per_b(LANES, n_cmp_pad),
                  pl.BlockSpec(mt.shape, lambda b, i: (0, 0)),
                  per_b(seq, 256), per_b4(vs), per_b(seq, LANES), per_b4(vw)],
        out_specs=tile(256),
        out_shape=jax.ShapeDtypeStruct((bsz, seq, 256), jnp.float32),
        scratch_shapes=[pltpu.VMEM((1, rows), jnp.float32), pltpu.VMEM((LANES, rows), jnp.float32),
                        pltpu.VMEM((tk, rows), jnp.float32), pltpu.VMEM((tk, rows), jnp.float32)],
        compiler_params=_cparams(2),
        name="nsa",
    )(qc, qr, gate, kk, vv, mt, ks, vs, kw, vw)


def _mix_out_kernel(x_ref, za_ref, zah_ref, zc_ref, zd_ref, zdh_ref, ynsa_ref,
                    cw_ref, cb_ref, clg_ref, clb_ref, slg_ref, slb_ref, sw_ref, sb_ref,
                    pw_ref, ps_ref, wout_ref, gpost_ref, out_ref,
                    uext_ref, ushift_ref, yconv_ref, pext_ref, *, tm):
    i = pl.program_id(1)
    not_first = jnp.where(i > 0, 1.0, 0.0)

    def glu(z):
        return z[:, :256] * jax.nn.sigmoid(z[:, 256:])

    uext_ref[0:CONV_HALO, :] = glu(zah_ref[...]) * not_first
    uext_ref[CONV_HALO:, :] = glu(za_ref[...])
    first = CONV_HALO - (CONV_K - 1)
    n_shift = tm + CONV_HALO - SUBLANES
    for r in range(SUBLANES):
        n = n_shift if first + r + n_shift <= tm + CONV_HALO else n_shift - SUBLANES
        ushift_ref[r, 0:n, :] = uext_ref[pl.ds(first + r, n), :]
    rc = 64
    for r0 in range(0, tm, rc):
        acc = jnp.zeros((rc, 256), jnp.float32)
        for k in range(CONV_K):
            acc = acc + cw_ref[k:k + 1, :] * ushift_ref[k % SUBLANES, pl.ds(r0 + k - k % SUBLANES, rc), :]
        yc = _layer_norm(acc + cb_ref[...], clg_ref[...], clb_ref[...])
        yconv_ref[r0:r0 + rc, :] = _bf(yc * jax.nn.sigmoid(yc))

    zc = jax.nn.gelu(zc_ref[...], approximate=True)
    u_s = zc[:, :256]
    v_s = _layer_norm(zc[:, 256:], slg_ref[...], slb_ref[...])
    t_i = lax.broadcasted_iota(jnp.int32, (SGU_CHUNK, SGU_GROUPS * SGU_CHUNK), 0)
    s_i = lax.rem(lax.broadcasted_iota(jnp.int32, (SGU_CHUNK, SGU_GROUPS * SGU_CHUNK), 1), SGU_CHUNK)
    w_cat = _bf(jnp.where(s_i <= t_i, sw_ref[...], 0.0))
    grp = lax.broadcasted_iota(jnp.int32, (SGU_CHUNK, 256), 1) >> 6
    ysgu = []
    for c in range(tm // SGU_CHUNK):
        vc = v_s[c * SGU_CHUNK:(c + 1) * SGU_CHUNK]
        stacked = _bf(jnp.concatenate([jnp.where(grp == g, vc, 0.0) for g in range(SGU_GROUPS)], axis=0))
        f = jnp.dot(w_cat, stacked, preferred_element_type=jnp.float32) + sb_ref[...]
        ysgu.append(u_s[c * SGU_CHUNK:(c + 1) * SGU_CHUNK] * f)
    y_sgu = _bf(jnp.concatenate(ysgu, axis=0))

    pext_ref[0:POOL_HALO, :] = zdh_ref[...] * not_first
    pext_ref[POOL_HALO:, :] = zd_ref[...]
    n_ext = tm + POOL_HALO
    sums = []
    for shift in (1, 2, 4, 8):
        n = n_ext - shift
        nxt = pext_ref[pl.ds(shift, n), :] + pext_ref[pl.ds(0, n), :]
        pext_ref[pl.ds(shift, n), :] = nxt
        sums.append(pext_ref[pl.ds(POOL_HALO, tm), :])
    lane_grp = lax.broadcasted_iota(jnp.int32, (tm, 256), 1) >> 6
    pos = i * tm + lax.broadcasted_iota(jnp.int32, (tm, 256), 0)
    wsum = jnp.where(lane_grp == 0, sums[0], jnp.where(lane_grp == 1, sums[1],
                     jnp.where(lane_grp == 2, sums[2], sums[3])))
    width = jnp.where(lane_grp == 0, POOL_WINDOWS[0], jnp.where(lane_grp == 1, POOL_WINDOWS[1],
                      jnp.where(lane_grp == 2, POOL_WINDOWS[2], POOL_WINDOWS[3])))
    cnt = jnp.minimum(pos + 1, width).astype(jnp.float32)
    pooled = wsum / cnt - zd_ref[...]
    y_pool = _bf(jnp.dot(_bf(pooled), pw_ref[...], preferred_element_type=jnp.float32) * ps_ref[...])

    y = (jnp.dot(yconv_ref[...], wout_ref[0:256, :], preferred_element_type=jnp.float32)
         + jnp.dot(_bf(ynsa_ref[...]), wout_ref[256:512, :], preferred_element_type=jnp.float32)
         + jnp.dot(y_sgu, wout_ref[512:768, :], preferred_element_type=jnp.float32)
         + jnp.dot(y_pool, wout_ref[768:1024, :], preferred_element_type=jnp.float32))
    out_ref[...] = x_ref[...] + _rms(y, gpost_ref[...])


def _mix_out(l, x, za, zc, zd, ynsa, cw, cb, clg, clb, slg, slb, sw_cat, sb_exp, pw_bd, ps, wout, gpost):
    bsz, seq, d = x.shape
    tm = min(TM_PROJ, seq)
    tok = lambda c: pl.BlockSpec((None, tm, c), lambda b, i: (b, i, 0))
    halo = lambda c, h: pl.BlockSpec((None, h, c), lambda b, i: (b, jnp.maximum(i * (tm // h) - 1, 0), 0))
    weights = [cw, cb, clg, clb, slg, slb, sw_cat, sb_exp, pw_bd, ps, wout, gpost]
    return pl.pallas_call(
        functools.partial(_mix_out_kernel, tm=tm),
        grid=(bsz, seq // tm),
        in_specs=[tok(d), tok(512), halo(512, CONV_HALO), tok(512), tok(256), halo(256, POOL_HALO), tok(256)]
                 + [_layer_spec(w, l) for w in weights],
        out_specs=tok(d),
        out_shape=jax.ShapeDtypeStruct((bsz, seq, d), jnp.float32),
        scratch_shapes=[pltpu.VMEM((tm + CONV_HALO, 256), jnp.float32),
                        pltpu.VMEM((SUBLANES, tm + CONV_HALO - SUBLANES, 256), jnp.float32),
                        pltpu.VMEM((tm, 256), MXU_DTYPE),
                        pltpu.VMEM((tm + POOL_HALO, 256), jnp.float32)],
        compiler_params=_cparams(2),
        name="mix_out",
    )(x, za, za, zc, zd, zd, ynsa, *weights)


def _ffn_kernel(x_ref, xh_ref, gpre_ref, wup_ref, cw_ref, cb_ref, wdn_ref, gpost_ref, out_ref,
                hext_ref, g0_ref, u0_ref, g1_ref, u1_ref, acc_ref, *, tm, n_chunks, chunk):
    i = pl.program_id(1)
    not_first = jnp.where(i > 0, 1.0, 0.0)
    x = x_ref[...]
    hext_ref[0:FFN_HALO, :] = _bf(_rms(xh_ref[...], gpre_ref[...]) * not_first)
    hext_ref[FFN_HALO:, :] = _bf(_rms(x, gpre_ref[...]))

    def cols(c):
        return pl.ds(pl.multiple_of(c * chunk, chunk), chunk)

    def up_to(g_ref, u_ref, c):
        h_ext = hext_ref[...]
        g_ref[...] = jnp.dot(h_ext, wup_ref[:, cols(c)], preferred_element_type=jnp.float32)
        u_ref[...] = jnp.dot(h_ext, wup_ref[:, cols(n_chunks + c)], preferred_element_type=jnp.float32)

    def conv(ext_ref, c):
        w = cw_ref[:, cols(c)]
        return (w[0:1] * ext_ref[pl.ds(FFN_HALO - 2, tm), :] + w[1:2] * ext_ref[pl.ds(FFN_HALO - 1, tm), :]
                + w[2:3] * ext_ref[pl.ds(FFN_HALO, tm), :] + cb_ref[:, cols(c)])

    def down_from(g_ref, u_ref, c):
        act = jax.nn.gelu(conv(g_ref, c), approximate=True) * conv(u_ref, n_chunks + c)
        acc_ref[...] += jnp.dot(_bf(act), wdn_ref[c], preferred_element_type=jnp.float32)

    assert n_chunks % 2 == 1
    acc_ref[...] = jnp.zeros_like(acc_ref)
    up_to(g0_ref, u0_ref, 0)

    def pair(j, carry):
        c = 2 * j
        up_to(g1_ref, u1_ref, c + 1)
        down_from(g0_ref, u0_ref, c)
        up_to(g0_ref, u0_ref, c + 2)
        down_from(g1_ref, u1_ref, c + 1)
        return carry

    lax.fori_loop(0, (n_chunks - 1) // 2, pair, 0)
    down_from(g0_ref, u0_ref, n_chunks - 1)
    out_ref[...] = x + _rms(acc_ref[...], gpost_ref[...])


def _ffn(l, x, gpre, wup, cw, cb, wdn, gpost):
    bsz, seq, d = x.shape
    tm = min(TM_PROJ, seq)
    n_chunks = wdn.shape[1]
    chunk = wdn.shape[2]
    tok = pl.BlockSpec((None, tm, d), lambda b, i: (b, i, 0))
    halo = pl.BlockSpec((None, FFN_HALO, d), lambda b, i: (b, jnp.maximum(i * (tm // FFN_HALO) - 1, 0), 0))
    weights = [gpre, wup, cw, cb, wdn, gpost]
    return pl.pallas_call(
        functools.partial(_ffn_kernel, tm=tm, n_chunks=n_chunks, chunk=chunk),
        grid=(bsz, seq // tm),
        in_specs=[tok, halo] + [_layer_spec(w, l) for w in weights],
        out_specs=tok,
        out_shape=jax.ShapeDtypeStruct((bsz, seq, d), jnp.float32),
        scratch_shapes=[pltpu.VMEM((tm + FFN_HALO, d), MXU_DTYPE)]
                       + [pltpu.VMEM((tm + FFN_HALO, chunk), jnp.float32)] * 4
                       + [pltpu.VMEM((tm, d), jnp.float32)],
        compiler_params=_cparams(2),
        name="ffn",
    )(x, x, *weights)


def _swap_halves(w):
    k = w.shape[-1] // HEAD_DIM
    w3 = w.reshape(w.shape[:-1] + (k, HEAD_DIM))
    half = HEAD_DIM // 2
    return jnp.concatenate([-w3[..., half:], w3[..., :half]], axis=-1).reshape(w.shape)


def _prep_w_in(w):
    d_mix = 1024
    c_conv, c_nsa, c_sgu, c_pool = d_mix // 4, NSA_HEADS * HEAD_DIM, d_mix // 4, d_mix // 4
    sizes = (2 * c_conv, c_nsa, 6 * HEAD_DIM, 3 * NSA_HEADS, 2 * c_sgu, c_pool)
    offs = np.cumsum((0,) + sizes)
    wa, wq, wkv, wg, wc, wd = [w[..., offs[j]:offs[j + 1]] for j in range(6)]
    kv = [wkv[..., j * HEAD_DIM:(j + 1) * HEAD_DIM] for j in range(6)]
    wg_pad = jnp.pad(wg, ((0, 0), (0, 0), (0, LANES - wg.shape[-1])))
    cols = [wa, wc, wd, wq, _swap_halves(wq), kv[0], kv[1],
            kv[2], _swap_halves(kv[2]), kv[4], _swap_halves(kv[4]), kv[3], kv[5], wg_pad]
    return _bf(jnp.concatenate(cols, axis=-1))


def _prep_compress(pe_k, pe_v, ck_w1, ck_w2, cv_w1, cv_w2):
    half = CMP_LEN // 2

    def w1_part(w1k, w1v, lo):
        wk = w1k.reshape(CMP_LEN, HEAD_DIM, HEAD_DIM)[lo:lo + half]
        wv = w1v.reshape(CMP_LEN, HEAD_DIM, HEAD_DIM)[lo:lo + half]
        z = jnp.zeros_like(wk)
        top = jnp.concatenate([wk, z], axis=-1)
        bot = jnp.concatenate([z, wv], axis=-1)
        return jnp.concatenate([top, bot], axis=1).reshape(half * 2 * HEAD_DIM, 2 * HEAD_DIM)

    def pe_part(lo):
        return jnp.concatenate([pe_k[lo:lo + half], pe_v[lo:lo + half]], axis=1).reshape(1, half * 2 * HEAD_DIM)

    z = jnp.zeros_like(ck_w2)
    w2 = jnp.concatenate([jnp.concatenate([ck_w2, z], axis=1), jnp.concatenate([z, cv_w2], axis=1)], axis=0)
    return (pe_part(0), pe_part(half), _bf(w1_part(ck_w1, cv_w1, 0)), _bf(w1_part(ck_w1, cv_w1, half)), _bf(w2))


def _importance_matrix(seq):
    n_cmp = (seq - CMP_LEN) // CMP_STRIDE + 1
    n_sel = seq // SEL_LEN
    ratio = SEL_LEN // CMP_STRIDE
    n_ov = CMP_LEN // CMP_STRIDE
    n_pad = -(-n_cmp // LANES) * LANES
    assert n_sel <= LANES, "selection blocks must fit one lane group"
    assert min(SEL_TOPK, n_sel) > 3, "top-k must exceed the three forced blocks"
    mt = np.zeros((LANES, n_pad), np.float32)
    for j in range(n_sel):
        for m in range(ratio):
            for n in range(n_ov):
                c = ratio * j + m - n
                if 0 <= c < n_cmp:
                    mt[j, c] += 1.0
    return jnp.asarray(mt, MXU_DTYPE)


def _rope_table(seq):
    half = HEAD_DIM // 2
    inv = ROPE_THETA ** (-jnp.arange(half, dtype=jnp.float32) * 2.0 / HEAD_DIM)
    ang = jnp.arange(seq, dtype=jnp.float32)[:, None] * inv[None, :]
    cos, sin = jnp.cos(ang), jnp.sin(ang)
    return jnp.concatenate([cos, cos, sin, sin], axis=1)


def _block_diag(w):
    g, c, d = w.shape
    out = jnp.zeros((g * c, g * d), w.dtype)
    for j in range(g):
        out = out.at[j * c:(j + 1) * c, j * d:(j + 1) * d].set(w[j])
    return out


def kernel(x, norm_mix_pre, norm_mix_post, norm_ffn_pre, norm_ffn_post, w_in, w_out, conv_dw_w, conv_dw_b, conv_ln_g, conv_ln_b, nsa_pe_k, nsa_pe_v, nsa_ck_w1, nsa_ck_w2, nsa_cv_w1, nsa_cv_w2, sgu_ln_g, sgu_ln_b, sgu_w, sgu_b, pool_w, pool_scale, ffn_up, ffn_conv_w, ffn_conv_b, ffn_down):
    bsz, seq, d = x.shape
    depth = w_in.shape[0]
    ffn_dim = ffn_down.shape[1]
    cs = _rope_table(seq)
    mt = _importance_matrix(seq)
    rows = lambda v: v[:, None, :]

    w_ext = _prep_w_in(w_in)
    cmp_w = jax.vmap(_prep_compress)(nsa_pe_k, nsa_pe_v, nsa_ck_w1, nsa_ck_w2, nsa_cv_w1, nsa_cv_w2)
    sw_cat = jnp.transpose(sgu_w, (0, 2, 1, 3)).reshape(depth, SGU_CHUNK, SGU_GROUPS * SGU_CHUNK)
    sb_exp = jnp.repeat(jnp.swapaxes(sgu_b, 1, 2), 256 // SGU_GROUPS, axis=2)
    mix_w = [conv_dw_w, rows(conv_dw_b), rows(conv_ln_g), rows(conv_ln_b), rows(sgu_ln_g), rows(sgu_ln_b),
             sw_cat, sb_exp, _bf(jax.vmap(_block_diag)(pool_w)), rows(pool_scale), _bf(w_out), rows(norm_mix_post)]
    ffn_w = [rows(norm_ffn_pre), _bf(ffn_up), ffn_conv_w, rows(ffn_conv_b),
             _bf(ffn_down).reshape(depth, ffn_dim // FFN_CHUNK, FFN_CHUNK, d), rows(norm_ffn_post)]
    g_pre = rows(norm_mix_pre)

    for l in range(depth):
        za, zc, zd, qc, qr, kvc, ks, vs, kw, vw, gate = _in_proj(l, x, g_pre, w_ext, cs)
        a_cmp = kvc.reshape(bsz, seq // (CMP_LEN // 2), (CMP_LEN // 2) * 2 * HEAD_DIM)
        kk, vv = _compress(l, a_cmp, *cmp_w)
        y_nsa = _nsa(qc, qr, gate, kk, vv, mt, ks, vs, kw, vw)
        x = _mix_out(l, x, za, zc, zd, y_nsa, *mix_w)
        x = _ffn(l, x, *ffn_w)
    return x
```

```python
import functools

import numpy as np
import jax
import jax.numpy as jnp
from jax import lax
from jax.experimental import pallas as pl
from jax.experimental.pallas import tpu as pltpu

HEAD_DIM = 64
NSA_HEADS = 4
CONV_K = 31
CMP_LEN = 32
CMP_STRIDE = 16
SEL_LEN = 64
SEL_TOPK = 16
WINDOW = 512
SGU_CHUNK = 128
SGU_GROUPS = 4
POOL_WINDOWS = (2, 4, 8, 16)
FFN_CONV_K = 3
ROPE_THETA = 10000.0
RMS_EPS = 1e-6
LN_EPS = 1e-5
NEG_INF = -1e30
FORCE_SCORE = 1e6
REMOVED = -3e38
LOG2E = 1.4426950408889634

LANES = 128
SUBLANES = 8
V_ROWS = 80
MXU_DTYPE = jnp.bfloat16
VMEM_LIMIT = 56 * 1024 * 1024

TM_PROJ = 512
TM_FFN = 1024
TQ = 256
TK = 512
FFN_CHUNK = 256
CONV_HALO = 32
POOL_HALO = 16
FFN_HALO = 16

_NT = (((1,), (1,)), ((), ()))


def _cparams(n_axes, flags=None):
    return pltpu.CompilerParams(dimension_semantics=("arbitrary",) * n_axes,
                                vmem_limit_bytes=VMEM_LIMIT, flags=flags)


def _layer_spec(a, l):
    return pl.BlockSpec((None,) + a.shape[1:], lambda *_: (l,) + (0,) * (a.ndim - 1),
                        pipeline_mode=pl.Buffered(1))


def _rms(x, g):
    return x * lax.rsqrt(jnp.mean(x * x, axis=-1, keepdims=True) + RMS_EPS) * g


def _layer_norm(x, g, b):
    mu = jnp.mean(x, axis=-1, keepdims=True)
    var = jnp.mean(jnp.square(x - mu), axis=-1, keepdims=True)
    return (x - mu) * lax.rsqrt(var + LN_EPS) * g + b


def _bf(x):
    return x.astype(MXU_DTYPE)


def _swap64(x):
    return pltpu.roll(x, 64, axis=1)


def _in_proj_kernel(x_ref, g_ref, w_ref, cs_ref,
                    za_ref, zc_ref, zd_ref, qc_ref, qr_ref, kv_ref,
                    ks_ref, vs_ref, kw_ref, vw_ref, gate_ref, kvt_ref, *, tm, tk):
    i = pl.program_id(1)
    h = _bf(_rms(x_ref[...], g_ref[...]))

    def proj(lo, hi):
        return jnp.dot(h, w_ref[:, lo:hi], preferred_element_type=jnp.float32)

    za_ref[...] = proj(0, 512)
    zc_ref[...] = proj(512, 1024)
    zd_ref[...] = proj(1024, 1280)
    q = proj(1280, 1536)
    q_sw = proj(1536, 1792)
    kv_ksl = proj(1792, 2048)
    kwn_vsw = proj(2048, 2304)
    kvt_ref[...] = kv_ksl[:, :LANES]
    group = CMP_LEN // 2
    for j in range(group):
        kv_ref[:, j * LANES:(j + 1) * LANES] = kvt_ref[pl.ds(j, tm // group, stride=group), :]
    ksl = kv_ksl[:, LANES:]
    kwn = kwn_vsw[:, :LANES]
    vsw = kwn_vsw[:, LANES:]
    gate_ref[...] = jax.nn.sigmoid(proj(2304, 2432))

    cs = cs_ref[...]
    lane = lax.broadcasted_iota(jnp.int32, (tm, LANES), 1)
    sc = _swap64(cs)
    cos2 = jnp.where(lane < 64, cs, sc)
    sin2 = jnp.where(lane < 64, sc, cs)
    scale = HEAD_DIM ** -0.5 * LOG2E
    qc_ref[...] = _bf(q * scale)
    for pair in range(NSA_HEADS // 2):
        c = slice(pair * LANES, (pair + 1) * LANES)
        qr_ref[:, c] = _bf((q[:, c] * cos2 + q_sw[:, c] * sin2) * scale)

    ts = ksl * cs
    k_rot2 = ts + _swap64(ts)
    tw = kwn * cs
    kw_rot2 = tw + _swap64(tw)
    pos = i * tm + lax.broadcasted_iota(jnp.int32, (tm, LANES), 0)
    onehot =jnp.where((pos >> 6) == lane, 1.0, 0.0)
    ks_ref[:, 0:LANES] = _bf(onehot)
    ks_ref[:, LANES:2 * LANES] = _bf(k_rot2)
    kw_ref[...] = _bf(kw_rot2)
    ones_col = jnp.where(lane == 64, 1.0, 0.0)
    vs = jnp.where(lane < 64, vsw, ones_col)
    vw = jnp.where(lane < 64, _swap64(vsw), ones_col)
    for j in range(tm // tk):
        vs_ref[j] = _bf(vs[j * tk:(j + 1) * tk].T[0:V_ROWS])
    for j in range(tm // TQ):
        vw_ref[j] = _bf(vw[j * TQ:(j + 1) * TQ].T[0:V_ROWS])


def _in_proj(l, x, g, w_ext, cs):
    bsz, seq, d = x.shape
    tm = min(TM_PROJ, seq)
    tk = min(TK, seq)
    tok = lambda c: pl.BlockSpec((None, tm, c), lambda b, i: (b, i, 0))
    f32, bf16 = jnp.float32, MXU_DTYPE

    def tok_out(c, dt):
        return tok(c), jax.ShapeDtypeStruct((bsz, seq, c), dt)

    def keytile_out(t):
        return (pl.BlockSpec((None, tm // t, V_ROWS, t), lambda b, i: (b, i, 0, 0)),
                jax.ShapeDtypeStruct((bsz, seq // t, V_ROWS, t), bf16))

    group = CMP_LEN // 2
    cmp_out = (pl.BlockSpec((None, tm // group, group * LANES), lambda b, i: (b, i, 0)),
               jax.ShapeDtypeStruct((bsz, seq // group, group * LANES), f32))
    outs = [tok_out(512, f32), tok_out(512, f32), tok_out(256, f32), tok_out(256, bf16), tok_out(256, bf16),
            cmp_out, tok_out(256, bf16), keytile_out(tk), tok_out(128, bf16), keytile_out(TQ),
            tok_out(128, f32)]
    return pl.pallas_call(
        functools.partial(_in_proj_kernel, tm=tm, tk=tk),
        grid=(bsz, seq // tm),
        in_specs=[tok(d), _layer_spec(g, l), _layer_spec(w_ext, l),
                  pl.BlockSpec((tm, LANES), lambda b, i: (i, 0))],
        out_specs=[spec for spec, _ in outs],
        out_shape=[shape for _, shape in outs],
        scratch_shapes=[pltpu.VMEM((tm, LANES), f32)],
        compiler_params=_cparams(2),
        name="in_proj",
    )(x, g, w_ext, cs)


def _compress_kernel(a_ref, pe_lo_ref, pe_hi_ref, w_lo_ref, w_hi_ref, w2_ref, kk_ref, vv_ref):
    a = a_ref[...]
    u = jnp.dot(_bf(a + pe_lo_ref[...]), w_lo_ref[...], preferred_element_type=jnp.float32)
    v = jnp.dot(_bf(a + pe_hi_ref[...]), w_hi_ref[...], preferred_element_type=jnp.float32)
    n = a.shape[0]
    pre = u + pltpu.roll(v, n - 1, axis=0)
    act = jax.nn.gelu(pre, approximate=True)
    kv = jnp.dot(_bf(act), w2_ref[...], preferred_element_type=jnp.float32)
    vk = _swap64(kv)
    lane = lax.broadcasted_iota(jnp.int32, (n, LANES), 1)
    kk_ref[...] = _bf(jnp.where(lane < 64, kv, vk))
    vv = jnp.where(lane < 64, vk, jnp.where(lane == 64, 1.0, 0.0))
    vv_ref[...] = _bf(vv.T[0:V_ROWS])


def _compress(l, a, pe_lo, pe_hi, w_lo, w_hi, w2):
    bsz, n, width = a.shape
    per_b = lambda c: pl.BlockSpec((None, n, c), lambda b: (b, 0, 0))
    return pl.pallas_call(
        _compress_kernel,
        grid=(bsz,),
        in_specs=[per_b(width)] + [_layer_spec(w, l) for w in (pe_lo, pe_hi, w_lo, w_hi, w2)],
        out_specs=[per_b(LANES), pl.BlockSpec((None, V_ROWS, n), lambda b: (b, 0, 0))],
        out_shape=[jax.ShapeDtypeStruct((bsz, n, LANES), MXU_DTYPE),
                   jax.ShapeDtypeStruct((bsz, V_ROWS, n), MXU_DTYPE)],
        compiler_params=_cparams(1),
        name="compress",
    )(a, pe_lo, pe_hi, w_lo, w_hi, w2)


def _nsa_kernel(qc_ref, qr_ref, gate_ref, kk_ref, vv_ref, mt_ref, ks_ref, vs_ref, kw_ref, vw_ref,
                out_ref, m_ref, acc_ref, s0_ref, s1_ref, ocmp_ref, imp_ref, *, tq, tk, seq):
    qi = pl.program_id(1)
    t0 = qi * tq
    rows = NSA_HEADS * tq
    n_cmp_pad = kk_ref.shape[0]
    top_n = min(SEL_TOPK, seq // SEL_LEN)
    n_sel = LANES
    win_keys = min(WINDOW + tq, seq)

    lane =lax.broadcasted_iota(jnp.int32, (tq, LANES), 1)

    def head_rows(q_ref):
        parts = []
        for hd in range(NSA_HEADS):
            pair = q_ref[:, (hd // 2) * LANES:(hd // 2 + 1) * LANES]
            keep = (lane < 64) if hd % 2 == 0 else (lane >= 64)
            parts.append(jnp.where(keep, pair, jnp.zeros_like(pair)))
        return jnp.concatenate(parts, axis=0)

    q_c = head_rows(qc_ref)
    q_r = head_rows(qr_ref)
    pos_l = t0 + (lax.broadcasted_iota(jnp.int32, (1, rows), 1) & (tq - 1))

    last_blk = (pos_l - (CMP_LEN - 1)) >> 4

    def compressed(nr):
        s = lax.dot_general(kk_ref[0:nr, :], q_c, _NT, preferred_element_type=jnp.float32)
        n_idx = lax.broadcasted_iota(jnp.int32, (nr, rows), 0)
        sm = jnp.where(n_idx <= last_blk, s, NEG_INF)
        mx = jnp.max(sm, axis=0, keepdims=True)
        e = jnp.exp2(sm - mx)
        den = jnp.sum(e, axis=0, keepdims=True)
        p_cmp = e * jnp.where(last_blk >= 0, 1.0 / den, 0.0)
        ocmp_ref[...] = jnp.dot(vv_ref[:, 0:nr], _bf(p_cmp), preferred_element_type=jnp.float32)
        p_sum = p_cmp[:, 0:tq] + p_cmp[:, tq:2 * tq] + p_cmp[:, 2 * tq:3 * tq] + p_cmp[:, 3 * tq:4 * tq]
        hi = _bf(p_sum)
        r1 = p_sum - hi.astype(jnp.float32)
        mid = _bf(r1)
        lo = _bf(r1 - mid.astype(jnp.float32))
        mt = mt_ref[:, 0:nr]
        imp_ref[...] = (jnp.dot(mt, hi, preferred_element_type=jnp.float32)
                        + jnp.dot(mt, mid, preferred_element_type=jnp.float32)
                        + jnp.dot(mt, lo, preferred_element_type=jnp.float32))

    n_var = n_cmp_pad // LANES
    tile_last = jnp.maximum((t0 + tq - CMP_LEN) >> 4, 0)
    need = jnp.minimum(tile_last // LANES, n_var - 1)
    for v in range(n_var):
        pl.when(need == v)(functools.partial(compressed, (v + 1) * LANES))
    o_cmp = ocmp_ref[...]
    imp = imp_ref[...]

    start = pl.multiple_of(jnp.maximum(t0 + tq - win_keys, 0), tq)
    sw = lax.dot_general(kw_ref[pl.ds(start, win_keys), :], q_r, _NT, preferred_element_type=jnp.float32)
    back = pos_l - (start + lax.broadcasted_iota(jnp.int32, (win_keys, rows), 0))
    in_win = lax.bitcast_convert_type(back, jnp.uint32) < WINDOW
    sw = jnp.where(in_win, sw, NEG_INF)
    mw = jnp.max(sw, axis=0, keepdims=True)
    pw = _bf(jnp.exp2(sw - mw))
    wt0 = start // tq
    ow = jnp.dot(vw_ref[wt0], pw[0:tq], preferred_element_type=jnp.float32)
    for j in range(1, win_keys // tq):
        ow = ow + jnp.dot(vw_ref[wt0 + j], pw[j * tq:(j + 1) * tq], preferred_element_type=jnp.float32)
    o_win = ow / ow[64:65, :]

    blk = lax.broadcasted_iota(jnp.int32, (n_sel, tq), 0)
    blk_f = blk.astype(jnp.float32)
    pos_t = t0 + lax.broadcasted_iota(jnp.int32, (n_sel, tq), 1)
    cur = pos_t >> 6
    forced = (blk == 0) | (blk == cur) | (blk == cur - 1)
    valid = (blk * SEL_LEN) <= pos_t
    val = jnp.where(valid & jnp.logical_not(forced), imp, NEG_INF)
    picked = jnp.where(forced, 1.0, 0.0)
    for _ in range(top_n - 3):
        best = jnp.max(val, axis=0, keepdims=True)
        first = jnp.min(jnp.where(val == best, blk_f, float(n_sel)), axis=0, keepdims=True)
        hit = blk_f == first
        picked = jnp.where(hit, 1.0, picked)
        val = jnp.where(hit, REMOVED, val)
    bias_t = jnp.where((picked > 0.5) & valid, 0.0, NEG_INF)
    bias = _bf(bias_t.T)

    l_sel = jnp.concatenate(
        [jnp.concatenate([bias, q_r[hd * tq:(hd + 1) * tq]], axis=1) for hd in range(NSA_HEADS)], axis=0)
    m_ref[...] = jnp.full((1, rows), NEG_INF, jnp.float32)
    acc_ref[...] = jnp.zeros((V_ROWS, rows), jnp.float32)

    def scores_to(s_ref, kt, causal):
        k0 = pl.multiple_of(kt * tk, tk)
        sc = lax.dot_general(ks_ref[pl.ds(k0, tk), :], l_sel, _NT, preferred_element_type=jnp.float32)
        if causal:
            kpos = k0 + lax.broadcasted_iota(jnp.int32, (tk, rows), 0)
            sc = jnp.where(kpos <= pos_l, sc, NEG_INF)
        s_ref[...] = sc

    def softmax_from(s_ref, kt):
        sc = s_ref[...]
        m_old = m_ref[...]
        m_new = jnp.maximum(m_old, jnp.max(sc, axis=0, keepdims=True))
        alpha = jnp.exp2(m_old - m_new)
        p = jnp.exp2(sc - m_new)
        acc_ref[...] = alpha * acc_ref[...] + jnp.dot(vs_ref[kt], _bf(p), preferred_element_type=jnp.float32)
        m_ref[...] = m_new

    n_full = t0 // tk
    n_pairs = n_full // 2
    scores_to(s0_ref, n_full, True)

    def pair(j, carry):
        k = 2 * j
        scores_to(s1_ref, k, False)
        softmax_from(s0_ref, jnp.where(j == 0, n_full, k - 1))
        scores_to(s0_ref, k + 1, False)
        softmax_from(s1_ref, k)
        return carry

    lax.fori_loop(0, n_pairs, pair, 0)
    pending = jnp.where(n_pairs == 0, n_full, 2 * n_pairs - 1)

    @pl.when(n_full % 2 == 1)
    def _():
        scores_to(s1_ref, n_full - 1, False)
        softmax_from(s0_ref, pending)
        softmax_from(s1_ref, n_full - 1)

    @pl.when(n_full % 2 == 0)
    def _():
        softmax_from(s0_ref, pending)

    acc = acc_ref[...]
    o_slc = acc / acc[64:65, :]

    gate_t = gate_ref[...].T
    y_t = []
    for hd in range(NSA_HEADS):
        c = slice(hd * tq, (hd + 1) * tq)
        y_t.append(gate_t[3 * hd:3 * hd + 1] * o_cmp[0:64, c]
                   + gate_t[3 * hd + 1:3 * hd + 2] * o_slc[0:64, c]
                   + gate_t[3 * hd + 2:3 * hd + 3] * o_win[0:64, c])
    out_ref[:, 0:LANES] = jnp.concatenate(y_t[0:2], axis=0).T
    out_ref[:, LANES:2 * LANES] = jnp.concatenate(y_t[2:4], axis=0).T


def _nsa(qc, qr, gate, kk, vv, mt, ks, vs, kw, vw):
    bsz, seq, _ = qc.shape
    tq = min(TQ, seq)
    tk = min(TK, seq)
    n_cmp_pad = kk.shape[1]
    rows = NSA_HEADS * tq
    tile = lambda c: pl.BlockSpec((None, tq, c), lambda b, i: (b, i, 0))
    per_b = lambda r, c: pl.BlockSpec((None, r, c), lambda b, i: (b, 0, 0))
    per_b4 = lambda a: pl.BlockSpec((None,) + a.shape[1:], lambda b, i: (b, 0, 0, 0))
    return pl.pallas_call(
        functools.partial(_nsa_kernel, tq=tq, tk=tk, seq=seq),
        grid=(bsz, seq // tq),
        in_specs=[tile(256), tile(256), tile(LANES),
                  per_b(n_cmp_pad, LANES), per_b(V_ROWS, n_cmp_pad),
                  pl.BlockSpec(mt.shape, lambda b, i: (0, 0)),
                  per_b(seq, 256), per_b4(vs), per_b(seq, LANES), per_b4(vw)],
        out_specs=tile(256),
        out_shape=jax.ShapeDtypeStruct((bsz, seq, 256), jnp.float32),
        scratch_shapes=[pltpu.VMEM((1, rows), jnp.float32), pltpu.VMEM((V_ROWS, rows), jnp.float32),
                        pltpu.VMEM((tk, rows), jnp.float32), pltpu.VMEM((tk, rows), jnp.float32),
                        pltpu.VMEM((V_ROWS, rows), jnp.float32), pltpu.VMEM((LANES, tq), jnp.float32)],
        compiler_params=_cparams(2),
        name="nsa",
    )(qc, qr, gate, kk, vv, mt, ks, vs, kw, vw)


def _mix_out_kernel(x_ref, za_ref, zah_ref, zc_ref, zd_ref, zdh_ref, ynsa_ref,
                    cw_ref, cb_ref, clg_ref, clb_ref, slg_ref, slb_ref, sw_ref, sb_ref,
                    pw_ref, ps_ref, wout_ref, gpost_ref, out_ref,
                    uext_ref, ushift_ref, yconv_ref, pext_ref, *, tm):
    i = pl.program_id(1)
    not_first = jnp.where(i > 0, 1.0, 0.0)

    def glu(z):
        return z[:, :256] * jax.nn.sigmoid(z[:, 256:])

    uext_ref[0:CONV_HALO, :] = glu(zah_ref[...]) * not_first
    uext_ref[CONV_HALO:, :] = glu(za_ref[...])
    first = CONV_HALO - (CONV_K - 1)
    n_shift = tm + CONV_HALO - SUBLANES
    for r in range(SUBLANES):
        n = n_shift if first + r + n_shift <= tm + CONV_HALO else n_shift - SUBLANES
        ushift_ref[r, 0:n, :] = uext_ref[pl.ds(first + r, n), :]
    rc = 64
    for r0 in range(0, tm, rc):
        acc = jnp.zeros((rc, 256), jnp.float32)
        for k in range(CONV_K):
            acc = acc + cw_ref[k:k + 1, :] * ushift_ref[k % SUBLANES, pl.ds(r0 + k - k % SUBLANES, rc), :]
        yc = _layer_norm(acc + cb_ref[...], clg_ref[...], clb_ref[...])
        yconv_ref[r0:r0 + rc, :] = _bf(yc * jax.nn.sigmoid(yc))

    zc = jax.nn.gelu(zc_ref[...], approximate=True)
    u_s = zc[:, :256]
    v_s = _layer_norm(zc[:, 256:], slg_ref[...], slb_ref[...])
    t_i = lax.broadcasted_iota(jnp.int32, (SGU_CHUNK, SGU_GROUPS * SGU_CHUNK), 0)
    s_i = lax.rem(lax.broadcasted_iota(jnp.int32, (SGU_CHUNK, SGU_GROUPS * SGU_CHUNK), 1), SGU_CHUNK)
    w_cat = _bf(jnp.where(s_i <= t_i, sw_ref[...], 0.0))
    grp = lax.broadcasted_iota(jnp.int32, (SGU_CHUNK, 256), 1) >> 6
    ysgu = []
    for c in range(tm // SGU_CHUNK):
        vc = v_s[c * SGU_CHUNK:(c + 1) * SGU_CHUNK]
        stacked = _bf(jnp.concatenate([jnp.where(grp == g, vc, 0.0) for g in range(SGU_GROUPS)], axis=0))
        f = jnp.dot(w_cat, stacked, preferred_element_type=jnp.float32) + sb_ref[...]
        ysgu.append(u_s[c * SGU_CHUNK:(c + 1) * SGU_CHUNK] * f)
    y_sgu = _bf(jnp.concatenate(ysgu, axis=0))

    pext_ref[0:POOL_HALO, :] = zdh_ref[...] * not_first
    pext_ref[POOL_HALO:, :] = zd_ref[...]
    n_ext = tm + POOL_HALO
    sums = []
    for shift in (1, 2, 4, 8):
        n = n_ext - shift
        nxt = pext_ref[pl.ds(shift, n), :] + pext_ref[pl.ds(0, n), :]
        pext_ref[pl.ds(shift, n), :] = nxt
        sums.append(pext_ref[pl.ds(POOL_HALO, tm), :])
    lane_grp = lax.broadcasted_iota(jnp.int32, (tm, 256), 1) >> 6
    pos = i * tm + lax.broadcasted_iota(jnp.int32, (tm, 256), 0)
    wsum = jnp.where(lane_grp == 0, sums[0], jnp.where(lane_grp == 1, sums[1],
                     jnp.where(lane_grp == 2, sums[2], sums[3])))
    width = jnp.where(lane_grp == 0, POOL_WINDOWS[0], jnp.where(lane_grp == 1, POOL_WINDOWS[1],
                      jnp.where(lane_grp == 2, POOL_WINDOWS[2], POOL_WINDOWS[3])))
    cnt = jnp.minimum(pos + 1, width).astype(jnp.float32)
    pooled = wsum / cnt - zd_ref[...]
    y_pool = _bf(jnp.dot(_bf(pooled), pw_ref[...], preferred_element_type=jnp.float32) * ps_ref[...])

    y = (jnp.dot(yconv_ref[...], wout_ref[0:256, :], preferred_element_type=jnp.float32)
         + jnp.dot(_bf(ynsa_ref[...]), wout_ref[256:512, :], preferred_element_type=jnp.float32)
         + jnp.dot(y_sgu, wout_ref[512:768, :], preferred_element_type=jnp.float32)
         + jnp.dot(y_pool, wout_ref[768:1024, :], preferred_element_type=jnp.float32))
    out_ref[...] = x_ref[...] + _rms(y, gpost_ref[...])


def _mix_out(l, x, za, zc, zd, ynsa, cw, cb, clg, clb, slg, slb, sw_cat, sb_exp, pw_bd, ps, wout, gpost):
    bsz, seq, d = x.shape
    tm = min(TM_PROJ, seq)
    tok = lambda c: pl.BlockSpec((None, tm, c), lambda b, i: (b, i, 0))
    halo = lambda c, h: pl.BlockSpec((None, h, c), lambda b, i: (b, jnp.maximum(i * (tm // h) - 1, 0), 0))
    weights = [cw, cb, clg, clb, slg, slb, sw_cat, sb_exp, pw_bd, ps, wout, gpost]
    return pl.pallas_call(
        functools.partial(_mix_out_kernel, tm=tm),
        grid=(bsz, seq // tm),
        in_specs=[tok(d), tok(512), halo(512, CONV_HALO), tok(512), tok(256), halo(256, POOL_HALO), tok(256)]
                 + [_layer_spec(w, l) for w in weights],
        out_specs=tok(d),
        out_shape=jax.ShapeDtypeStruct((bsz, seq, d), jnp.float32),
        scratch_shapes=[pltpu.VMEM((tm + CONV_HALO, 256), jnp.float32),
                        pltpu.VMEM((SUBLANES, tm + CONV_HALO - SUBLANES, 256), jnp.float32),
                        pltpu.VMEM((tm, 256), MXU_DTYPE),
                        pltpu.VMEM((tm + POOL_HALO, 256), jnp.float32)],
        compiler_params=_cparams(2),
        name="mix_out",
    )(x, za, za, zc, zd, zd, ynsa, *weights)


def _ffn_kernel(x_ref, xh_ref, gpre_ref, wup_ref, cw_ref, cb_ref, wdn_ref, gpost_ref, out_ref,
                hext_ref, g0_ref, u0_ref, g1_ref, u1_ref, acc_ref, *, tm, n_chunks, chunk):
    i = pl.program_id(1)
    not_first = jnp.where(i > 0, 1.0, 0.0)
    x = x_ref[...]
    hext_ref[0:FFN_HALO, :] = _bf(_rms(xh_ref[...], gpre_ref[...]) * not_first)
    hext_ref[FFN_HALO:, :] = _bf(_rms(x, gpre_ref[...]))

    def cols(c):
        return pl.ds(pl.multiple_of(c * chunk, chunk), chunk)

    def up_to(g_ref, u_ref, c):
        h_ext = hext_ref[...]
        g_ref[...] = jnp.dot(h_ext, wup_ref[:, cols(c)], preferred_element_type=jnp.float32)
        u_ref[...] = jnp.dot(h_ext, wup_ref[:, cols(n_chunks + c)], preferred_element_type=jnp.float32)

    def conv(ext_ref, c):
        w = cw_ref[:, cols(c)]
        return (w[0:1] * ext_ref[pl.ds(FFN_HALO - 2, tm), :] + w[1:2] * ext_ref[pl.ds(FFN_HALO - 1, tm), :]
                + w[2:3] * ext_ref[pl.ds(FFN_HALO, tm), :] + cb_ref[:, cols(c)])

    def down_from(g_ref, u_ref, c):
        act = jax.nn.gelu(conv(g_ref, c), approximate=True) * conv(u_ref, n_chunks + c)
        acc_ref[...] += jnp.dot(_bf(act), wdn_ref[c], preferred_element_type=jnp.float32)

    assert n_chunks % 2 == 1
    acc_ref[...] = jnp.zeros_like(acc_ref)
    up_to(g0_ref, u0_ref, 0)

    def pair(j, carry):
        c = 2 * j
        up_to(g1_ref, u1_ref, c + 1)
        down_from(g0_ref, u0_ref, c)
        up_to(g0_ref, u0_ref, c + 2)
        down_from(g1_ref, u1_ref, c + 1)
        return carry

    lax.fori_loop(0, (n_chunks - 1) // 2, pair, 0)
    down_from(g0_ref, u0_ref, n_chunks - 1)
    out_ref[...] = x + _rms(acc_ref[...], gpost_ref[...])


def _ffn(l, x, gpre, wup, cw, cb, wdn, gpost):
    bsz, seq, d = x.shape
    tm = min(TM_FFN, seq)
    n_chunks = wdn.shape[1]
    chunk = wdn.shape[2]
    tok = pl.BlockSpec((None, tm, d), lambda b, i: (b, i, 0))
    halo = pl.BlockSpec((None, FFN_HALO, d), lambda b, i: (b, jnp.maximum(i * (tm // FFN_HALO) - 1, 0), 0))
    weights = [gpre, wup, cw, cb, wdn, gpost]
    return pl.pallas_call(
        functools.partial(_ffn_kernel, tm=tm, n_chunks=n_chunks, chunk=chunk),
        grid=(bsz, seq // tm),
        in_specs=[tok, halo] + [_layer_spec(w, l) for w in weights],
        out_specs=tok,
        out_shape=jax.ShapeDtypeStruct((bsz, seq, d), jnp.float32),
        scratch_shapes=[pltpu.VMEM((tm + FFN_HALO, d), MXU_DTYPE)]
                       + [pltpu.VMEM((tm + FFN_HALO, chunk), jnp.float32)] * 4
                       + [pltpu.VMEM((tm, d), jnp.float32)],
        compiler_params=_cparams(2),
        name="ffn",
    )(x, x, *weights)


def _swap_halves(w):
    k = w.shape[-1] // HEAD_DIM
    w3 = w.reshape(w.shape[:-1] + (k, HEAD_DIM))
    half = HEAD_DIM // 2
    return jnp.concatenate([-w3[..., half:], w3[..., :half]], axis=-1).reshape(w.shape)


def _prep_w_in(w):
    d_mix = 1024
    c_conv, c_nsa, c_sgu, c_pool = d_mix // 4, NSA_HEADS * HEAD_DIM, d_mix // 4, d_mix // 4
    sizes = (2 * c_conv, c_nsa, 6 * HEAD_DIM, 3 * NSA_HEADS, 2 * c_sgu, c_pool)
    offs = np.cumsum((0,) + sizes)
    wa, wq, wkv, wg, wc, wd = [w[..., offs[j]:offs[j + 1]] for j in range(6)]
    kv = [wkv[..., j * HEAD_DIM:(j + 1) * HEAD_DIM] for j in range(6)]
    wg_pad = jnp.pad(wg, ((0, 0), (0, 0), (0, LANES - wg.shape[-1])))
    cols = [wa, wc, wd, wq, _swap_halves(wq), kv[0], kv[1],
            kv[2], _swap_halves(kv[2]), kv[4], _swap_halves(kv[4]), kv[3], kv[5], wg_pad]
    return _bf(jnp.concatenate(cols, axis=-1))


def _prep_compress(pe_k, pe_v, ck_w1, ck_w2, cv_w1, cv_w2):
    half = CMP_LEN // 2

    def w1_part(w1k, w1v, lo):
        wk = w1k.reshape(CMP_LEN, HEAD_DIM, HEAD_DIM)[lo:lo + half]
        wv = w1v.reshape(CMP_LEN, HEAD_DIM, HEAD_DIM)[lo:lo + half]
        z = jnp.zeros_like(wk)
        top = jnp.concatenate([wk, z], axis=-1)
        bot = jnp.concatenate([z, wv], axis=-1)
        return jnp.concatenate([top, bot], axis=1).reshape(half * 2 * HEAD_DIM, 2 * HEAD_DIM)

    def pe_part(lo):
        return jnp.concatenate([pe_k[lo:lo + half], pe_v[lo:lo + half]], axis=1).reshape(1, half * 2 * HEAD_DIM)

    z = jnp.zeros_like(ck_w2)
    w2 = jnp.concatenate([jnp.concatenate([ck_w2, z], axis=1), jnp.concatenate([z, cv_w2], axis=1)], axis=0)
    return (pe_part(0), pe_part(half), _bf(w1_part(ck_w1, cv_w1, 0)), _bf(w1_part(ck_w1, cv_w1, half)), _bf(w2))


def _importance_matrix(seq):
    n_cmp = (seq - CMP_LEN) // CMP_STRIDE + 1
    n_sel = seq // SEL_LEN
    ratio = SEL_LEN // CMP_STRIDE
    n_ov = CMP_LEN // CMP_STRIDE
    n_pad = -(-n_cmp // LANES) * LANES
    assert n_sel <= LANES, "selection blocks must fit one lane group"
    assert min(SEL_TOPK, n_sel) > 3, "top-k must exceed the three forced blocks"
    mt = np.zeros((LANES, n_pad), np.float32)
    for j in range(n_sel):
        for m in range(ratio):
            for n in range(n_ov):
                c = ratio * j + m - n
                if 0 <= c < n_cmp:
                    mt[j, c] += 1.0
    return jnp.asarray(mt, MXU_DTYPE)


def _rope_table(seq):
    half = HEAD_DIM // 2
    inv = ROPE_THETA ** (-jnp.arange(half, dtype=jnp.float32) * 2.0 / HEAD_DIM)
    ang = jnp.arange(seq, dtype=jnp.float32)[:, None] * inv[None, :]
    cos, sin = jnp.cos(ang), jnp.sin(ang)
    return jnp.concatenate([cos, cos, sin, sin], axis=1)


def _block_diag(w):
    g, c, d = w.shape
    out = jnp.zeros((g * c, g * d), w.dtype)
    for j in range(g):
        out = out.at[j * c:(j + 1) * c, j * d:(j + 1) * d].set(w[j])
    return out


def kernel(x, norm_mix_pre, norm_mix_post, norm_ffn_pre, norm_ffn_post, w_in, w_out, conv_dw_w, conv_dw_b, conv_ln_g, conv_ln_b, nsa_pe_k, nsa_pe_v, nsa_ck_w1, nsa_ck_w2, nsa_cv_w1, nsa_cv_w2, sgu_ln_g, sgu_ln_b, sgu_w, sgu_b, pool_w, pool_scale, ffn_up, ffn_conv_w, ffn_conv_b, ffn_down):
    bsz, seq, d = x.shape
    depth = w_in.shape[0]
    ffn_dim = ffn_down.shape[1]
    cs = _rope_table(seq)
    mt = _importance_matrix(seq)
    rows = lambda v: v[:, None, :]

    w_ext = _prep_w_in(w_in)
    cmp_w = jax.vmap(_prep_compress)(nsa_pe_k, nsa_pe_v, nsa_ck_w1, nsa_ck_w2, nsa_cv_w1, nsa_cv_w2)
    sw_cat = jnp.transpose(sgu_w, (0, 2, 1, 3)).reshape(depth, SGU_CHUNK, SGU_GROUPS * SGU_CHUNK)
    sb_exp = jnp.repeat(jnp.swapaxes(sgu_b, 1, 2), 256 // SGU_GROUPS, axis=2)
    mix_w = [conv_dw_w, rows(conv_dw_b), rows(conv_ln_g), rows(conv_ln_b), rows(sgu_ln_g), rows(sgu_ln_b),
             sw_cat, sb_exp, _bf(jax.vmap(_block_diag)(pool_w)), rows(pool_scale), _bf(w_out), rows(norm_mix_post)]
    ffn_w = [rows(norm_ffn_pre), _bf(ffn_up), ffn_conv_w, rows(ffn_conv_b),
             _bf(ffn_down).reshape(depth, ffn_dim // FFN_CHUNK, FFN_CHUNK, d), rows(norm_ffn_post)]
    g_pre = rows(norm_mix_pre)

    for l in range(depth):
        za, zc, zd, qc, qr, a_cmp, ks, vs, kw, vw, gate = _in_proj(l, x, g_pre, w_ext, cs)
        kk, vv = _compress(l, a_cmp, *cmp_w)
        y_nsa = _nsa(qc, qr, gate, kk, vv, mt, ks, vs, kw, vw)
        x = _mix_out(l, x, za, zc, zd, y_nsa, *mix_w)
        x = _ffn(l, x, *ffn_w)
    return x
```

```python
import functools

import numpy as np
import jax
import jax.numpy as jnp
from jax import lax
from jax.experimental import pallas as pl
from jax.experimental.pallas import tpu as pltpu

HEAD_DIM = 64
NSA_HEADS = 4
CONV_K = 31
CMP_LEN = 32
CMP_STRIDE = 16
SEL_LEN = 64
SEL_TOPK = 16
WINDOW = 512
SGU_CHUNK = 128
SGU_GROUPS = 4
POOL_WINDOWS = (2, 4, 8, 16)
FFN_CONV_K = 3
ROPE_THETA = 10000.0
RMS_EPS = 1e-6
LN_EPS = 1e-5
NEG_INF = -1e30
FORCE_SCORE = 1e6
REMOVED = -3e38
LOG2E = 1.4426950408889634

LANES = 128
SUBLANES = 8
V_ROWS = 80
MXU_DTYPE = jnp.bfloat16
VMEM_LIMIT = 56 * 1024 * 1024

TM_PROJ = 512
TM_FFN = 1024
TQ = 256
TK = 512
FFN_CHUNK = 256
FFN_ROWS = 256
CONV_HALO = 32
POOL_HALO = 16
FFN_HALO = 16

_NT = (((1,), (1,)), ((), ()))


def _cparams(n_axes, flags=None):
    return pltpu.CompilerParams(dimension_semantics=("arbitrary",) * n_axes,
                                vmem_limit_bytes=VMEM_LIMIT, flags=flags)


def _layer_spec(a, l):
    return pl.BlockSpec((None,) + a.shape[1:], lambda *_: (l,) + (0,) * (a.ndim - 1),
                        pipeline_mode=pl.Buffered(1))


def _rms(x, g):
    return x * lax.rsqrt(jnp.mean(x * x, axis=-1, keepdims=True) + RMS_EPS) * g


def _layer_norm(x, g, b):
    mu = jnp.mean(x, axis=-1, keepdims=True)
    var = jnp.mean(jnp.square(x - mu), axis=-1, keepdims=True)
    return (x - mu) * lax.rsqrt(var + LN_EPS) * g + b


def _bf(x):
    return x.astype(MXU_DTYPE)


def _swap64(x):
    return pltpu.roll(x, 64, axis=1)


def _in_proj_kernel(x_ref, g_ref, w_ref, cs_ref,
                    za_ref, zc_ref, zd_ref, qc_ref, qr_ref, kv_ref,
                    ks_ref, vs_ref, kw_ref, vw_ref, gate_ref, kvt_ref, *, tm, tk):
    i = pl.program_id(1)
    h = _bf(_rms(x_ref[...], g_ref[...]))

    def proj(lo, hi):
        return jnp.dot(h, w_ref[:, lo:hi], preferred_element_type=jnp.float32)

    za_ref[...] = proj(0, 512)
    zc_ref[...] = proj(512, 1024)
    zd_ref[...] = proj(1024, 1280)
    q = proj(1280, 1536)
    q_sw = proj(1536, 1792)
    kv_ksl = proj(1792, 2048)
    kwn_vsw = proj(2048, 2304)
    kvt_ref[...] = kv_ksl[:, :LANES]
    group = CMP_LEN // 2
    for j in range(group):
        kv_ref[:, j * LANES:(j + 1) * LANES] = kvt_ref[pl.ds(j, tm // group, stride=group), :]
    ksl = kv_ksl[:, LANES:]
    kwn = kwn_vsw[:, :LANES]
    vsw = kwn_vsw[:, LANES:]
    gate_ref[...] = jax.nn.sigmoid(proj(2304, 2432))

    cs = cs_ref[...]
    lane = lax.broadcasted_iota(jnp.int32, (tm, LANES), 1)
    sc = _swap64(cs)
    cos2 = jnp.where(lane < 64, cs, sc)
    sin2 = jnp.where(lane < 64, sc, cs)
    scale = HEAD_DIM ** -0.5 * LOG2E
    qc_ref[...] = _bf(q * scale)
    for pair in range(NSA_HEADS // 2):
        c = slice(pair * LANES, (pair + 1) * LANES)
        qr_ref[:, c] = _bf((q[:, c] * cos2 + q_sw[:, c] * sin2) * scale)

    ts = ksl * cs
    k_rot2 = ts + _swap64(ts)
    tw = kwn * cs
    kw_rot2 = tw + _swap64(tw)
    pos = i * tm + lax.broadcasted_iota(jnp.int32, (tm, LANES), 0)
    onehot =jnp.where((pos >> 6) == lane, 1.0, 0.0)
    ks_ref[:, 0:LANES] = _bf(onehot)
    ks_ref[:, LANES:2 * LANES] = _bf(k_rot2)
    kw_ref[...] = _bf(kw_rot2)
    ones_col = jnp.where(lane == 64, 1.0, 0.0)
    vs = jnp.where(lane < 64, vsw, ones_col)
    vw = jnp.where(lane < 64, _swap64(vsw), ones_col)
    for j in range(tm // tk):
        vs_ref[j] = _bf(vs[j * tk:(j + 1) * tk].T[0:V_ROWS])
    for j in range(tm // TQ):
        vw_ref[j] = _bf(vw[j * TQ:(j + 1) * TQ].T[0:V_ROWS])


def _in_proj(l, x, g, w_ext, cs):
    bsz, seq, d = x.shape
    tm = min(TM_PROJ, seq)
    tk = min(TK, seq)
    tok = lambda c: pl.BlockSpec((None, tm, c), lambda b, i: (b, i, 0))
    f32, bf16 = jnp.float32, MXU_DTYPE

    def tok_out(c, dt):
        return tok(c), jax.ShapeDtypeStruct((bsz, seq, c), dt)

    def keytile_out(t):
        return (pl.BlockSpec((None, tm // t, V_ROWS, t), lambda b, i: (b, i, 0, 0)),
                jax.ShapeDtypeStruct((bsz, seq // t, V_ROWS, t), bf16))

    group = CMP_LEN // 2
    cmp_out = (pl.BlockSpec((None, tm // group, group * LANES), lambda b, i: (b, i, 0)),
               jax.ShapeDtypeStruct((bsz, seq // group, group * LANES), f32))
    outs = [tok_out(512, f32), tok_out(512, f32), tok_out(256, f32), tok_out(256, bf16), tok_out(256, bf16),
            cmp_out, tok_out(256, bf16), keytile_out(tk), tok_out(128, bf16), keytile_out(TQ),
            tok_out(128, f32)]
    return pl.pallas_call(
        functools.partial(_in_proj_kernel, tm=tm, tk=tk),
        grid=(bsz, seq // tm),
        in_specs=[tok(d), _layer_spec(g, l), _layer_spec(w_ext, l),
                  pl.BlockSpec((tm, LANES), lambda b, i: (i, 0))],
        out_specs=[spec for spec, _ in outs],
        out_shape=[shape for _, shape in outs],
        scratch_shapes=[pltpu.VMEM((tm, LANES), f32)],
        compiler_params=_cparams(2),
        name="in_proj",
    )(x, g, w_ext, cs)


def _compress_kernel(a_ref, pe_lo_ref, pe_hi_ref, w_lo_ref, w_hi_ref, w2_ref, kk_ref, vv_ref):
    a = a_ref[...]
    u = jnp.dot(_bf(a + pe_lo_ref[...]), w_lo_ref[...], preferred_element_type=jnp.float32)
    v = jnp.dot(_bf(a + pe_hi_ref[...]), w_hi_ref[...], preferred_element_type=jnp.float32)
    n = a.shape[0]
    pre = u + pltpu.roll(v, n - 1, axis=0)
    act = jax.nn.gelu(pre, approximate=True)
    kv = jnp.dot(_bf(act), w2_ref[...], preferred_element_type=jnp.float32)
    vk = _swap64(kv)
    lane = lax.broadcasted_iota(jnp.int32, (n, LANES), 1)
    kk_ref[...] = _bf(jnp.where(lane < 64, kv, vk))
    vv = jnp.where(lane < 64, vk, jnp.where(lane == 64, 1.0, 0.0))
    vv_ref[...] = _bf(vv.T[0:V_ROWS])


def _compress(l, a, pe_lo, pe_hi, w_lo, w_hi, w2):
    bsz, n, width = a.shape
    per_b = lambda c: pl.BlockSpec((None, n, c), lambda b: (b, 0, 0))
    return pl.pallas_call(
        _compress_kernel,
        grid=(bsz,),
        in_specs=[per_b(width)] + [_layer_spec(w, l) for w in (pe_lo, pe_hi, w_lo, w_hi, w2)],
        out_specs=[per_b(LANES), pl.BlockSpec((None, V_ROWS, n), lambda b: (b, 0, 0))],
        out_shape=[jax.ShapeDtypeStruct((bsz, n, LANES), MXU_DTYPE),
                   jax.ShapeDtypeStruct((bsz, V_ROWS, n), MXU_DTYPE)],
        compiler_params=_cparams(1),
        name="compress",
    )(a, pe_lo, pe_hi, w_lo, w_hi, w2)


def _nsa_kernel(qc_ref, qr_ref, gate_ref, kk_ref, vv_ref, mt_ref, ks_ref, vs_ref, kw_ref, vw_ref,
                out_ref, m_ref, acc_ref, s0_ref, s1_ref, ocmp_ref, imp_ref, *, tq, tk, seq):
    qi = pl.program_id(1)
    t0 = qi * tq
    rows = NSA_HEADS * tq
    n_cmp_pad = kk_ref.shape[0]
    top_n = min(SEL_TOPK, seq // SEL_LEN)
    n_sel = LANES
    win_keys = min(WINDOW + tq, seq)

    lane =lax.broadcasted_iota(jnp.int32, (tq, LANES), 1)

    def head_rows(q_ref):
        parts = []
        for hd in range(NSA_HEADS):
            pair = q_ref[:, (hd // 2) * LANES:(hd // 2 + 1) * LANES]
            keep = (lane < 64) if hd % 2 == 0 else (lane >= 64)
            parts.append(jnp.where(keep, pair, jnp.zeros_like(pair)))
        return jnp.concatenate(parts, axis=0)

    q_c = head_rows(qc_ref)
    q_r = head_rows(qr_ref)
    pos_l = t0 + (lax.broadcasted_iota(jnp.int32, (1, rows), 1) & (tq - 1))

    last_blk = (pos_l - (CMP_LEN - 1)) >> 4

    def compressed(nr):
        s = lax.dot_general(kk_ref[0:nr, :], q_c, _NT, preferred_element_type=jnp.float32)
        n_idx = lax.broadcasted_iota(jnp.int32, (nr, rows), 0)
        sm = jnp.where(n_idx <= last_blk, s, NEG_INF)
        mx = jnp.max(sm, axis=0, keepdims=True)
        e = jnp.exp2(sm - mx)
        den = jnp.sum(e, axis=0, keepdims=True)
        p_cmp = e * jnp.where(last_blk >= 0, 1.0 / den, 0.0)
        ocmp_ref[...] = jnp.dot(vv_ref[:, 0:nr], _bf(p_cmp), preferred_element_type=jnp.float32)
        p_sum = p_cmp[:, 0:tq] + p_cmp[:, tq:2 * tq] + p_cmp[:, 2 * tq:3 * tq] + p_cmp[:, 3 * tq:4 * tq]
        hi = _bf(p_sum)
        r1 = p_sum - hi.astype(jnp.float32)
        mid = _bf(r1)
        lo = _bf(r1 - mid.astype(jnp.float32))
        mt = mt_ref[:, 0:nr]
        imp_ref[...] = (jnp.dot(mt, hi, preferred_element_type=jnp.float32)
                        + jnp.dot(mt, mid, preferred_element_type=jnp.float32)
                        + jnp.dot(mt, lo, preferred_element_type=jnp.float32))

    n_var = n_cmp_pad // LANES
    tile_last = jnp.maximum((t0 + tq - CMP_LEN) >> 4, 0)
    need = jnp.minimum(tile_last // LANES, n_var - 1)
    for v in range(n_var):
        pl.when(need == v)(functools.partial(compressed, (v + 1) * LANES))
    o_cmp = ocmp_ref[...]
    imp = imp_ref[...]

    start = pl.multiple_of(jnp.maximum(t0 + tq - win_keys, 0), tq)
    sw = lax.dot_general(kw_ref[pl.ds(start, win_keys), :], q_r, _NT, preferred_element_type=jnp.float32)
    back = pos_l - (start + lax.broadcasted_iota(jnp.int32, (win_keys, rows), 0))
    in_win = lax.bitcast_convert_type(back, jnp.uint32) < WINDOW
    sw = jnp.where(in_win, sw, NEG_INF)
    mw = jnp.max(sw, axis=0, keepdims=True)
    pw = _bf(jnp.exp2(sw - mw))
    wt0 = start // tq
    ow = jnp.dot(vw_ref[wt0], pw[0:tq], preferred_element_type=jnp.float32)
    for j in range(1, win_keys // tq):
        ow = ow + jnp.dot(vw_ref[wt0 + j], pw[j * tq:(j + 1) * tq], preferred_element_type=jnp.float32)
    o_win = ow / ow[64:65, :]

    blk = lax.broadcasted_iota(jnp.int32, (n_sel, tq), 0)
    blk_f = blk.astype(jnp.float32)
    pos_t = t0 + lax.broadcasted_iota(jnp.int32, (n_sel, tq), 1)
    cur = pos_t >> 6
    forced = (blk == 0) | (blk == cur) | (blk == cur - 1)
    valid = (blk * SEL_LEN) <= pos_t
    val = jnp.where(valid & jnp.logical_not(forced), imp, NEG_INF)
    for _ in range(top_n - 3):
        best = jnp.max(val, axis=0, keepdims=True)
        first = jnp.min(jnp.where(val == best, blk_f, float(n_sel)), axis=0, keepdims=True)
        val = jnp.where(blk_f == first, REMOVED, val)
    bias_t = jnp.where((forced | (val == REMOVED)) & valid, 0.0, NEG_INF)
    bias = _bf(bias_t.T)

    l_sel = jnp.concatenate(
        [jnp.concatenate([bias, q_r[hd * tq:(hd + 1) * tq]], axis=1) for hd in range(NSA_HEADS)], axis=0)
    m_ref[...] = jnp.full((1, rows), NEG_INF, jnp.float32)
    acc_ref[...] = jnp.zeros((V_ROWS, rows), jnp.float32)

    def scores_to(s_ref, kt, causal):
        k0 = pl.multiple_of(kt * tk, tk)
        sc = lax.dot_general(ks_ref[pl.ds(k0, tk), :], l_sel, _NT, preferred_element_type=jnp.float32)
        if causal:
            kpos = k0 + lax.broadcasted_iota(jnp.int32, (tk, rows), 0)
            sc = jnp.where(kpos <= pos_l, sc, NEG_INF)
        s_ref[...] = sc

    def softmax_from(s_ref, kt):
        sc = s_ref[...]
        m_old = m_ref[...]
        m_new = jnp.maximum(m_old, jnp.max(sc, axis=0, keepdims=True))
        alpha = jnp.exp2(m_old - m_new)
        p = jnp.exp2(sc - m_new)
        acc_ref[...] = alpha * acc_ref[...] + jnp.dot(vs_ref[kt], _bf(p), preferred_element_type=jnp.float32)
        m_ref[...] = m_new

    n_full = t0 // tk
    n_pairs = n_full // 2
    scores_to(s0_ref, n_full, True)

    def pair(j, carry):
        k = 2 * j
        scores_to(s1_ref, k, False)
        softmax_from(s0_ref, jnp.where(j == 0, n_full, k - 1))
        scores_to(s0_ref, k + 1, False)
        softmax_from(s1_ref, k)
        return carry

    lax.fori_loop(0, n_pairs, pair, 0)
    pending = jnp.where(n_pairs == 0, n_full, 2 * n_pairs - 1)

    @pl.when(n_full % 2 == 1)
    def _():
        scores_to(s1_ref, n_full - 1, False)
        softmax_from(s0_ref, pending)
        softmax_from(s1_ref, n_full - 1)

    @pl.when(n_full % 2 == 0)
    def _():
        softmax_from(s0_ref, pending)

    acc = acc_ref[...]
    o_slc = acc / acc[64:65, :]

    gate_t = gate_ref[...].T
    y_t = []
    for hd in range(NSA_HEADS):
        c = slice(hd * tq, (hd + 1) * tq)
        y_t.append(gate_t[3 * hd:3 * hd + 1] * o_cmp[0:64, c]
                   + gate_t[3 * hd + 1:3 * hd + 2] * o_slc[0:64, c]
                   + gate_t[3 * hd + 2:3 * hd + 3] * o_win[0:64, c])
    out_ref[:, 0:LANES] = _bf(jnp.concatenate(y_t[0:2], axis=0).T)
    out_ref[:, LANES:2 * LANES] = _bf(jnp.concatenate(y_t[2:4], axis=0).T)


def _nsa(qc, qr, gate, kk, vv, mt, ks, vs, kw, vw):
    bsz, seq, _ = qc.shape
    tq = min(TQ, seq)
    tk = min(TK, seq)
    n_cmp_pad = kk.shape[1]
    rows = NSA_HEADS * tq
    tile = lambda c: pl.BlockSpec((None, tq, c), lambda b, i: (b, i, 0))
    per_b = lambda r, c: pl.BlockSpec((None, r, c), lambda b, i: (b, 0, 0))
    per_b4 = lambda a: pl.BlockSpec((None,) + a.shape[1:], lambda b, i: (b, 0, 0, 0))
    return pl.pallas_call(
        functools.partial(_nsa_kernel, tq=tq, tk=tk, seq=seq),
        grid=(bsz, seq // tq),
        in_specs=[tile(256), tile(256), tile(LANES),
                  per_b(n_cmp_pad, LANES), per_b(V_ROWS, n_cmp_pad),
                  pl.BlockSpec(mt.shape, lambda b, i: (0, 0)),
                  per_b(seq, 256), per_b4(vs), per_b(seq, LANES), per_b4(vw)],
        out_specs=tile(256),
        out_shape=jax.ShapeDtypeStruct((bsz, seq, 256), MXU_DTYPE),
        scratch_shapes=[pltpu.VMEM((1, rows), jnp.float32), pltpu.VMEM((V_ROWS, rows), jnp.float32),
                        pltpu.VMEM((tk, rows), jnp.float32), pltpu.VMEM((tk, rows), jnp.float32),
                        pltpu.VMEM((V_ROWS, rows), jnp.float32), pltpu.VMEM((LANES, tq), jnp.float32)],
        compiler_params=_cparams(2),
        name="nsa",
    )(qc, qr, gate, kk, vv, mt, ks, vs, kw, vw)


def _mix_out_kernel(x_ref, za_ref, zah_ref, zc_ref, zd_ref, zdh_ref, ynsa_ref,
                    cw_ref, cb_ref, clg_ref, clb_ref, slg_ref, slb_ref, sw_ref, sb_ref,
                    pw_ref, ps_ref, wout_ref, gpost_ref, out_ref,
                    uext_ref, ushift_ref, ycat_ref, pext_ref, *, tm):
    i = pl.program_id(1)
    not_first = jnp.where(i > 0, 1.0, 0.0)

    zc = jax.nn.gelu(zc_ref[...], approximate=True)
    u_s = zc[:, :256]
    v_s = _layer_norm(zc[:, 256:], slg_ref[...], slb_ref[...])
    t_i = lax.broadcasted_iota(jnp.int32, (SGU_CHUNK, SGU_GROUPS * SGU_CHUNK), 0)
    s_i = lax.rem(lax.broadcasted_iota(jnp.int32, (SGU_CHUNK, SGU_GROUPS * SGU_CHUNK), 1), SGU_CHUNK)
    w_cat = _bf(jnp.where(s_i <= t_i, sw_ref[...], 0.0))
    grp = lax.broadcasted_iota(jnp.int32, (SGU_CHUNK, 256), 1) >> 6
    ysgu = []
    for c in range(tm // SGU_CHUNK):
        vc = v_s[c * SGU_CHUNK:(c + 1) * SGU_CHUNK]
        stacked = _bf(jnp.concatenate([jnp.where(grp == g, vc, 0.0) for g in range(SGU_GROUPS)], axis=0))
        f = jnp.dot(w_cat, stacked, preferred_element_type=jnp.float32) + sb_ref[...]
        ysgu.append(u_s[c * SGU_CHUNK:(c + 1) * SGU_CHUNK] * f)
    ycat_ref[:, 512:768] = _bf(jnp.concatenate(ysgu, axis=0))

    pext_ref[0:POOL_HALO, :] = zdh_ref[...] * not_first
    pext_ref[POOL_HALO:, :] = zd_ref[...]
    n_ext = tm + POOL_HALO
    sums = []
    for shift in (1, 2, 4, 8):
        n = n_ext - shift
        nxt = pext_ref[pl.ds(shift, n), :] + pext_ref[pl.ds(0, n), :]
        pext_ref[pl.ds(shift, n), :] = nxt
        sums.append(pext_ref[pl.ds(POOL_HALO, tm), :])
    lane_grp = lax.broadcasted_iota(jnp.int32, (tm, 256), 1) >> 6
    pos = i * tm + lax.broadcasted_iota(jnp.int32, (tm, 256), 0)
    wsum = jnp.where(lane_grp == 0, sums[0], jnp.where(lane_grp == 1, sums[1],
                     jnp.where(lane_grp == 2, sums[2], sums[3])))
    width = jnp.where(lane_grp == 0, POOL_WINDOWS[0], jnp.where(lane_grp == 1, POOL_WINDOWS[1],
                      jnp.where(lane_grp == 2, POOL_WINDOWS[2], POOL_WINDOWS[3])))
    cnt = jnp.minimum(pos + 1, width).astype(jnp.float32)
    pooled = wsum / cnt - zd_ref[...]
    ycat_ref[:, 768:1024] = _bf(jnp.dot(_bf(pooled), pw_ref[...], preferred_element_type=jnp.float32) * ps_ref[...])

    ycat_ref[:, 256:512] = ynsa_ref[...]
    y_rest = (jnp.dot(ycat_ref[:, 256:512], wout_ref[256:512, :], preferred_element_type=jnp.float32)
              + jnp.dot(ycat_ref[:, 512:768], wout_ref[512:768, :], preferred_element_type=jnp.float32)
              + jnp.dot(ycat_ref[:, 768:1024], wout_ref[768:1024, :], preferred_element_type=jnp.float32))

    def glu(z):
        return z[:, :256] * jax.nn.sigmoid(z[:, 256:])

    uext_ref[0:CONV_HALO, :] = glu(zah_ref[...]) * not_first
    uext_ref[CONV_HALO:, :] = glu(za_ref[...])
    first = CONV_HALO - (CONV_K - 1)
    n_shift = tm + CONV_HALO - SUBLANES
    for r in range(SUBLANES):
        n = n_shift if first + r + n_shift <= tm + CONV_HALO else n_shift - SUBLANES
        ushift_ref[r, 0:n, :] = uext_ref[pl.ds(first + r, n), :]
    rc = 64
    for r0 in range(0, tm, rc):
        acc = jnp.zeros((rc, 256), jnp.float32)
        for k in range(CONV_K):
            acc = acc + cw_ref[k:k + 1, :] * ushift_ref[k % SUBLANES, pl.ds(r0 + k - k % SUBLANES, rc), :]
        yc = _layer_norm(acc + cb_ref[...], clg_ref[...], clb_ref[...])
        ycat_ref[r0:r0 + rc, 0:256] = _bf(yc * jax.nn.sigmoid(yc))

    y = y_rest + jnp.dot(ycat_ref[:, 0:256], wout_ref[0:256, :], preferred_element_type=jnp.float32)
    out_ref[...] = x_ref[...] + _rms(y, gpost_ref[...])


def _mix_out(l, x, za, zc, zd, ynsa, cw, cb, clg, clb, slg, slb, sw_cat, sb_exp, pw_bd, ps, wout, gpost):
    bsz, seq, d = x.shape
    tm = min(TM_PROJ, seq)
    tok = lambda c: pl.BlockSpec((None, tm, c), lambda b, i: (b, i, 0))
    halo = lambda c, h: pl.BlockSpec((None, h, c), lambda b, i: (b, jnp.maximum(i * (tm // h) - 1, 0), 0))
    weights = [cw, cb, clg, clb, slg, slb, sw_cat, sb_exp, pw_bd, ps, wout, gpost]
    return pl.pallas_call(
        functools.partial(_mix_out_kernel, tm=tm),
        grid=(bsz, seq // tm),
        in_specs=[tok(d), tok(512), halo(512, CONV_HALO), tok(512), tok(256), halo(256, POOL_HALO), tok(256)]
                 + [_layer_spec(w, l) for w in weights],
        out_specs=tok(d),
        out_shape=jax.ShapeDtypeStruct((bsz, seq, d), jnp.float32),
        scratch_shapes=[pltpu.VMEM((tm + CONV_HALO, 256), jnp.float32),
                        pltpu.VMEM((SUBLANES, tm + CONV_HALO - SUBLANES, 256), jnp.float32),
                        pltpu.VMEM((tm, 4 * 256), MXU_DTYPE),
                        pltpu.VMEM((tm + POOL_HALO, 256), jnp.float32)],
        compiler_params=_cparams(2),
        name="mix_out",
    )(x, za, za, zc, zd, zd, ynsa, *weights)


def _ffn_kernel(x_ref, xh_ref, gpre_ref, wup_ref, cw_ref, cb_ref, wdn_ref, gpost_ref, out_ref,
                hext_ref, g0_ref, u0_ref, g1_ref, u1_ref, acc_ref, *, tm, n_chunks, chunk, rb):
    i = pl.program_id(1)
    not_first = jnp.where(i > 0, 1.0, 0.0)
    x = x_ref[...]
    hext_ref[0:FFN_HALO, :] = _bf(_rms(xh_ref[...], gpre_ref[...]) * not_first)
    hext_ref[FFN_HALO:, :] = _bf(_rms(x, gpre_ref[...]))

    def cols(c):
        return pl.ds(pl.multiple_of(c * chunk, chunk), chunk)

    n_rb = tm // rb

    def ext_rows(r):
        lo = 0 if r == 0 else FFN_HALO + r * rb
        return lo, FFN_HALO + (r + 1) * rb

    def up_rows(g_ref, u_ref, c, r):
        lo, hi = ext_rows(r)
        h_ext = hext_ref[lo:hi, :]
        g_ref[lo:hi, :] = jnp.dot(h_ext, wup_ref[:, cols(c)], preferred_element_type=jnp.float32)
        u_ref[lo:hi, :] = jnp.dot(h_ext, wup_ref[:, cols(n_chunks + c)], preferred_element_type=jnp.float32)

    def conv(ext_ref, c, r):
        w = cw_ref[:, cols(c)]
        t0 = FFN_HALO + r * rb
        return (w[0:1] * ext_ref[pl.ds(t0 - 2, rb), :] + w[1:2] * ext_ref[pl.ds(t0 - 1, rb), :]
                + w[2:3] * ext_ref[pl.ds(t0, rb), :] + cb_ref[:, cols(c)])

    def down_rows(g_ref, u_ref, c, r):
        act = jax.nn.gelu(conv(g_ref, c, r), approximate=True) * conv(u_ref, n_chunks + c, r)
        acc_ref[r * rb:(r + 1) * rb, :] += jnp.dot(_bf(act), wdn_ref[c], preferred_element_type=jnp.float32)

    def step(nxt, c_up, cur, c_down):
        for r in range(n_rb):
            if c_up is not None:
                up_rows(*nxt, c_up, r)
            if c_down is not None:
                down_rows(*cur, c_down, r)

    assert n_chunks % 2 == 1
    buf0, buf1 = (g0_ref, u0_ref), (g1_ref, u1_ref)
    acc_ref[...] = jnp.zeros_like(acc_ref)
    step(buf0, 0, None, None)

    def pair(j, carry):
        c = 2 * j
        step(buf1, c + 1, buf0, c)
        step(buf0, c + 2, buf1, c + 1)
        return carry

    lax.fori_loop(0, (n_chunks - 1) // 2, pair, 0)
    step(None, None, buf0, n_chunks - 1)
    out_ref[...] = x + _rms(acc_ref[...], gpost_ref[...])


def _ffn(l, x, gpre, wup, cw, cb, wdn, gpost):
    bsz, seq, d = x.shape
    tm = min(TM_FFN, seq)
    n_chunks = wdn.shape[1]
    chunk = wdn.shape[2]
    tok = pl.BlockSpec((None, tm, d), lambda b, i: (b, i, 0))
    halo = pl.BlockSpec((None, FFN_HALO, d), lambda b, i: (b, jnp.maximum(i * (tm // FFN_HALO) - 1, 0), 0))
    weights = [gpre, wup, cw, cb, wdn, gpost]
    return pl.pallas_call(
        functools.partial(_ffn_kernel, tm=tm, n_chunks=n_chunks, chunk=chunk, rb=min(FFN_ROWS, tm)),
        grid=(bsz, seq // tm),
        in_specs=[tok, halo] + [_layer_spec(w, l) for w in weights],
        out_specs=tok,
        out_shape=jax.ShapeDtypeStruct((bsz, seq, d), jnp.float32),
        scratch_shapes=[pltpu.VMEM((tm + FFN_HALO, d), MXU_DTYPE)]
                       + [pltpu.VMEM((tm + FFN_HALO, chunk), jnp.float32)] * 4
                       + [pltpu.VMEM((tm, d), jnp.float32)],
        compiler_params=_cparams(2),
        name="ffn",
    )(x, x, *weights)


def _swap_halves(w):
    k = w.shape[-1] // HEAD_DIM
    w3 = w.reshape(w.shape[:-1] + (k, HEAD_DIM))
    half = HEAD_DIM // 2
    return jnp.concatenate([-w3[..., half:], w3[..., :half]], axis=-1).reshape(w.shape)


def _prep_w_in(w):
    d_mix = 1024
    c_conv, c_nsa, c_sgu, c_pool = d_mix // 4, NSA_HEADS * HEAD_DIM, d_mix // 4, d_mix // 4
    sizes = (2 * c_conv, c_nsa, 6 * HEAD_DIM, 3 * NSA_HEADS, 2 * c_sgu, c_pool)
    offs = np.cumsum((0,) + sizes)
    wa, wq, wkv, wg, wc, wd = [w[..., offs[j]:offs[j + 1]] for j in range(6)]
    kv = [wkv[..., j * HEAD_DIM:(j + 1) * HEAD_DIM] for j in range(6)]
    wg_pad = jnp.pad(wg, ((0, 0), (0, 0), (0, LANES - wg.shape[-1])))
    cols = [wa, wc, wd, wq, _swap_halves(wq), kv[0], kv[1],
            kv[2], _swap_halves(kv[2]), kv[4], _swap_halves(kv[4]), kv[3], kv[5], wg_pad]
    return _bf(jnp.concatenate(cols, axis=-1))


def _prep_compress(pe_k, pe_v, ck_w1, ck_w2, cv_w1, cv_w2):
    half = CMP_LEN // 2

    def w1_part(w1k, w1v, lo):
        wk = w1k.reshape(CMP_LEN, HEAD_DIM, HEAD_DIM)[lo:lo + half]
        wv = w1v.reshape(CMP_LEN, HEAD_DIM, HEAD_DIM)[lo:lo + half]
        z = jnp.zeros_like(wk)
        top = jnp.concatenate([wk, z], axis=-1)
        bot = jnp.concatenate([z, wv], axis=-1)
        return jnp.concatenate([top, bot], axis=1).reshape(half * 2 * HEAD_DIM, 2 * HEAD_DIM)

    def pe_part(lo):
        return jnp.concatenate([pe_k[lo:lo + half], pe_v[lo:lo + half]], axis=1).reshape(1, half * 2 * HEAD_DIM)

    z = jnp.zeros_like(ck_w2)
    w2 = jnp.concatenate([jnp.concatenate([ck_w2, z], axis=1), jnp.concatenate([z, cv_w2], axis=1)], axis=0)
    return (pe_part(0), pe_part(half), _bf(w1_part(ck_w1, cv_w1, 0)), _bf(w1_part(ck_w1, cv_w1, half)), _bf(w2))


def _importance_matrix(seq):
    n_cmp = (seq - CMP_LEN) // CMP_STRIDE + 1
    n_sel = seq // SEL_LEN
    ratio = SEL_LEN // CMP_STRIDE
    n_ov = CMP_LEN // CMP_STRIDE
    n_pad = -(-n_cmp // LANES) * LANES
    assert n_sel <= LANES, "selection blocks must fit one lane group"
    assert min(SEL_TOPK, n_sel) > 3, "top-k must exceed the three forced blocks"
    mt = np.zeros((LANES, n_pad), np.float32)
    for j in range(n_sel):
        for m in range(ratio):
            for n in range(n_ov):
                c = ratio * j + m - n
                if 0 <= c < n_cmp:
                    mt[j, c] += 1.0
    return jnp.asarray(mt, MXU_DTYPE)


def _rope_table(seq):
    half = HEAD_DIM // 2
    inv = ROPE_THETA ** (-jnp.arange(half, dtype=jnp.float32) * 2.0 / HEAD_DIM)
    ang = jnp.arange(seq, dtype=jnp.float32)[:, None] * inv[None, :]
    cos, sin = jnp.cos(ang), jnp.sin(ang)
    return jnp.concatenate([cos, cos, sin, sin], axis=1)


def _block_diag(w):
    g, c, d = w.shape
    out = jnp.zeros((g * c, g * d), w.dtype)
    for j in range(g):
        out = out.at[j * c:(j + 1) * c, j * d:(j + 1) * d].set(w[j])
    return out


def kernel(x, norm_mix_pre, norm_mix_post, norm_ffn_pre, norm_ffn_post, w_in, w_out, conv_dw_w, conv_dw_b, conv_ln_g, conv_ln_b, nsa_pe_k, nsa_pe_v, nsa_ck_w1, nsa_ck_w2, nsa_cv_w1, nsa_cv_w2, sgu_ln_g, sgu_ln_b, sgu_w, sgu_b, pool_w, pool_scale, ffn_up, ffn_conv_w, ffn_conv_b, ffn_down):
    bsz, seq, d = x.shape
    depth = w_in.shape[0]
    ffn_dim = ffn_down.shape[1]
    cs = _rope_table(seq)
    mt = _importance_matrix(seq)
    rows = lambda v: v[:, None, :]

    w_ext = _prep_w_in(w_in)
    cmp_w = jax.vmap(_prep_compress)(nsa_pe_k, nsa_pe_v, nsa_ck_w1, nsa_ck_w2, nsa_cv_w1, nsa_cv_w2)
    sw_cat = jnp.transpose(sgu_w, (0, 2, 1, 3)).reshape(depth, SGU_CHUNK, SGU_GROUPS * SGU_CHUNK)
    sb_exp = jnp.repeat(jnp.swapaxes(sgu_b, 1, 2), 256 // SGU_GROUPS, axis=2)
    mix_w = [conv_dw_w, rows(conv_dw_b), rows(conv_ln_g), rows(conv_ln_b), rows(sgu_ln_g), rows(sgu_ln_b),
             sw_cat, sb_exp, _bf(jax.vmap(_block_diag)(pool_w)), rows(pool_scale), _bf(w_out), rows(norm_mix_post)]
    ffn_w = [rows(norm_ffn_pre), _bf(ffn_up), ffn_conv_w, rows(ffn_conv_b),
             _bf(ffn_down).reshape(depth, ffn_dim // FFN_CHUNK, FFN_CHUNK, d), rows(norm_ffn_post)]
    g_pre = rows(norm_mix_pre)

    for l in range(depth):
        za, zc, zd, qc, qr, a_cmp, ks, vs, kw, vw, gate = _in_proj(l, x, g_pre, w_ext, cs)
        kk, vv = _compress(l, a_cmp, *cmp_w)
        y_nsa = _nsa(qc, qr, gate, kk, vv, mt, ks, vs, kw, vw)
        x = _mix_out(l, x, za, zc, zd, y_nsa, *mix_w)
        x = _ffn(l, x, *ffn_w)
    return x
```

```python
import functools

import numpy as np
import jax
import jax.numpy as jnp
from jax import lax
from jax.experimental import pallas as pl
from jax.experimental.pallas import tpu as pltpu

HEAD_DIM = 64
NSA_HEADS = 4
CONV_K = 31
CMP_LEN = 32
CMP_STRIDE = 16
SEL_LEN = 64
SEL_TOPK = 16
WINDOW = 512
SGU_CHUNK = 128
SGU_GROUPS = 4
POOL_WINDOWS = (2, 4, 8, 16)
FFN_CONV_K = 3
ROPE_THETA = 10000.0
RMS_EPS = 1e-6
LN_EPS = 1e-5
NEG_INF = -1e30
FORCE_SCORE = 1e6
REMOVED = -3e38
LOG2E = 1.4426950408889634

LANES = 128
SUBLANES = 8
V_ROWS = 80
MXU_DTYPE = jnp.bfloat16
VMEM_LIMIT = 56 * 1024 * 1024

TM_PROJ = 512
TM_FFN = 1024
TQ = 256
TK = 512
FFN_CHUNK = 256
CONV_HALO = 32
POOL_HALO = 16
FFN_HALO = 16

_NT = (((1,), (1,)), ((), ()))


def _cparams(n_axes, flags=None):
    return pltpu.CompilerParams(dimension_semantics=("arbitrary",) * n_axes,
                                vmem_limit_bytes=VMEM_LIMIT, flags=flags)


def _layer_spec(a, l):
    return pl.BlockSpec((None,) + a.shape[1:], lambda *_: (l,) + (0,) * (a.ndim - 1),
                        pipeline_mode=pl.Buffered(1))


def _rms(x, g):
    return x * lax.rsqrt(jnp.mean(x * x, axis=-1, keepdims=True) + RMS_EPS) * g


def _layer_norm(x, g, b):
    mu = jnp.mean(x, axis=-1, keepdims=True)
    var = jnp.mean(jnp.square(x - mu), axis=-1, keepdims=True)
    return (x - mu) * lax.rsqrt(var + LN_EPS) * g + b


def _bf(x):
    return x.astype(MXU_DTYPE)


def _swap64(x):
    return pltpu.roll(x, 64, axis=1)


def _in_proj_kernel(x_ref, g_ref, w_ref, cs_ref,
                    za_ref, zc_ref, zd_ref, qc_ref, qr_ref, kv_ref,
                    ks_ref, vs_ref, kw_ref, vw_ref, gate_ref, kvt_ref, *, tm, tk):
    i = pl.program_id(1)
    h = _bf(_rms(x_ref[...], g_ref[...]))

    def proj(lo, hi):
        return jnp.dot(h, w_ref[:, lo:hi], preferred_element_type=jnp.float32)

    za_ref[...] = proj(0, 512)
    zc_ref[...] = proj(512, 1024)
    zd_ref[...] = proj(1024, 1280)
    q = proj(1280, 1536)
    q_sw = proj(1536, 1792)
    kv_ksl = proj(1792, 2048)
    kwn_vsw = proj(2048, 2304)
    kvt_ref[...] = kv_ksl[:, :LANES]
    group = CMP_LEN // 2
    for j in range(group):
        kv_ref[:, j * LANES:(j + 1) * LANES] = kvt_ref[pl.ds(j, tm // group, stride=group), :]
    ksl = kv_ksl[:, LANES:]
    kwn = kwn_vsw[:, :LANES]
    vsw = kwn_vsw[:, LANES:]
    gate_ref[...] = jax.nn.sigmoid(proj(2304, 2432))

    cs = cs_ref[...]
    lane = lax.broadcasted_iota(jnp.int32, (tm, LANES), 1)
    sc = _swap64(cs)
    cos2 = jnp.where(lane < 64, cs, sc)
    sin2 = jnp.where(lane < 64, sc, cs)
    scale = HEAD_DIM ** -0.5 * LOG2E
    qc_ref[...] = _bf(q * scale)
    for pair in range(NSA_HEADS // 2):
        c = slice(pair * LANES, (pair + 1) * LANES)
        qr_ref[:, c] = _bf((q[:, c] * cos2 + q_sw[:, c] * sin2) * scale)

    ts = ksl * cs
    k_rot2 = ts + _swap64(ts)
    tw = kwn * cs
    kw_rot2 = tw + _swap64(tw)
    pos = i * tm + lax.broadcasted_iota(jnp.int32, (tm, LANES), 0)
    onehot =jnp.where((pos >> 6) == lane, 1.0, 0.0)
    ks_ref[:, 0:LANES] = _bf(onehot)
    ks_ref[:, LANES:2 * LANES] = _bf(k_rot2)
    kw_ref[...] = _bf(kw_rot2)
    ones_col = jnp.where(lane == 64, 1.0, 0.0)
    vs = jnp.where(lane < 64, vsw, ones_col)
    vw = jnp.where(lane < 64, _swap64(vsw), ones_col)
    for j in range(tm // tk):
        vs_ref[j] = _bf(vs[j * tk:(j + 1) * tk].T[0:V_ROWS])
    for j in range(tm // TQ):
        vw_ref[j] = _bf(vw[j * TQ:(j + 1) * TQ].T[0:V_ROWS])


def _in_proj(l, x, g, w_ext, cs):
    bsz, seq, d = x.shape
    tm = min(TM_PROJ, seq)
    tk = min(TK, seq)
    tok = lambda c: pl.BlockSpec((None, tm, c), lambda b, i: (b, i, 0))
    f32, bf16 = jnp.float32, MXU_DTYPE

    def tok_out(c, dt):
        return tok(c), jax.ShapeDtypeStruct((bsz, seq, c), dt)

    def keytile_out(t):
        return (pl.BlockSpec((None, tm // t, V_ROWS, t), lambda b, i: (b, i, 0, 0)),
                jax.ShapeDtypeStruct((bsz, seq // t, V_ROWS, t), bf16))

    group = CMP_LEN // 2
    cmp_out = (pl.BlockSpec((None, tm // group, group * LANES), lambda b, i: (b, i, 0)),
               jax.ShapeDtypeStruct((bsz, seq // group, group * LANES), f32))
    outs = [tok_out(512, f32), tok_out(512, f32), tok_out(256, f32), tok_out(256, bf16), tok_out(256, bf16),
            cmp_out, tok_out(256, bf16), keytile_out(tk), tok_out(128, bf16), keytile_out(TQ),
            tok_out(128, f32)]
    return pl.pallas_call(
        functools.partial(_in_proj_kernel, tm=tm, tk=tk),
        grid=(bsz, seq // tm),
        in_specs=[tok(d), _layer_spec(g, l), _layer_spec(w_ext, l),
                  pl.BlockSpec((tm, LANES), lambda b, i: (i, 0))],
        out_specs=[spec for spec, _ in outs],
        out_shape=[shape for _, shape in outs],
        scratch_shapes=[pltpu.VMEM((tm, LANES), f32)],
        compiler_params=_cparams(2),
        name="in_proj",
    )(x, g, w_ext, cs)


def _compress_kernel(a_ref, pe_lo_ref, pe_hi_ref, w_lo_ref, w_hi_ref, w2_ref, kk_ref, vv_ref):
    a = a_ref[...]
    u = jnp.dot(_bf(a + pe_lo_ref[...]), w_lo_ref[...], preferred_element_type=jnp.float32)
    v = jnp.dot(_bf(a + pe_hi_ref[...]), w_hi_ref[...], preferred_element_type=jnp.float32)
    n = a.shape[0]
    pre = u + pltpu.roll(v, n - 1, axis=0)
    act = jax.nn.gelu(pre, approximate=True)
    kv = jnp.dot(_bf(act), w2_ref[...], preferred_element_type=jnp.float32)
    vk = _swap64(kv)
    lane = lax.broadcasted_iota(jnp.int32, (n, LANES), 1)
    kk_ref[...] = _bf(jnp.where(lane < 64, kv, vk))
    vv = jnp.where(lane < 64, vk, jnp.where(lane == 64, 1.0, 0.0))
    vv_ref[...] = _bf(vv.T[0:V_ROWS])


def _compress(l, a, pe_lo, pe_hi, w_lo, w_hi, w2):
    bsz, n, width = a.shape
    per_b = lambda c: pl.BlockSpec((None, n, c), lambda b: (b, 0, 0))
    return pl.pallas_call(
        _compress_kernel,
        grid=(bsz,),
        in_specs=[per_b(width)] + [_layer_spec(w, l) for w in (pe_lo, pe_hi, w_lo, w_hi, w2)],
        out_specs=[per_b(LANES), pl.BlockSpec((None, V_ROWS, n), lambda b: (b, 0, 0))],
        out_shape=[jax.ShapeDtypeStruct((bsz, n, LANES), MXU_DTYPE),
                   jax.ShapeDtypeStruct((bsz, V_ROWS, n), MXU_DTYPE)],
        compiler_params=_cparams(1),
        name="compress",
    )(a, pe_lo, pe_hi, w_lo, w_hi, w2)


def _nsa_kernel(qc_ref, qr_ref, gate_ref, kk_ref, vv_ref, mt_ref, mask_ref, ks_ref, vs_ref, kw_ref, vw_ref,
                out_ref, m_ref, acc_ref, s0_ref, s1_ref, ocmp_ref, imp_ref, *, tq, tk, seq):
    qi = pl.program_id(1)
    t0 = qi * tq
    rows = NSA_HEADS * tq
    n_cmp_pad = kk_ref.shape[0]
    top_n = min(SEL_TOPK, seq // SEL_LEN)
    n_sel = LANES
    win_keys = min(WINDOW + tq, seq)

    lane =lax.broadcasted_iota(jnp.int32, (tq, LANES), 1)

    def head_rows(q_ref):
        parts = []
        for hd in range(NSA_HEADS):
            pair = q_ref[:, (hd // 2) * LANES:(hd // 2 + 1) * LANES]
            keep = (lane < 64) if hd % 2 == 0 else (lane >= 64)
            parts.append(jnp.where(keep, pair, jnp.zeros_like(pair)))
        return jnp.concatenate(parts, axis=0)

    q_c = head_rows(qc_ref)
    q_r = head_rows(qr_ref)
    pos_l = t0 + (lax.broadcasted_iota(jnp.int32, (1, rows), 1) & (tq - 1))

    last_blk = (pos_l - (CMP_LEN - 1)) >> 4

    def compressed(nr):
        s = lax.dot_general(kk_ref[0:nr, :], q_c, _NT, preferred_element_type=jnp.float32)
        n_idx = lax.broadcasted_iota(jnp.int32, (nr, rows), 0)
        sm = jnp.where(n_idx <= last_blk, s, NEG_INF)
        mx = jnp.max(sm, axis=0, keepdims=True)
        e = jnp.exp2(sm - mx)
        den = jnp.sum(e, axis=0, keepdims=True)
        p_cmp = e * jnp.where(last_blk >= 0, 1.0 / den, 0.0)
        ocmp_ref[...] = jnp.dot(vv_ref[:, 0:nr], _bf(p_cmp), preferred_element_type=jnp.float32)
        p_sum = p_cmp[:, 0:tq] + p_cmp[:, tq:2 * tq] + p_cmp[:, 2 * tq:3 * tq] + p_cmp[:, 3 * tq:4 * tq]
        hi = _bf(p_sum)
        r1 = p_sum - hi.astype(jnp.float32)
        mid = _bf(r1)
        lo = _bf(r1 - mid.astype(jnp.float32))
        mt = mt_ref[:, 0:nr]
        imp_ref[...] = (jnp.dot(mt, hi, preferred_element_type=jnp.float32)
                        + jnp.dot(mt, mid, preferred_element_type=jnp.float32)
                        + jnp.dot(mt, lo, preferred_element_type=jnp.float32))

    n_var = n_cmp_pad // LANES
    tile_last = jnp.maximum((t0 + tq - CMP_LEN) >> 4, 0)
    need = jnp.minimum(tile_last // LANES, n_var - 1)
    for v in range(n_var):
        pl.when(need == v)(functools.partial(compressed, (v + 1) * LANES))
    o_cmp = ocmp_ref[...]
    imp = imp_ref[...]

    start = pl.multiple_of(jnp.maximum(t0 + tq - win_keys, 0), tq)
    sw = lax.dot_general(kw_ref[pl.ds(start, win_keys), :], q_r, _NT, preferred_element_type=jnp.float32)
    w_case = jnp.minimum(t0 // tq, 2)
    m_idx = (jnp.where(w_case == 0, 1, jnp.where(w_case == 1, 0, 3)),
             jnp.where(w_case == 0, 2, jnp.where(w_case == 1, 1, 0)),
             jnp.where(w_case == 2, 1, 2))
    sw = jnp.concatenate([sw[j * tq:(j + 1) * tq] + mask_ref[m_idx[j]] for j in range(3)], axis=0)
    mw = jnp.max(sw, axis=0, keepdims=True)
    pw = _bf(jnp.exp2(sw - mw))
    wt0 = start // tq
    ow = jnp.dot(vw_ref[wt0], pw[0:tq], preferred_element_type=jnp.float32)
    for j in range(1, win_keys // tq):
        ow = ow + jnp.dot(vw_ref[wt0 + j], pw[j * tq:(j + 1) * tq], preferred_element_type=jnp.float32)
    o_win = ow / ow[64:65, :]

    blk = lax.broadcasted_iota(jnp.int32, (n_sel, tq), 0)
    blk_f = blk.astype(jnp.float32)
    pos_t = t0 + lax.broadcasted_iota(jnp.int32, (n_sel, tq), 1)
    cur = pos_t >> 6
    forced = (blk == 0) | (blk == cur) | (blk == cur - 1)
    valid = (blk * SEL_LEN) <= pos_t
    val = jnp.where(valid & jnp.logical_not(forced), imp, NEG_INF)
    for _ in range(top_n - 3):
        best = jnp.max(val, axis=0, keepdims=True)
        first = jnp.min(jnp.where(val == best, blk_f, float(n_sel)), axis=0, keepdims=True)
        val = jnp.where(blk_f == first, REMOVED, val)
    bias_t = jnp.where((forced | (val == REMOVED)) & valid, 0.0, NEG_INF)
    bias = _bf(bias_t.T)

    l_sel = jnp.concatenate(
        [jnp.concatenate([bias, q_r[hd * tq:(hd + 1) * tq]], axis=1) for hd in range(NSA_HEADS)], axis=0)
    m_ref[...] = jnp.full((1, rows), NEG_INF, jnp.float32)
    acc_ref[...] = jnp.zeros((V_ROWS, rows), jnp.float32)

    def scores_to(s_ref, kt, causal):
        k0 = pl.multiple_of(kt * tk, tk)
        sc = lax.dot_general(ks_ref[pl.ds(k0, tk), :], l_sel, _NT, preferred_element_type=jnp.float32)
        if causal:
            own = (t0 - k0) // tq
            s_ref[0:tq, :] = sc[0:tq] + mask_ref[1 - own]
            s_ref[tq:2 * tq, :] = sc[tq:2 * tq] + mask_ref[2 - own]
        else:
            s_ref[...] = sc

    def softmax_from(s_ref, kt):
        sc = s_ref[...]
        m_old = m_ref[...]
        m_new = jnp.maximum(m_old, jnp.max(sc, axis=0, keepdims=True))
        alpha = jnp.exp2(m_old - m_new)
        p = jnp.exp2(sc - m_new)
        acc_ref[...] = alpha * acc_ref[...] + jnp.dot(vs_ref[kt], _bf(p), preferred_element_type=jnp.float32)
        m_ref[...] = m_new

    n_full = t0 // tk
    n_pairs = n_full // 2
    scores_to(s0_ref, n_full, True)

    def pair(j, carry):
        k = 2 * j
        scores_to(s1_ref, k, False)
        softmax_from(s0_ref, jnp.where(j == 0, n_full, k - 1))
        scores_to(s0_ref, k + 1, False)
        softmax_from(s1_ref, k)
        return carry

    lax.fori_loop(0, n_pairs, pair, 0)
    pending = jnp.where(n_pairs == 0, n_full, 2 * n_pairs - 1)

    @pl.when(n_full % 2 == 1)
    def _():
        scores_to(s1_ref, n_full - 1, False)
        softmax_from(s0_ref, pending)
        softmax_from(s1_ref, n_full - 1)

    @pl.when(n_full % 2 == 0)
    def _():
        softmax_from(s0_ref, pending)

    acc = acc_ref[...]
    o_slc = acc / acc[64:65, :]

    gate_t = gate_ref[...].T
    y_t = []
    for hd in range(NSA_HEADS):
        c = slice(hd * tq, (hd + 1) * tq)
        y_t.append(gate_t[3 * hd:3 * hd + 1] * o_cmp[0:64, c]
                   + gate_t[3 * hd + 1:3 * hd + 2] * o_slc[0:64, c]
                   + gate_t[3 * hd + 2:3 * hd + 3] * o_win[0:64, c])
    out_ref[:, 0:LANES] = _bf(jnp.concatenate(y_t[0:2], axis=0).T)
    out_ref[:, LANES:2 * LANES] = _bf(jnp.concatenate(y_t[2:4], axis=0).T)


def _nsa(qc, qr, gate, kk, vv, mt, ks, vs, kw, vw):
    bsz, seq, _ = qc.shape
    tq = min(TQ, seq)
    tk = min(TK, seq)
    n_cmp_pad = kk.shape[1]
    rows = NSA_HEADS * tq
    assert tk == 2 * tq and min(WINDOW + tq, seq) == 3 * tq, "mask patterns assume 2 / 3 query-tile-sized key blocks"
    r_idx = lax.broadcasted_iota(jnp.int32, (tq, rows), 0)
    tau = lax.broadcasted_iota(jnp.int32, (tq, rows), 1) & (tq - 1)
    zero = jnp.zeros((tq, rows), jnp.float32)
    masks = jnp.stack([zero, jnp.where(r_idx <= tau, 0.0, NEG_INF), zero + NEG_INF,
                       jnp.where(r_idx > tau, 0.0, NEG_INF)])
    tile = lambda c: pl.BlockSpec((None, tq, c), lambda b, i: (b, i, 0))
    per_b = lambda r, c: pl.BlockSpec((None, r, c), lambda b, i: (b, 0, 0))
    per_b4 = lambda a: pl.BlockSpec((None,) + a.shape[1:], lambda b, i: (b, 0, 0, 0))
    const = lambda a: pl.BlockSpec(a.shape, lambda b, i: (0,) * a.ndim, pipeline_mode=pl.Buffered(1))
    return pl.pallas_call(
        functools.partial(_nsa_kernel, tq=tq, tk=tk, seq=seq),
        grid=(bsz, seq // tq),
        in_specs=[tile(256), tile(256), tile(LANES),
                  per_b(n_cmp_pad, LANES), per_b(V_ROWS, n_cmp_pad), const(mt), const(masks),
                  per_b(seq, 256), per_b4(vs), per_b(seq, LANES), per_b4(vw)],
        out_specs=tile(256),
        out_shape=jax.ShapeDtypeStruct((bsz, seq, 256), MXU_DTYPE),
        scratch_shapes=[pltpu.VMEM((1, rows), jnp.float32), pltpu.VMEM((V_ROWS, rows), jnp.float32),
                        pltpu.VMEM((tk, rows), jnp.float32), pltpu.VMEM((tk, rows), jnp.float32),
                        pltpu.VMEM((V_ROWS, rows), jnp.float32), pltpu.VMEM((LANES, tq), jnp.float32)],
        compiler_params=_cparams(2),
        name="nsa",
    )(qc, qr, gate, kk, vv, mt, masks, ks, vs, kw, vw)


def _mix_out_kernel(x_ref, za_ref, zah_ref, zc_ref, zd_ref, zdh_ref, ynsa_ref,
                    cw_ref, cb_ref, clg_ref, clb_ref, slg_ref, slb_ref, sw_ref, sb_ref,
                    pw_ref, ps_ref, wout_ref, gpost_ref, out_ref,
                    uext_ref, ushift_ref, ycat_ref, pext_ref, *, tm):
    i = pl.program_id(1)
    not_first = jnp.where(i > 0, 1.0, 0.0)

    zc = jax.nn.gelu(zc_ref[...], approximate=True)
    u_s = zc[:, :256]
    v_s = _layer_norm(zc[:, 256:], slg_ref[...], slb_ref[...])
    t_i = lax.broadcasted_iota(jnp.int32, (SGU_CHUNK, SGU_GROUPS * SGU_CHUNK), 0)
    s_i = lax.rem(lax.broadcasted_iota(jnp.int32, (SGU_CHUNK, SGU_GROUPS * SGU_CHUNK), 1), SGU_CHUNK)
    w_cat = _bf(jnp.where(s_i <= t_i, sw_ref[...], 0.0))
    grp = lax.broadcasted_iota(jnp.int32, (SGU_CHUNK, 256), 1) >> 6
    ysgu = []
    for c in range(tm // SGU_CHUNK):
        vc = v_s[c * SGU_CHUNK:(c + 1) * SGU_CHUNK]
        stacked = _bf(jnp.concatenate([jnp.where(grp == g, vc, 0.0) for g in range(SGU_GROUPS)], axis=0))
        f = jnp.dot(w_cat, stacked, preferred_element_type=jnp.float32) + sb_ref[...]
        ysgu.append(u_s[c * SGU_CHUNK:(c + 1) * SGU_CHUNK] * f)
    ycat_ref[:, 512:768] = _bf(jnp.concatenate(ysgu, axis=0))

    pext_ref[0:POOL_HALO, :] = zdh_ref[...] * not_first
    pext_ref[POOL_HALO:, :] = zd_ref[...]
    n_ext = tm + POOL_HALO
    sums = []
    for shift in (1, 2, 4, 8):
        n = n_ext - shift
        nxt = pext_ref[pl.ds(shift, n), :] + pext_ref[pl.ds(0, n), :]
        pext_ref[pl.ds(shift, n), :] = nxt
        sums.append(pext_ref[pl.ds(POOL_HALO, tm), :])
    lane_grp = lax.broadcasted_iota(jnp.int32, (tm, 256), 1) >> 6
    pos = i * tm + lax.broadcasted_iota(jnp.int32, (tm, 256), 0)
    wsum = jnp.where(lane_grp == 0, sums[0], jnp.where(lane_grp == 1, sums[1],
                     jnp.where(lane_grp == 2, sums[2], sums[3])))
    width = jnp.where(lane_grp == 0, POOL_WINDOWS[0], jnp.where(lane_grp == 1, POOL_WINDOWS[1],
                      jnp.where(lane_grp == 2, POOL_WINDOWS[2], POOL_WINDOWS[3])))
    cnt = jnp.minimum(pos + 1, width).astype(jnp.float32)
    pooled = wsum / cnt - zd_ref[...]
    ycat_ref[:, 768:1024] = _bf(jnp.dot(_bf(pooled), pw_ref[...], preferred_element_type=jnp.float32) * ps_ref[...])

    ycat_ref[:, 256:512] = ynsa_ref[...]
    y_rest = (jnp.dot(ycat_ref[:, 256:512], wout_ref[256:512, :], preferred_element_type=jnp.float32)
              + jnp.dot(ycat_ref[:, 512:768], wout_ref[512:768, :], preferred_element_type=jnp.float32)
              + jnp.dot(ycat_ref[:, 768:1024], wout_ref[768:1024, :], preferred_element_type=jnp.float32))

    def glu(z):
        return z[:, :256] * jax.nn.sigmoid(z[:, 256:])

    uext_ref[0:CONV_HALO, :] = glu(zah_ref[...]) * not_first
    uext_ref[CONV_HALO:, :] = glu(za_ref[...])
    first = CONV_HALO - (CONV_K - 1)
    n_shift = tm + CONV_HALO - SUBLANES
    for r in range(SUBLANES):
        n = n_shift if first + r + n_shift <= tm + CONV_HALO else n_shift - SUBLANES
        ushift_ref[r, 0:n, :] = uext_ref[pl.ds(first + r, n), :]
    rc = 64
    for r0 in range(0, tm, rc):
        acc = jnp.zeros((rc, 256), jnp.float32)
        for k in range(CONV_K):
            acc = acc + cw_ref[k:k + 1, :] * ushift_ref[k % SUBLANES, pl.ds(r0 + k - k % SUBLANES, rc), :]
        yc = _layer_norm(acc + cb_ref[...], clg_ref[...], clb_ref[...])
        ycat_ref[r0:r0 + rc, 0:256] = _bf(yc * jax.nn.sigmoid(yc))

    y = y_rest + jnp.dot(ycat_ref[:, 0:256], wout_ref[0:256, :], preferred_element_type=jnp.float32)
    out_ref[...] = x_ref[...] + _rms(y, gpost_ref[...])


def _mix_out(l, x, za, zc, zd, ynsa, cw, cb, clg, clb, slg, slb, sw_cat, sb_exp, pw_bd, ps, wout, gpost):
    bsz, seq, d = x.shape
    tm = min(TM_PROJ, seq)
    tok = lambda c: pl.BlockSpec((None, tm, c), lambda b, i: (b, i, 0))
    halo = lambda c, h: pl.BlockSpec((None, h, c), lambda b, i: (b, jnp.maximum(i * (tm // h) - 1, 0), 0))
    weights = [cw, cb, clg, clb, slg, slb, sw_cat, sb_exp, pw_bd, ps, wout, gpost]
    return pl.pallas_call(
        functools.partial(_mix_out_kernel, tm=tm),
        grid=(bsz, seq // tm),
        in_specs=[tok(d), tok(512), halo(512, CONV_HALO), tok(512), tok(256), halo(256, POOL_HALO), tok(256)]
                 + [_layer_spec(w, l) for w in weights],
        out_specs=tok(d),
        out_shape=jax.ShapeDtypeStruct((bsz, seq, d), jnp.float32),
        scratch_shapes=[pltpu.VMEM((tm + CONV_HALO, 256), jnp.float32),
                        pltpu.VMEM((SUBLANES, tm + CONV_HALO - SUBLANES, 256), jnp.float32),
                        pltpu.VMEM((tm, 4 * 256), MXU_DTYPE),
                        pltpu.VMEM((tm + POOL_HALO, 256), jnp.float32)],
        compiler_params=_cparams(2),
        name="mix_out",
    )(x, za, za, zc, zd, zd, ynsa, *weights)


def _ffn_kernel(x_ref, xh_ref, gpre_ref, wup_ref, cw_ref, cb_ref, wdn_ref, gpost_ref, out_ref,
                hext_ref, g0_ref, u0_ref, g1_ref, u1_ref, acc_ref, *, tm, n_chunks, chunk):
    i = pl.program_id(1)
    not_first = jnp.where(i > 0, 1.0, 0.0)
    x = x_ref[...]
    hext_ref[0:FFN_HALO, :] = _bf(_rms(xh_ref[...], gpre_ref[...]) * not_first)
    hext_ref[FFN_HALO:, :] = _bf(_rms(x, gpre_ref[...]))

    def cols(c):
        return pl.ds(pl.multiple_of(c * chunk, chunk), chunk)

    def up_to(g_ref, u_ref, c):
        h_ext = hext_ref[...]
        g_ref[...] = jnp.dot(h_ext, wup_ref[:, cols(c)], preferred_element_type=jnp.float32)
        u_ref[...] = jnp.dot(h_ext, wup_ref[:, cols(n_chunks + c)], preferred_element_type=jnp.float32)

    def conv(ext_ref, c):
        w = cw_ref[:, cols(c)]
        return (w[0:1] * ext_ref[pl.ds(FFN_HALO - 2, tm), :] + w[1:2] * ext_ref[pl.ds(FFN_HALO - 1, tm), :]
                + w[2:3] * ext_ref[pl.ds(FFN_HALO, tm), :] + cb_ref[:, cols(c)])

    def down_from(g_ref, u_ref, c):
        act = jax.nn.gelu(conv(g_ref, c), approximate=True) * conv(u_ref, n_chunks + c)
        acc_ref[...] += jnp.dot(_bf(act), wdn_ref[c], preferred_element_type=jnp.float32)

    assert n_chunks % 2 == 1
    acc_ref[...] = jnp.zeros_like(acc_ref)
    up_to(g0_ref, u0_ref, 0)

    def pair(j, carry):
        c = 2 * j
        up_to(g1_ref, u1_ref, c + 1)
        down_from(g0_ref, u0_ref, c)
        up_to(g0_ref, u0_ref, c + 2)
        down_from(g1_ref, u1_ref, c + 1)
        return carry

    lax.fori_loop(0, (n_chunks - 1) // 2, pair, 0)
    down_from(g0_ref, u0_ref, n_chunks - 1)
    out_ref[...] = x + _rms(acc_ref[...], gpost_ref[...])


def _ffn(l, x, gpre, wup, cw, cb, wdn, gpost):
    bsz, seq, d = x.shape
    tm = min(TM_FFN, seq)
    n_chunks = wdn.shape[1]
    chunk = wdn.shape[2]
    tok = pl.BlockSpec((None, tm, d), lambda b, i: (b, i, 0))
    halo = pl.BlockSpec((None, FFN_HALO, d), lambda b, i: (b, jnp.maximum(i * (tm // FFN_HALO) - 1, 0), 0))
    weights = [gpre, wup, cw, cb, wdn, gpost]
    return pl.pallas_call(
        functools.partial(_ffn_kernel, tm=tm, n_chunks=n_chunks, chunk=chunk),
        grid=(bsz, seq // tm),
        in_specs=[tok, halo] + [_layer_spec(w, l) for w in weights],
        out_specs=tok,
        out_shape=jax.ShapeDtypeStruct((bsz, seq, d), jnp.float32),
        scratch_shapes=[pltpu.VMEM((tm + FFN_HALO, d), MXU_DTYPE)]
                       + [pltpu.VMEM((tm + FFN_HALO, chunk), jnp.float32)] * 4
                       + [pltpu.VMEM((tm, d), jnp.float32)],
        compiler_params=_cparams(2),
        name="ffn",
    )(x, x, *weights)


def _swap_halves(w):
    k = w.shape[-1] // HEAD_DIM
    w3 = w.reshape(w.shape[:-1] + (k, HEAD_DIM))
    half = HEAD_DIM // 2
    return jnp.concatenate([-w3[..., half:], w3[..., :half]], axis=-1).reshape(w.shape)


def _prep_w_in(w):
    d_mix = 1024
    c_conv, c_nsa, c_sgu, c_pool = d_mix // 4, NSA_HEADS * HEAD_DIM, d_mix // 4, d_mix // 4
    sizes = (2 * c_conv, c_nsa, 6 * HEAD_DIM, 3 * NSA_HEADS, 2 * c_sgu, c_pool)
    offs = np.cumsum((0,) + sizes)
    wa, wq, wkv, wg, wc, wd = [w[..., offs[j]:offs[j + 1]] for j in range(6)]
    kv = [wkv[..., j * HEAD_DIM:(j + 1) * HEAD_DIM] for j in range(6)]
    wg_pad = jnp.pad(wg, ((0, 0), (0, 0), (0, LANES - wg.shape[-1])))
    cols = [wa, wc, wd, wq, _swap_halves(wq), kv[0], kv[1],
            kv[2], _swap_halves(kv[2]), kv[4], _swap_halves(kv[4]), kv[3], kv[5], wg_pad]
    return _bf(jnp.concatenate(cols, axis=-1))


def _prep_compress(pe_k, pe_v, ck_w1, ck_w2, cv_w1, cv_w2):
    half = CMP_LEN // 2

    def w1_part(w1k, w1v, lo):
        wk = w1k.reshape(CMP_LEN, HEAD_DIM, HEAD_DIM)[lo:lo + half]
        wv = w1v.reshape(CMP_LEN, HEAD_DIM, HEAD_DIM)[lo:lo + half]
        z = jnp.zeros_like(wk)
        top = jnp.concatenate([wk, z], axis=-1)
        bot = jnp.concatenate([z, wv], axis=-1)
        return jnp.concatenate([top, bot], axis=1).reshape(half * 2 * HEAD_DIM, 2 * HEAD_DIM)

    def pe_part(lo):
        return jnp.concatenate([pe_k[lo:lo + half], pe_v[lo:lo + half]], axis=1).reshape(1, half * 2 * HEAD_DIM)

    z = jnp.zeros_like(ck_w2)
    w2 = jnp.concatenate([jnp.concatenate([ck_w2, z], axis=1), jnp.concatenate([z, cv_w2], axis=1)], axis=0)
    return (pe_part(0), pe_part(half), _bf(w1_part(ck_w1, cv_w1, 0)), _bf(w1_part(ck_w1, cv_w1, half)), _bf(w2))


def _importance_matrix(seq):
    n_cmp = (seq - CMP_LEN) // CMP_STRIDE + 1
    n_sel = seq // SEL_LEN
    ratio = SEL_LEN // CMP_STRIDE
    n_ov = CMP_LEN // CMP_STRIDE
    n_pad = -(-n_cmp // LANES) * LANES
    assert n_sel <= LANES, "selection blocks must fit one lane group"
    assert min(SEL_TOPK, n_sel) > 3, "top-k must exceed the three forced blocks"
    mt = np.zeros((LANES, n_pad), np.float32)
    for j in range(n_sel):
        for m in range(ratio):
            for n in range(n_ov):
                c = ratio * j + m - n
                if 0 <= c < n_cmp:
                    mt[j, c] += 1.0
    return jnp.asarray(mt, MXU_DTYPE)


def _rope_table(seq):
    half = HEAD_DIM // 2
    inv = ROPE_THETA ** (-jnp.arange(half, dtype=jnp.float32) * 2.0 / HEAD_DIM)
    ang = jnp.arange(seq, dtype=jnp.float32)[:, None] * inv[None, :]
    cos, sin = jnp.cos(ang), jnp.sin(ang)
    return jnp.concatenate([cos, cos, sin, sin], axis=1)


def _block_diag(w):
    g, c, d = w.shape
    out = jnp.zeros((g * c, g * d), w.dtype)
    for j in range(g):
        out = out.at[j * c:(j + 1) * c, j * d:(j + 1) * d].set(w[j])
    return out


def kernel(x, norm_mix_pre, norm_mix_post, norm_ffn_pre, norm_ffn_post, w_in, w_out, conv_dw_w, conv_dw_b, conv_ln_g, conv_ln_b, nsa_pe_k, nsa_pe_v, nsa_ck_w1, nsa_ck_w2, nsa_cv_w1, nsa_cv_w2, sgu_ln_g, sgu_ln_b, sgu_w, sgu_b, pool_w, pool_scale, ffn_up, ffn_conv_w, ffn_conv_b, ffn_down):
    bsz, seq, d = x.shape
    depth = w_in.shape[0]
    ffn_dim = ffn_down.shape[1]
    cs = _rope_table(seq)
    mt = _importance_matrix(seq)
    rows = lambda v: v[:, None, :]

    w_ext = _prep_w_in(w_in)
    cmp_w = jax.vmap(_prep_compress)(nsa_pe_k, nsa_pe_v, nsa_ck_w1, nsa_ck_w2, nsa_cv_w1, nsa_cv_w2)
    sw_cat = jnp.transpose(sgu_w, (0, 2, 1, 3)).reshape(depth, SGU_CHUNK, SGU_GROUPS * SGU_CHUNK)
    sb_exp = jnp.repeat(jnp.swapaxes(sgu_b, 1, 2), 256 // SGU_GROUPS, axis=2)
    mix_w = [conv_dw_w, rows(conv_dw_b), rows(conv_ln_g), rows(conv_ln_b), rows(sgu_ln_g), rows(sgu_ln_b),
             sw_cat, sb_exp, _bf(jax.vmap(_block_diag)(pool_w)), rows(pool_scale), _bf(w_out), rows(norm_mix_post)]
    ffn_w = [rows(norm_ffn_pre), _bf(ffn_up), ffn_conv_w, rows(ffn_conv_b),
             _bf(ffn_down).reshape(depth, ffn_dim // FFN_CHUNK, FFN_CHUNK, d), rows(norm_ffn_post)]
    g_pre = rows(norm_mix_pre)

    for l in range(depth):
        za, zc, zd, qc, qr, a_cmp, ks, vs, kw, vw, gate = _in_proj(l, x, g_pre, w_ext, cs)
        kk, vv = _compress(l, a_cmp, *cmp_w)
        y_nsa = _nsa(qc, qr, gate, kk, vv, mt, ks, vs, kw, vw)
        x = _mix_out(l, x, za, zc, zd, y_nsa, *mix_w)
        x = _ffn(l, x, *ffn_w)
    return x
```

```python
import functools

import numpy as np
import jax
import jax.numpy as jnp
from jax import lax
from jax.experimental import pallas as pl
from jax.experimental.pallas import tpu as pltpu

HEAD_DIM = 64
NSA_HEADS = 4
CONV_K = 31
CMP_LEN = 32
CMP_STRIDE = 16
SEL_LEN = 64
SEL_TOPK = 16
WINDOW = 512
SGU_CHUNK = 128
SGU_GROUPS = 4
POOL_WINDOWS = (2, 4, 8, 16)
FFN_CONV_K = 3
ROPE_THETA = 10000.0
RMS_EPS = 1e-6
LN_EPS = 1e-5
NEG_INF = -1e30
FORCE_SCORE = 1e6
REMOVED = -3e38
LOG2E = 1.4426950408889634

LANES = 128
SUBLANES = 8
V_ROWS = 80
MXU_DTYPE = jnp.bfloat16
VMEM_LIMIT = 56 * 1024 * 1024

TM_PROJ = 512
TM_FFN = 1024
TQ = 256
TK = 512
FFN_CHUNK = 256
CONV_HALO = 32
POOL_HALO = 16
FFN_HALO = 16

_NT = (((1,), (1,)), ((), ()))


def _cparams(n_axes, flags=None):
    return pltpu.CompilerParams(dimension_semantics=("arbitrary",) * n_axes,
                                vmem_limit_bytes=VMEM_LIMIT, flags=flags)


def _layer_spec(a, l):
    return pl.BlockSpec((None,) + a.shape[1:], lambda *_: (l,) + (0,) * (a.ndim - 1),
                        pipeline_mode=pl.Buffered(1))


def _rms(x, g):
    return x * lax.rsqrt(jnp.mean(x * x, axis=-1, keepdims=True) + RMS_EPS) * g


def _layer_norm(x, g, b):
    mu = jnp.mean(x, axis=-1, keepdims=True)
    var = jnp.mean(jnp.square(x - mu), axis=-1, keepdims=True)
    return (x - mu) * lax.rsqrt(var + LN_EPS) * g + b


def _bf(x):
    return x.astype(MXU_DTYPE)


def _swap64(x):
    return pltpu.roll(x, 64, axis=1)


def _in_proj_kernel(x_ref, g_ref, w_ref, cs_ref,
                    za_ref, zc_ref, zd_ref, qc_ref, qr_ref, kv_ref,
                    ks_ref, vs_ref, kw_ref, vw_ref, gate_ref, kvt_ref, *, tm, tk):
    i = pl.program_id(1)
    h = _bf(_rms(x_ref[...], g_ref[...]))

    def proj(lo, hi):
        return jnp.dot(h, w_ref[:, lo:hi], preferred_element_type=jnp.float32)

    q = proj(1280, 1536)
    kvk = proj(1536, 1792)
    vg = proj(1792, 2048)
    kvt_ref[...] = kvk[:, :LANES]
    group = CMP_LEN // 2
    for j in range(group):
        kv_ref[:, j * LANES:(j + 1) * LANES] = kvt_ref[pl.ds(j, tm // group, stride=group), :]
    kx = kvk[:, LANES:]
    vsw = vg[:, :LANES]
    gate_ref[...] = jax.nn.sigmoid(vg[:, LANES:])

    cs = cs_ref[...]
    lane = lax.broadcasted_iota(jnp.int32, (tm, LANES), 1)
    sc = _swap64(cs)
    cos2 = jnp.where(lane < 64, cs, sc)
    sin2 = jnp.where(lane < 64, sc, cs)
    first_half = (lane & (HEAD_DIM - 1)) < HEAD_DIM // 2

    def rope(x):
        swapped = jnp.where(first_half, -pltpu.roll(x, LANES - HEAD_DIM // 2, axis=1),
                            pltpu.roll(x, HEAD_DIM // 2, axis=1))
        return x * cos2 + swapped * sin2

    scale = HEAD_DIM ** -0.5 * LOG2E
    qc_ref[...] = _bf(q * scale)
    for pair in range(NSA_HEADS // 2):
        c = slice(pair * LANES, (pair + 1) * LANES)
        qr_ref[:, c] = _bf(rope(q[:, c]) * scale)

    kx_rot = rope(kx)
    kx_swap = _swap64(kx_rot)
    k_rot2 = jnp.where(lane < 64, kx_rot, kx_swap)
    kw_rot2 = jnp.where(lane < 64, kx_swap, kx_rot)
    pos = i * tm + lax.broadcasted_iota(jnp.int32, (tm, LANES), 0)
    onehot =jnp.where((pos >> 6) == lane, 1.0, 0.0)
    ks_ref[:, 0:LANES] = _bf(onehot)
    ks_ref[:, LANES:2 * LANES] = _bf(k_rot2)
    kw_ref[...] = _bf(kw_rot2)
    ones_col = jnp.where(lane == 64, 1.0, 0.0)
    vs = jnp.where(lane < 64, vsw, ones_col)
    vw = jnp.where(lane < 64, _swap64(vsw), ones_col)
    for j in range(tm // tk):
        vs_ref[j] = _bf(vs[j * tk:(j + 1) * tk].T[0:V_ROWS])
    for j in range(tm // TQ):
        vw_ref[j] = _bf(vw[j * TQ:(j + 1) * TQ].T[0:V_ROWS])

    za_ref[...] = proj(0, 512)
    zc_ref[...] = proj(512, 1024)
    zd_ref[...] = proj(1024, 1280)


def _in_proj(l, x, g, w_ext, cs):
    bsz, seq, d = x.shape
    tm = min(TM_PROJ, seq)
    tk = min(TK, seq)
    tok = lambda c: pl.BlockSpec((None, tm, c), lambda b, i: (b, i, 0))
    f32, bf16 = jnp.float32, MXU_DTYPE

    def tok_out(c, dt):
        return tok(c), jax.ShapeDtypeStruct((bsz, seq, c), dt)

    def keytile_out(t):
        return (pl.BlockSpec((None, tm // t, V_ROWS, t), lambda b, i: (b, i, 0, 0)),
                jax.ShapeDtypeStruct((bsz, seq // t, V_ROWS, t), bf16))

    group = CMP_LEN // 2
    cmp_out = (pl.BlockSpec((None, tm // group, group * LANES), lambda b, i: (b, i, 0)),
               jax.ShapeDtypeStruct((bsz, seq // group, group * LANES), f32))
    outs = [tok_out(512, f32), tok_out(512, f32), tok_out(256, f32), tok_out(256, bf16), tok_out(256, bf16),
            cmp_out, tok_out(256, bf16), keytile_out(tk), tok_out(128, bf16), keytile_out(TQ),
            tok_out(128, f32)]
    return pl.pallas_call(
        functools.partial(_in_proj_kernel, tm=tm, tk=tk),
        grid=(bsz, seq // tm),
        in_specs=[tok(d), _layer_spec(g, l), _layer_spec(w_ext, l),
                  pl.BlockSpec((tm, LANES), lambda b, i: (i, 0))],
        out_specs=[spec for spec, _ in outs],
        out_shape=[shape for _, shape in outs],
        scratch_shapes=[pltpu.VMEM((tm, LANES), f32)],
        compiler_params=_cparams(2),
        name="in_proj",
    )(x, g, w_ext, cs)


def _compress_kernel(a_ref, pe_lo_ref, pe_hi_ref, w_lo_ref, w_hi_ref, w2_ref, kk_ref, vv_ref):
    a = a_ref[...]
    u = jnp.dot(_bf(a + pe_lo_ref[...]), w_lo_ref[...], preferred_element_type=jnp.float32)
    v = jnp.dot(_bf(a + pe_hi_ref[...]), w_hi_ref[...], preferred_element_type=jnp.float32)
    n = a.shape[0]
    pre = u + pltpu.roll(v, n - 1, axis=0)
    act = jax.nn.gelu(pre, approximate=True)
    kv = jnp.dot(_bf(act), w2_ref[...], preferred_element_type=jnp.float32)
    vk = _swap64(kv)
    lane = lax.broadcasted_iota(jnp.int32, (n, LANES), 1)
    kk_ref[...] = _bf(jnp.where(lane < 64, kv, vk))
    vv = jnp.where(lane < 64, vk, jnp.where(lane == 64, 1.0, 0.0))
    vv_ref[...] = _bf(vv.T[0:V_ROWS])


def _compress(l, a, pe_lo, pe_hi, w_lo, w_hi, w2):
    bsz, n, width = a.shape
    per_b = lambda c: pl.BlockSpec((None, n, c), lambda b: (b, 0, 0))
    return pl.pallas_call(
        _compress_kernel,
        grid=(bsz,),
        in_specs=[per_b(width)] + [_layer_spec(w, l) for w in (pe_lo, pe_hi, w_lo, w_hi, w2)],
        out_specs=[per_b(LANES), pl.BlockSpec((None, V_ROWS, n), lambda b: (b, 0, 0))],
        out_shape=[jax.ShapeDtypeStruct((bsz, n, LANES), MXU_DTYPE),
                   jax.ShapeDtypeStruct((bsz, V_ROWS, n), MXU_DTYPE)],
        compiler_params=_cparams(1),
        name="compress",
    )(a, pe_lo, pe_hi, w_lo, w_hi, w2)


def _nsa_kernel(qc_ref, qr_ref, gate_ref, kk_ref, vv_ref, mt_ref, mask_ref, ks_ref, vs_ref, kw_ref, vw_ref,
                out_ref, m_ref, acc_ref, s0_ref, s1_ref, mx0_ref, mx1_ref, ocmp_ref, imp_ref, *, tq, tk, seq):
    qi = pl.program_id(1)
    t0 = qi * tq
    rows = NSA_HEADS * tq
    n_cmp_pad = kk_ref.shape[0]
    top_n = min(SEL_TOPK, seq // SEL_LEN)
    n_sel = LANES
    win_keys = min(WINDOW + tq, seq)

    lane =lax.broadcasted_iota(jnp.int32, (tq, LANES), 1)

    def head_rows(q_ref):
        parts = []
        for hd in range(NSA_HEADS):
            pair = q_ref[:, (hd // 2) * LANES:(hd // 2 + 1) * LANES]
            keep = (lane < 64) if hd % 2 == 0 else (lane >= 64)
            parts.append(jnp.where(keep, pair, jnp.zeros_like(pair)))
        return jnp.concatenate(parts, axis=0)

    q_c = head_rows(qc_ref)
    q_r = head_rows(qr_ref)
    pos_l = t0 + (lax.broadcasted_iota(jnp.int32, (1, rows), 1) & (tq - 1))

    last_blk = (pos_l - (CMP_LEN - 1)) >> 4

    def compressed(nr):
        s = lax.dot_general(kk_ref[0:nr, :], q_c, _NT, preferred_element_type=jnp.float32)
        n_idx = lax.broadcasted_iota(jnp.int32, (nr, rows), 0)
        sm = jnp.where(n_idx <= last_blk, s, NEG_INF)
        mx = jnp.max(sm, axis=0, keepdims=True)
        e = jnp.exp2(sm - mx)
        den = jnp.sum(e, axis=0, keepdims=True)
        p_cmp = e * jnp.where(last_blk >= 0, 1.0 / den, 0.0)
        ocmp_ref[...] = jnp.dot(vv_ref[:, 0:nr], _bf(p_cmp), preferred_element_type=jnp.float32)
        p_sum = p_cmp[:, 0:tq] + p_cmp[:, tq:2 * tq] + p_cmp[:, 2 * tq:3 * tq] + p_cmp[:, 3 * tq:4 * tq]
        hi = _bf(p_sum)
        r1 = p_sum - hi.astype(jnp.float32)
        mid = _bf(r1)
        lo = _bf(r1 - mid.astype(jnp.float32))
        mt = mt_ref[:, 0:nr]
        imp_ref[...] = (jnp.dot(mt, hi, preferred_element_type=jnp.float32)
                        + jnp.dot(mt, mid, preferred_element_type=jnp.float32)
                        + jnp.dot(mt, lo, preferred_element_type=jnp.float32))

    n_var = n_cmp_pad // LANES
    tile_last = jnp.maximum((t0 + tq - CMP_LEN) >> 4, 0)
    need = jnp.minimum(tile_last // LANES, n_var - 1)
    for v in range(n_var):
        pl.when(need == v)(functools.partial(compressed, (v + 1) * LANES))
    o_cmp = ocmp_ref[...]
    imp = imp_ref[...]

    start = pl.multiple_of(jnp.maximum(t0 + tq - win_keys, 0), tq)
    sw = lax.dot_general(kw_ref[pl.ds(start, win_keys), :], q_r, _NT, preferred_element_type=jnp.float32)
    w_case = jnp.minimum(t0 // tq, 2)
    m_idx = (jnp.where(w_case == 0, 1, jnp.where(w_case == 1, 0, 3)),
             jnp.where(w_case == 0, 2, jnp.where(w_case == 1, 1, 0)),
             jnp.where(w_case == 2, 1, 2))
    sw = jnp.concatenate([sw[j * tq:(j + 1) * tq] + mask_ref[m_idx[j]] for j in range(3)], axis=0)
    mw = jnp.max(sw, axis=0, keepdims=True)
    pw = _bf(jnp.exp2(sw - mw))
    wt0 = start // tq
    ow = jnp.dot(vw_ref[wt0], pw[0:tq], preferred_element_type=jnp.float32)
    for j in range(1, win_keys // tq):
        ow = ow + jnp.dot(vw_ref[wt0 + j], pw[j * tq:(j + 1) * tq], preferred_element_type=jnp.float32)
    o_win = ow / ow[64:65, :]

    blk = lax.broadcasted_iota(jnp.int32, (n_sel, tq), 0)
    blk_f = blk.astype(jnp.float32)
    pos_t = t0 + lax.broadcasted_iota(jnp.int32, (n_sel, tq), 1)
    cur = pos_t >> 6
    forced = (blk == 0) | (blk == cur) | (blk == cur - 1)
    valid = (blk * SEL_LEN) <= pos_t
    val = jnp.where(valid & jnp.logical_not(forced), imp, NEG_INF)
    for _ in range(top_n - 3):
        best = jnp.max(val, axis=0, keepdims=True)
        first = jnp.min(jnp.where(val == best, blk_f, float(n_sel)), axis=0, keepdims=True)
        val = jnp.where(blk_f == first, REMOVED, val)
    bias_t = jnp.where((forced | (val == REMOVED)) & valid, 0.0, NEG_INF)
    bias = _bf(bias_t.T)

    l_sel = jnp.concatenate(
        [jnp.concatenate([bias, q_r[hd * tq:(hd + 1) * tq]], axis=1) for hd in range(NSA_HEADS)], axis=0)
    m_ref[...] = jnp.full((1, rows), NEG_INF, jnp.float32)
    acc_ref[...] = jnp.zeros((V_ROWS, rows), jnp.float32)

    def scores_to(buf, kt, causal):
        s_ref, mx_ref = buf
        k0 = pl.multiple_of(kt * tk, tk)
        sc = lax.dot_general(ks_ref[pl.ds(k0, tk), :], l_sel, _NT, preferred_element_type=jnp.float32)
        if causal:
            own = (t0 - k0) // tq
            lo = sc[0:tq] + mask_ref[1 - own]
            hi = sc[tq:2 * tq] + mask_ref[2 - own]
            s_ref[0:tq, :] = lo
            s_ref[tq:2 * tq, :] = hi
            mx_ref[...] = jnp.maximum(jnp.max(lo, axis=0, keepdims=True), jnp.max(hi, axis=0, keepdims=True))
        else:
            s_ref[...] = sc
            mx_ref[...] = jnp.max(sc, axis=0, keepdims=True)

    def softmax_from(buf, kt):
        s_ref, mx_ref = buf
        sc = s_ref[...]
        m_old = m_ref[...]
        m_new = jnp.maximum(m_old, mx_ref[...])
        alpha = jnp.exp2(m_old - m_new)
        p = jnp.exp2(sc - m_new)
        acc_ref[...] = alpha * acc_ref[...] + jnp.dot(vs_ref[kt], _bf(p), preferred_element_type=jnp.float32)
        m_ref[...] = m_new

    buf0, buf1 = (s0_ref, mx0_ref), (s1_ref, mx1_ref)
    n_full = t0 // tk
    n_pairs = n_full // 2
    scores_to(buf0, n_full, True)

    def pair(j, carry):
        k = 2 * j
        scores_to(buf1, k, False)
        softmax_from(buf0, jnp.where(j == 0, n_full, k - 1))
        scores_to(buf0, k + 1, False)
        softmax_from(buf1, k)
        return carry

    lax.fori_loop(0, n_pairs, pair, 0)
    pending = jnp.where(n_pairs == 0, n_full, 2 * n_pairs - 1)

    @pl.when(n_full % 2 == 1)
    def _():
        scores_to(buf1, n_full - 1, False)
        softmax_from(buf0, pending)
        softmax_from(buf1, n_full - 1)

    @pl.when(n_full % 2 == 0)
    def _():
        softmax_from(buf0, pending)

    acc = acc_ref[...]
    o_slc = acc / acc[64:65, :]

    gate_t = gate_ref[...].T
    y_t = []
    for hd in range(NSA_HEADS):
        c = slice(hd * tq, (hd + 1) * tq)
        y_t.append(gate_t[3 * hd:3 * hd + 1] * o_cmp[0:64, c]
                   + gate_t[3 * hd + 1:3 * hd + 2] * o_slc[0:64, c]
                   + gate_t[3 * hd + 2:3 * hd + 3] * o_win[0:64, c])
    out_ref[:, 0:LANES] = _bf(jnp.concatenate(y_t[0:2], axis=0).T)
    out_ref[:, LANES:2 * LANES] = _bf(jnp.concatenate(y_t[2:4], axis=0).T)


def _nsa(qc, qr, gate, kk, vv, mt, ks, vs, kw, vw):
    bsz, seq, _ = qc.shape
    tq = min(TQ, seq)
    tk = min(TK, seq)
    n_cmp_pad = kk.shape[1]
    rows = NSA_HEADS * tq
    assert tk == 2 * tq and min(WINDOW + tq, seq) == 3 * tq, "mask patterns assume 2 / 3 query-tile-sized key blocks"
    r_idx = lax.broadcasted_iota(jnp.int32, (tq, rows), 0)
    tau = lax.broadcasted_iota(jnp.int32, (tq, rows), 1) & (tq - 1)
    zero = jnp.zeros((tq, rows), jnp.float32)
    masks = jnp.stack([zero, jnp.where(r_idx <= tau, 0.0, NEG_INF), zero + NEG_INF,
                       jnp.where(r_idx > tau, 0.0, NEG_INF)])
    tile = lambda c: pl.BlockSpec((None, tq, c), lambda b, i: (b, i, 0))
    per_b = lambda r, c: pl.BlockSpec((None, r, c), lambda b, i: (b, 0, 0))
    per_b4 = lambda a: pl.BlockSpec((None,) + a.shape[1:], lambda b, i: (b, 0, 0, 0))
    const = lambda a: pl.BlockSpec(a.shape, lambda b, i: (0,) * a.ndim, pipeline_mode=pl.Buffered(1))
    return pl.pallas_call(
        functools.partial(_nsa_kernel, tq=tq, tk=tk, seq=seq),
        grid=(bsz, seq // tq),
        in_specs=[tile(256), tile(256), tile(LANES),
                  per_b(n_cmp_pad, LANES), per_b(V_ROWS, n_cmp_pad), const(mt), const(masks),
                  per_b(seq, 256), per_b4(vs), per_b(seq, LANES), per_b4(vw)],
        out_specs=tile(256),
        out_shape=jax.ShapeDtypeStruct((bsz, seq, 256), MXU_DTYPE),
        scratch_shapes=[pltpu.VMEM((1, rows), jnp.float32), pltpu.VMEM((V_ROWS, rows), jnp.float32),
                        pltpu.VMEM((tk, rows), jnp.float32), pltpu.VMEM((tk, rows), jnp.float32),
                        pltpu.VMEM((1, rows), jnp.float32), pltpu.VMEM((1, rows), jnp.float32),
                        pltpu.VMEM((V_ROWS, rows), jnp.float32), pltpu.VMEM((LANES, tq), jnp.float32)],
        compiler_params=_cparams(2),
        name="nsa",
    )(qc, qr, gate, kk, vv, mt, masks, ks, vs, kw, vw)


def _mix_out_kernel(x_ref, za_ref, zah_ref, zc_ref, zd_ref, zdh_ref, ynsa_ref,
                    cw_ref, cb_ref, clg_ref, clb_ref, slg_ref, slb_ref, sw_ref, sb_ref,
                    pw_ref, ps_ref, wout_ref, gpost_ref, out_ref,
                    uext_ref, ushift_ref, ycat_ref, pext_ref, *, tm):
    i = pl.program_id(1)
    not_first = jnp.where(i > 0, 1.0, 0.0)

    zc = jax.nn.gelu(zc_ref[...], approximate=True)
    u_s = zc[:, :256]
    v_s = _layer_norm(zc[:, 256:], slg_ref[...], slb_ref[...])
    t_i = lax.broadcasted_iota(jnp.int32, (SGU_CHUNK, SGU_GROUPS * SGU_CHUNK), 0)
    s_i = lax.rem(lax.broadcasted_iota(jnp.int32, (SGU_CHUNK, SGU_GROUPS * SGU_CHUNK), 1), SGU_CHUNK)
    w_cat = _bf(jnp.where(s_i <= t_i, sw_ref[...], 0.0))
    grp = lax.broadcasted_iota(jnp.int32, (SGU_CHUNK, 256), 1) >> 6
    ysgu = []
    for c in range(tm // SGU_CHUNK):
        vc = v_s[c * SGU_CHUNK:(c + 1) * SGU_CHUNK]
        stacked = _bf(jnp.concatenate([jnp.where(grp == g, vc, 0.0) for g in range(SGU_GROUPS)], axis=0))
        f = jnp.dot(w_cat, stacked, preferred_element_type=jnp.float32) + sb_ref[...]
        ysgu.append(u_s[c * SGU_CHUNK:(c + 1) * SGU_CHUNK] * f)
    ycat_ref[:, 512:768] = _bf(jnp.concatenate(ysgu, axis=0))

    pext_ref[0:POOL_HALO, :] = zdh_ref[...] * not_first
    pext_ref[POOL_HALO:, :] = zd_ref[...]
    n_ext = tm + POOL_HALO
    sums = []
    for shift in (1, 2, 4, 8):
        n = n_ext - shift
        nxt = pext_ref[pl.ds(shift, n), :] + pext_ref[pl.ds(0, n), :]
        pext_ref[pl.ds(shift, n), :] = nxt
        sums.append(pext_ref[pl.ds(POOL_HALO, tm), :])
    lane_grp = lax.broadcasted_iota(jnp.int32, (tm, 256), 1) >> 6
    pos = i * tm + lax.broadcasted_iota(jnp.int32, (tm, 256), 0)
    wsum = jnp.where(lane_grp == 0, sums[0], jnp.where(lane_grp == 1, sums[1],
                     jnp.where(lane_grp == 2, sums[2], sums[3])))
    width = jnp.where(lane_grp == 0, POOL_WINDOWS[0], jnp.where(lane_grp == 1, POOL_WINDOWS[1],
                      jnp.where(lane_grp == 2, POOL_WINDOWS[2], POOL_WINDOWS[3])))
    cnt = jnp.minimum(pos + 1, width).astype(jnp.float32)
    pooled = wsum / cnt - zd_ref[...]
    ycat_ref[:, 768:1024] = _bf(jnp.dot(_bf(pooled), pw_ref[...], preferred_element_type=jnp.float32) * ps_ref[...])

    ycat_ref[:, 256:512] = ynsa_ref[...]
    y_rest = (jnp.dot(ycat_ref[:, 256:512], wout_ref[256:512, :], preferred_element_type=jnp.float32)
              + jnp.dot(ycat_ref[:, 512:768], wout_ref[512:768, :], preferred_element_type=jnp.float32)
              + jnp.dot(ycat_ref[:, 768:1024], wout_ref[768:1024, :], preferred_element_type=jnp.float32))

    def glu(z):
        return z[:, :256] * jax.nn.sigmoid(z[:, 256:])

    uext_ref[0:CONV_HALO, :] = glu(zah_ref[...]) * not_first
    uext_ref[CONV_HALO:, :] = glu(za_ref[...])
    first = CONV_HALO - (CONV_K - 1)
    n_shift = tm + CONV_HALO - SUBLANES
    for r in range(SUBLANES):
        n = n_shift if first + r + n_shift <= tm + CONV_HALO else n_shift - SUBLANES
        ushift_ref[r, 0:n, :] = uext_ref[pl.ds(first + r, n), :]
    rc = 64
    for r0 in range(0, tm, rc):
        acc = jnp.zeros((rc, 256), jnp.float32)
        for k in range(CONV_K):
            acc = acc + cw_ref[k:k + 1, :] * ushift_ref[k % SUBLANES, pl.ds(r0 + k - k % SUBLANES, rc), :]
        yc = _layer_norm(acc + cb_ref[...], clg_ref[...], clb_ref[...])
        ycat_ref[r0:r0 + rc, 0:256] = _bf(yc * jax.nn.sigmoid(yc))

    y = y_rest + jnp.dot(ycat_ref[:, 0:256], wout_ref[0:256, :], preferred_element_type=jnp.float32)
    out_ref[...] = x_ref[...] + _rms(y, gpost_ref[...])


def _mix_out(l, x, za, zc, zd, ynsa, cw, cb, clg, clb, slg, slb, sw_cat, sb_exp, pw_bd, ps, wout, gpost):
    bsz, seq, d = x.shape
    tm = min(TM_PROJ, seq)
    tok = lambda c: pl.BlockSpec((None, tm, c), lambda b, i: (b, i, 0))
    halo = lambda c, h: pl.BlockSpec((None, h, c), lambda b, i: (b, jnp.maximum(i * (tm // h) - 1, 0), 0))
    weights = [cw, cb, clg, clb, slg, slb, sw_cat, sb_exp, pw_bd, ps, wout, gpost]
    return pl.pallas_call(
        functools.partial(_mix_out_kernel, tm=tm),
        grid=(bsz, seq // tm),
        in_specs=[tok(d), tok(512), halo(512, CONV_HALO), tok(512), tok(256), halo(256, POOL_HALO), tok(256)]
                 + [_layer_spec(w, l) for w in weights],
        out_specs=tok(d),
        out_shape=jax.ShapeDtypeStruct((bsz, seq, d), jnp.float32),
        scratch_shapes=[pltpu.VMEM((tm + CONV_HALO, 256), jnp.float32),
                        pltpu.VMEM((SUBLANES, tm + CONV_HALO - SUBLANES, 256), jnp.float32),
                        pltpu.VMEM((tm, 4 * 256), MXU_DTYPE),
                        pltpu.VMEM((tm + POOL_HALO, 256), jnp.float32)],
        compiler_params=_cparams(2),
        name="mix_out",
    )(x, za, za, zc, zd, zd, ynsa, *weights)


def _ffn_kernel(x_ref, xh_ref, gpre_ref, wup_ref, cw_ref, cb_ref, wdn_ref, gpost_ref, out_ref,
                hext_ref, g0_ref, u0_ref, g1_ref, u1_ref, acc_ref, *, tm, n_chunks, chunk):
    i = pl.program_id(1)
    not_first = jnp.where(i > 0, 1.0, 0.0)
    x = x_ref[...]
    hext_ref[0:FFN_HALO, :] = _bf(_rms(xh_ref[...], gpre_ref[...]) * not_first)
    hext_ref[FFN_HALO:, :] = _bf(_rms(x, gpre_ref[...]))

    def cols(c):
        return pl.ds(pl.multiple_of(c * chunk, chunk), chunk)

    def up_to(g_ref, u_ref, c):
        h_ext = hext_ref[...]
        g_ref[...] = jnp.dot(h_ext, wup_ref[:, cols(c)], preferred_element_type=jnp.float32)
        u_ref[...] = jnp.dot(h_ext, wup_ref[:, cols(n_chunks + c)], preferred_element_type=jnp.float32)

    def conv(ext_ref, c):
        w = cw_ref[:, cols(c)]
        return (w[0:1] * ext_ref[pl.ds(FFN_HALO - 2, tm), :] + w[1:2] * ext_ref[pl.ds(FFN_HALO - 1, tm), :]
                + w[2:3] * ext_ref[pl.ds(FFN_HALO, tm), :] + cb_ref[:, cols(c)])

    def down_from(g_ref, u_ref, c):
        act = jax.nn.gelu(conv(g_ref, c), approximate=True) * conv(u_ref, n_chunks + c)
        acc_ref[...] += jnp.dot(_bf(act), wdn_ref[c], preferred_element_type=jnp.float32)

    assert n_chunks % 2 == 1
    acc_ref[...] = jnp.zeros_like(acc_ref)
    up_to(g0_ref, u0_ref, 0)

    def pair(j, carry):
        c = 2 * j
        up_to(g1_ref, u1_ref, c + 1)
        down_from(g0_ref, u0_ref, c)
        up_to(g0_ref, u0_ref, c + 2)
        down_from(g1_ref, u1_ref, c + 1)
        return carry

    lax.fori_loop(0, (n_chunks - 1) // 2, pair, 0)
    down_from(g0_ref, u0_ref, n_chunks - 1)
    out_ref[...] = x + _rms(acc_ref[...], gpost_ref[...])


def _ffn(l, x, gpre, wup, cw, cb, wdn, gpost):
    bsz, seq, d = x.shape
    tm = min(TM_FFN, seq)
    n_chunks = wdn.shape[1]
    chunk = wdn.shape[2]
    tok = pl.BlockSpec((None, tm, d), lambda b, i: (b, i, 0))
    halo = pl.BlockSpec((None, FFN_HALO, d), lambda b, i: (b, jnp.maximum(i * (tm // FFN_HALO) - 1, 0), 0))
    weights = [gpre, wup, cw, cb, wdn, gpost]
    return pl.pallas_call(
        functools.partial(_ffn_kernel, tm=tm, n_chunks=n_chunks, chunk=chunk),
        grid=(bsz, seq // tm),
        in_specs=[tok, halo] + [_layer_spec(w, l) for w in weights],
        out_specs=tok,
        out_shape=jax.ShapeDtypeStruct((bsz, seq, d), jnp.float32),
        scratch_shapes=[pltpu.VMEM((tm + FFN_HALO, d), MXU_DTYPE)]
                       + [pltpu.VMEM((tm + FFN_HALO, chunk), jnp.float32)] * 4
                       + [pltpu.VMEM((tm, d), jnp.float32)],
        compiler_params=_cparams(2),
        name="ffn",
    )(x, x, *weights)


def _prep_w_in(w):
    d_mix = 1024
    c_conv, c_nsa, c_sgu, c_pool = d_mix // 4, NSA_HEADS * HEAD_DIM, d_mix // 4, d_mix // 4
    sizes = (2 * c_conv, c_nsa, 6 * HEAD_DIM, 3 * NSA_HEADS, 2 * c_sgu, c_pool)
    offs = np.cumsum((0,) + sizes)
    wa, wq, wkv, wg, wc, wd = [w[..., offs[j]:offs[j + 1]] for j in range(6)]
    kv = [wkv[..., j * HEAD_DIM:(j + 1) * HEAD_DIM] for j in range(6)]
    wg_pad = jnp.pad(wg, ((0, 0), (0, 0), (0, LANES - wg.shape[-1])))
    cols = [wa, wc, wd, wq, kv[0], kv[1], kv[2], kv[4], kv[3], kv[5], wg_pad]
    return _bf(jnp.concatenate(cols, axis=-1))


def _prep_compress(pe_k, pe_v, ck_w1, ck_w2, cv_w1, cv_w2):
    half = CMP_LEN // 2

    def w1_part(w1k, w1v, lo):
        wk = w1k.reshape(CMP_LEN, HEAD_DIM, HEAD_DIM)[lo:lo + half]
        wv = w1v.reshape(CMP_LEN, HEAD_DIM, HEAD_DIM)[lo:lo + half]
        z = jnp.zeros_like(wk)
        top = jnp.concatenate([wk, z], axis=-1)
        bot = jnp.concatenate([z, wv], axis=-1)
        return jnp.concatenate([top, bot], axis=1).reshape(half * 2 * HEAD_DIM, 2 * HEAD_DIM)

    def pe_part(lo):
        return jnp.concatenate([pe_k[lo:lo + half], pe_v[lo:lo + half]], axis=1).reshape(1, half * 2 * HEAD_DIM)

    z = jnp.zeros_like(ck_w2)
    w2 = jnp.concatenate([jnp.concatenate([ck_w2, z], axis=1), jnp.concatenate([z, cv_w2], axis=1)], axis=0)
    return (pe_part(0), pe_part(half), _bf(w1_part(ck_w1, cv_w1, 0)), _bf(w1_part(ck_w1, cv_w1, half)), _bf(w2))


def _importance_matrix(seq):
    n_cmp = (seq - CMP_LEN) // CMP_STRIDE + 1
    n_sel = seq // SEL_LEN
    ratio = SEL_LEN // CMP_STRIDE
    n_ov = CMP_LEN // CMP_STRIDE
    n_pad = -(-n_cmp // LANES) * LANES
    assert n_sel <= LANES, "selection blocks must fit one lane group"
    assert min(SEL_TOPK, n_sel) > 3, "top-k must exceed the three forced blocks"
    mt = np.zeros((LANES, n_pad), np.float32)
    for j in range(n_sel):
        for m in range(ratio):
            for n in range(n_ov):
                c = ratio * j + m - n
                if 0 <= c < n_cmp:
                    mt[j, c] += 1.0
    return jnp.asarray(mt, MXU_DTYPE)


def _rope_table(seq):
    half = HEAD_DIM // 2
    inv = ROPE_THETA ** (-jnp.arange(half, dtype=jnp.float32) * 2.0 / HEAD_DIM)
    ang = jnp.arange(seq, dtype=jnp.float32)[:, None] * inv[None, :]
    cos, sin = jnp.cos(ang), jnp.sin(ang)
    return jnp.concatenate([cos, cos, sin, sin], axis=1)


def _block_diag(w):
    g, c, d = w.shape
    out = jnp.zeros((g * c, g * d), w.dtype)
    for j in range(g):
        out = out.at[j * c:(j + 1) * c, j * d:(j + 1) * d].set(w[j])
    return out


def kernel(x, norm_mix_pre, norm_mix_post, norm_ffn_pre, norm_ffn_post, w_in, w_out, conv_dw_w, conv_dw_b, conv_ln_g, conv_ln_b, nsa_pe_k, nsa_pe_v, nsa_ck_w1, nsa_ck_w2, nsa_cv_w1, nsa_cv_w2, sgu_ln_g, sgu_ln_b, sgu_w, sgu_b, pool_w, pool_scale, ffn_up, ffn_conv_w, ffn_conv_b, ffn_down):
    bsz, seq, d = x.shape
    depth = w_in.shape[0]
    ffn_dim = ffn_down.shape[1]
    cs = _rope_table(seq)
    mt = _importance_matrix(seq)
    rows = lambda v: v[:, None, :]

    w_ext = _prep_w_in(w_in)
    cmp_w = jax.vmap(_prep_compress)(nsa_pe_k, nsa_pe_v, nsa_ck_w1, nsa_ck_w2, nsa_cv_w1, nsa_cv_w2)
    sw_cat = jnp.transpose(sgu_w, (0, 2, 1, 3)).reshape(depth, SGU_CHUNK, SGU_GROUPS * SGU_CHUNK)
    sb_exp = jnp.repeat(jnp.swapaxes(sgu_b, 1, 2), 256 // SGU_GROUPS, axis=2)
    mix_w = [conv_dw_w, rows(conv_dw_b), rows(conv_ln_g), rows(conv_ln_b), rows(sgu_ln_g), rows(sgu_ln_b),
             sw_cat, sb_exp, _bf(jax.vmap(_block_diag)(pool_w)), rows(pool_scale), _bf(w_out), rows(norm_mix_post)]
    ffn_w = [rows(norm_ffn_pre), _bf(ffn_up), ffn_conv_w, rows(ffn_conv_b),
             _bf(ffn_down).reshape(depth, ffn_dim // FFN_CHUNK, FFN_CHUNK, d), rows(norm_ffn_post)]
    g_pre = rows(norm_mix_pre)

    for l in range(depth):
        za, zc, zd, qc, qr, a_cmp, ks, vs, kw, vw, gate = _in_proj(l, x, g_pre, w_ext, cs)
        kk, vv = _compress(l, a_cmp, *cmp_w)
        y_nsa = _nsa(qc, qr, gate, kk, vv, mt, ks, vs, kw, vw)
        x = _mix_out(l, x, za, zc, zd, y_nsa, *mix_w)
        x = _ffn(l, x, *ffn_w)
    return x
```

```python
import functools

import numpy as np
import jax
import jax.numpy as jnp
from jax import lax
from jax.experimental import pallas as pl
from jax.experimental.pallas import tpu as pltpu

HEAD_DIM = 64
NSA_HEADS = 4
CONV_K = 31
CMP_LEN = 32
CMP_STRIDE = 16
SEL_LEN = 64
SEL_TOPK = 16
WINDOW = 512
SGU_CHUNK = 128
SGU_GROUPS = 4
POOL_WINDOWS = (2, 4, 8, 16)
FFN_CONV_K = 3
ROPE_THETA = 10000.0
RMS_EPS = 1e-6
LN_EPS = 1e-5
NEG_INF = -1e30
FORCE_SCORE = 1e6
REMOVED = -3e38
LOG2E = 1.4426950408889634

LANES = 128
SUBLANES = 8
V_ROWS = 80
MXU_DTYPE = jnp.bfloat16
VMEM_LIMIT = 56 * 1024 * 1024

TM_PROJ = 512
TM_FFN = 1024
TQ = 256
TK = 512
FFN_CHUNK = 256
CONV_HALO = 32
POOL_HALO = 16
FFN_HALO = 16

_NT = (((1,), (1,)), ((), ()))


def _cparams(n_axes, flags=None):
    return pltpu.CompilerParams(dimension_semantics=("arbitrary",) * n_axes,
                                vmem_limit_bytes=VMEM_LIMIT, flags=flags)


def _layer_spec(a, l):
    return pl.BlockSpec((None,) + a.shape[1:], lambda *_: (l,) + (0,) * (a.ndim - 1),
                        pipeline_mode=pl.Buffered(1))


def _rms(x, g):
    return x * lax.rsqrt(jnp.mean(x * x, axis=-1, keepdims=True) + RMS_EPS) * g


def _layer_norm(x, g, b):
    mu = jnp.mean(x, axis=-1, keepdims=True)
    var = jnp.mean(jnp.square(x - mu), axis=-1, keepdims=True)
    return (x - mu) * lax.rsqrt(var + LN_EPS) * g + b


def _bf(x):
    return x.astype(MXU_DTYPE)


def _swap64(x):
    return pltpu.roll(x, 64, axis=1)


def _in_proj_kernel(x_ref, g_ref, w_ref, cs_ref,
                    za_ref, zc_ref, zd_ref, qc_ref, qr_ref, kv_ref,
                    ks_ref, vs_ref, kw_ref, vw_ref, gate_ref, kvt_ref, *, tm, tk):
    i = pl.program_id(1)
    h = _bf(_rms(x_ref[...], g_ref[...]))

    def proj(lo, hi):
        return jnp.dot(h, w_ref[:, lo:hi], preferred_element_type=jnp.float32)

    lane = lax.broadcasted_iota(jnp.int32, (tm, LANES), 1)
    q = proj(512, 768)
    kv01 = proj(768, 1024)
    kv2t = proj(1024, 1280)
    kvt_ref[...] = kv01[:, :LANES]
    group = CMP_LEN // 2
    for j in range(group):
        kv_ref[:, j * LANES:(j + 1) * LANES] = kvt_ref[pl.ds(j, tm // group, stride=group), :]
    kvs, kvw, tail0 = kv01[:, LANES:], kv2t[:, :LANES], kv2t[:, LANES:]
    kx = jnp.where(lane < 64, kvs, _swap64(kvw))
    vsw = jnp.where(lane < 64, _swap64(kvs), kvw)
    gate_ref[...] = jax.nn.sigmoid(tail0)

    cs = cs_ref[...]
    sc = _swap64(cs)
    cos2 = jnp.where(lane < 64, cs, sc)
    sin2 = jnp.where(lane < 64, sc, cs)
    first_half = (lane & (HEAD_DIM - 1)) < HEAD_DIM // 2

    def rope(x):
        swapped = jnp.where(first_half, -pltpu.roll(x, LANES - HEAD_DIM // 2, axis=1),
                            pltpu.roll(x, HEAD_DIM // 2, axis=1))
        return x * cos2 + swapped * sin2

    scale = HEAD_DIM ** -0.5 * LOG2E
    qc_ref[...] = _bf(q * scale)
    for pair in range(NSA_HEADS // 2):
        c = slice(pair * LANES, (pair + 1) * LANES)
        qr_ref[:, c] = _bf(rope(q[:, c]) * scale)

    kx_rot = rope(kx)
    kx_swap = _swap64(kx_rot)
    k_rot2 = jnp.where(lane < 64, kx_rot, kx_swap)
    kw_rot2 = jnp.where(lane < 64, kx_swap, kx_rot)
    pos = i * tm + lax.broadcasted_iota(jnp.int32, (tm, LANES), 0)
    onehot =jnp.where((pos >> 6) == lane, 1.0, 0.0)
    ks_ref[:, 0:LANES] = _bf(onehot)
    ks_ref[:, LANES:2 * LANES] = _bf(k_rot2)
    kw_ref[...] = _bf(kw_rot2)
    ones_col = jnp.where(lane == 64, 1.0, 0.0)
    vs = jnp.where(lane < 64, vsw, ones_col)
    vw = jnp.where(lane < 64, _swap64(vsw), ones_col)
    for j in range(tm // tk):
        vs_ref[j] = _bf(vs[j * tk:(j + 1) * tk].T[0:V_ROWS])
    for j in range(tm // TQ):
        vw_ref[j] = _bf(vw[j * TQ:(j + 1) * TQ].T[0:V_ROWS])

    tail = jnp.concatenate([tail0, proj(1280, 2048)], axis=1)
    n_gate = 3 * NSA_HEADS
    zc_ref[...] = tail[:, n_gate:n_gate + 512]
    zd_ref[...] = tail[:, n_gate + 512:n_gate + 768]
    za_ref[...] = proj(0, 512)


def _in_proj(l, x, g, w_ext, cs):
    bsz, seq, d = x.shape
    tm = min(TM_PROJ, seq)
    tk = min(TK, seq)
    tok = lambda c: pl.BlockSpec((None, tm, c), lambda b, i: (b, i, 0))
    f32, bf16 = jnp.float32, MXU_DTYPE

    def tok_out(c, dt):
        return tok(c), jax.ShapeDtypeStruct((bsz, seq, c), dt)

    def keytile_out(t):
        return (pl.BlockSpec((None, tm // t, V_ROWS, t), lambda b, i: (b, i, 0, 0)),
                jax.ShapeDtypeStruct((bsz, seq // t, V_ROWS, t), bf16))

    group = CMP_LEN // 2
    cmp_out = (pl.BlockSpec((None, tm // group, group * LANES), lambda b, i: (b, i, 0)),
               jax.ShapeDtypeStruct((bsz, seq // group, group * LANES), f32))
    outs = [tok_out(512, f32), tok_out(512, f32), tok_out(256, f32), tok_out(256, bf16), tok_out(256, bf16),
            cmp_out, tok_out(256, bf16), keytile_out(tk), tok_out(128, bf16), keytile_out(TQ),
            tok_out(128, f32)]
    return pl.pallas_call(
        functools.partial(_in_proj_kernel, tm=tm, tk=tk),
        grid=(bsz, seq // tm),
        in_specs=[tok(d), _layer_spec(g, l), _layer_spec(w_ext, l),
                  pl.BlockSpec((tm, LANES), lambda b, i: (i, 0))],
        out_specs=[spec for spec, _ in outs],
        out_shape=[shape for _, shape in outs],
        scratch_shapes=[pltpu.VMEM((tm, LANES), f32)],
        compiler_params=_cparams(2),
        name="in_proj",
    )(x, g, w_ext, cs)


def _compress_kernel(a_ref, pe_lo_ref, pe_hi_ref, w_lo_ref, w_hi_ref, w2_ref, kk_ref, vv_ref):
    a = a_ref[...]
    u = jnp.dot(_bf(a + pe_lo_ref[...]), w_lo_ref[...], preferred_element_type=jnp.float32)
    v = jnp.dot(_bf(a + pe_hi_ref[...]), w_hi_ref[...], preferred_element_type=jnp.float32)
    n = a.shape[0]
    pre = u + pltpu.roll(v, n - 1, axis=0)
    act = jax.nn.gelu(pre, approximate=True)
    kv = jnp.dot(_bf(act), w2_ref[...], preferred_element_type=jnp.float32)
    vk = _swap64(kv)
    lane = lax.broadcasted_iota(jnp.int32, (n, LANES), 1)
    kk_ref[...] = _bf(jnp.where(lane < 64, kv, vk))
    vv = jnp.where(lane < 64, vk, jnp.where(lane == 64, 1.0, 0.0))
    vv_ref[...] = _bf(vv.T[0:V_ROWS])


def _compress(l, a, pe_lo, pe_hi, w_lo, w_hi, w2):
    bsz, n, width = a.shape
    per_b = lambda c: pl.BlockSpec((None, n, c), lambda b: (b, 0, 0))
    return pl.pallas_call(
        _compress_kernel,
        grid=(bsz,),
        in_specs=[per_b(width)] + [_layer_spec(w, l) for w in (pe_lo, pe_hi, w_lo, w_hi, w2)],
        out_specs=[per_b(LANES), pl.BlockSpec((None, V_ROWS, n), lambda b: (b, 0, 0))],
        out_shape=[jax.ShapeDtypeStruct((bsz, n, LANES), MXU_DTYPE),
                   jax.ShapeDtypeStruct((bsz, V_ROWS, n), MXU_DTYPE)],
        compiler_params=_cparams(1),
        name="compress",
    )(a, pe_lo, pe_hi, w_lo, w_hi, w2)


def _nsa_kernel(qc_ref, qr_ref, gate_ref, kk_ref, vv_ref, mt_ref, mask_ref, ks_ref, vs_ref, kw_ref, vw_ref,
                out_ref, m_ref, acc_ref, s0_ref, s1_ref, mx0_ref, mx1_ref, ocmp_ref, imp_ref, *, tq, tk, seq):
    qi = pl.program_id(1)
    t0 = qi * tq
    rows = NSA_HEADS * tq
    n_cmp_pad = kk_ref.shape[0]
    top_n = min(SEL_TOPK, seq // SEL_LEN)
    n_sel = LANES
    win_keys = min(WINDOW + tq, seq)

    lane =lax.broadcasted_iota(jnp.int32, (tq, LANES), 1)

    def head_rows(q_ref):
        parts = []
        for hd in range(NSA_HEADS):
            pair = q_ref[:, (hd // 2) * LANES:(hd // 2 + 1) * LANES]
            keep = (lane < 64) if hd % 2 == 0 else (lane >= 64)
            parts.append(jnp.where(keep, pair, jnp.zeros_like(pair)))
        return jnp.concatenate(parts, axis=0)

    q_c = head_rows(qc_ref)
    q_r = head_rows(qr_ref)
    pos_l = t0 + (lax.broadcasted_iota(jnp.int32, (1, rows), 1) & (tq - 1))

    last_blk = (pos_l - (CMP_LEN - 1)) >> 4

    def compressed(nr):
        s = lax.dot_general(kk_ref[0:nr, :], q_c, _NT, preferred_element_type=jnp.float32)
        n_idx = lax.broadcasted_iota(jnp.int32, (nr, rows), 0)
        sm = jnp.where(n_idx <= last_blk, s, NEG_INF)
        mx = jnp.max(sm, axis=0, keepdims=True)
        e = jnp.exp2(sm - mx)
        den = jnp.sum(e, axis=0, keepdims=True)
        p_cmp = e * jnp.where(last_blk >= 0, 1.0 / den, 0.0)
        ocmp_ref[...] = jnp.dot(vv_ref[:, 0:nr], _bf(p_cmp), preferred_element_type=jnp.float32)
        p_sum = p_cmp[:, 0:tq] + p_cmp[:, tq:2 * tq] + p_cmp[:, 2 * tq:3 * tq] + p_cmp[:, 3 * tq:4 * tq]
        hi = _bf(p_sum)
        r1 = p_sum - hi.astype(jnp.float32)
        mid = _bf(r1)
        lo = _bf(r1 - mid.astype(jnp.float32))
        mt = mt_ref[:, 0:nr]
        imp_ref[...] = (jnp.dot(mt, hi, preferred_element_type=jnp.float32)
                        + jnp.dot(mt, mid, preferred_element_type=jnp.float32)
                        + jnp.dot(mt, lo, preferred_element_type=jnp.float32))

    n_var = n_cmp_pad // LANES
    tile_last = jnp.maximum((t0 + tq - CMP_LEN) >> 4, 0)
    need = jnp.minimum(tile_last // LANES, n_var - 1)
    for v in range(n_var):
        pl.when(need == v)(functools.partial(compressed, (v + 1) * LANES))
    o_cmp = ocmp_ref[...]
    imp = imp_ref[...]

    start = pl.multiple_of(jnp.maximum(t0 + tq - win_keys, 0), tq)
    sw = lax.dot_general(kw_ref[pl.ds(start, win_keys), :], q_r, _NT, preferred_element_type=jnp.float32)
    w_case = jnp.minimum(t0 // tq, 2)
    m_idx = (jnp.where(w_case == 0, 1, jnp.where(w_case == 1, 0, 3)),
             jnp.where(w_case == 0, 2, jnp.where(w_case == 1, 1, 0)),
             jnp.where(w_case == 2, 1, 2))
    sw = jnp.concatenate([sw[j * tq:(j + 1) * tq] + mask_ref[m_idx[j]] for j in range(3)], axis=0)
    mw = jnp.max(sw, axis=0, keepdims=True)
    pw = _bf(jnp.exp2(sw - mw))
    wt0 = start // tq
    ow = jnp.dot(vw_ref[wt0], pw[0:tq], preferred_element_type=jnp.float32)
    for j in range(1, win_keys // tq):
        ow = ow + jnp.dot(vw_ref[wt0 + j], pw[j * tq:(j + 1) * tq], preferred_element_type=jnp.float32)
    o_win = ow / ow[64:65, :]

    blk = lax.broadcasted_iota(jnp.int32, (n_sel, tq), 0)
    blk_f = blk.astype(jnp.float32)
    pos_t = t0 + lax.broadcasted_iota(jnp.int32, (n_sel, tq), 1)
    cur = pos_t >> 6
    forced = (blk == 0) | (blk == cur) | (blk == cur - 1)
    valid = (blk * SEL_LEN) <= pos_t
    val = jnp.where(valid & jnp.logical_not(forced), imp, NEG_INF)
    for _ in range(top_n - 3):
        best = jnp.max(val, axis=0, keepdims=True)
        first = jnp.min(jnp.where(val == best, blk_f, float(n_sel)), axis=0, keepdims=True)
        val = jnp.where(blk_f == first, REMOVED, val)
    bias_t = jnp.where((forced | (val == REMOVED)) & valid, 0.0, NEG_INF)
    bias = _bf(bias_t.T)

    l_sel = jnp.concatenate(
        [jnp.concatenate([bias, q_r[hd * tq:(hd + 1) * tq]], axis=1) for hd in range(NSA_HEADS)], axis=0)
    m_ref[...] = jnp.full((1, rows), NEG_INF, jnp.float32)
    acc_ref[...] = jnp.zeros((V_ROWS, rows), jnp.float32)

    def scores_to(buf, kt, causal):
        s_ref, mx_ref = buf
        k0 = pl.multiple_of(kt * tk, tk)
        sc = lax.dot_general(ks_ref[pl.ds(k0, tk), :], l_sel, _NT, preferred_element_type=jnp.float32)
        if causal:
            own = (t0 - k0) // tq
            lo = sc[0:tq] + mask_ref[1 - own]
            hi = sc[tq:2 * tq] + mask_ref[2 - own]
            s_ref[0:tq, :] = lo
            s_ref[tq:2 * tq, :] = hi
            mx_ref[...] = jnp.maximum(jnp.max(lo, axis=0, keepdims=True), jnp.max(hi, axis=0, keepdims=True))
        else:
            s_ref[...] = sc
            mx_ref[...] = jnp.max(sc, axis=0, keepdims=True)

    def softmax_from(buf, kt):
        s_ref, mx_ref = buf
        sc = s_ref[...]
        m_old = m_ref[...]
        m_new = jnp.maximum(m_old, mx_ref[...])
        alpha = jnp.exp2(m_old - m_new)
        p = jnp.exp2(sc - m_new)
        acc_ref[...] = alpha * acc_ref[...] + jnp.dot(vs_ref[kt], _bf(p), preferred_element_type=jnp.float32)
        m_ref[...] = m_new

    buf0, buf1 = (s0_ref, mx0_ref), (s1_ref, mx1_ref)
    n_full = t0 // tk
    n_pairs = n_full // 2
    scores_to(buf0, n_full, True)

    def pair(j, carry):
        k = 2 * j
        scores_to(buf1, k, False)
        softmax_from(buf0, jnp.where(j == 0, n_full, k - 1))
        scores_to(buf0, k + 1, False)
        softmax_from(buf1, k)
        return carry

    lax.fori_loop(0, n_pairs, pair, 0)
    pending = jnp.where(n_pairs == 0, n_full, 2 * n_pairs - 1)

    @pl.when(n_full % 2 == 1)
    def _():
        scores_to(buf1, n_full - 1, False)
        softmax_from(buf0, pending)
        softmax_from(buf1, n_full - 1)

    @pl.when(n_full % 2 == 0)
    def _():
        softmax_from(buf0, pending)

    acc = acc_ref[...]
    o_slc = acc / acc[64:65, :]

    gate_t = gate_ref[...].T
    y_t = []
    for hd in range(NSA_HEADS):
        c = slice(hd * tq, (hd + 1) * tq)
        y_t.append(gate_t[3 * hd:3 * hd + 1] * o_cmp[0:64, c]
                   + gate_t[3 * hd + 1:3 * hd + 2] * o_slc[0:64, c]
                   + gate_t[3 * hd + 2:3 * hd + 3] * o_win[0:64, c])
    out_ref[:, 0:LANES] = _bf(jnp.concatenate(y_t[0:2], axis=0).T)
    out_ref[:, LANES:2 * LANES] = _bf(jnp.concatenate(y_t[2:4], axis=0).T)


def _nsa(qc, qr, gate, kk, vv, mt, ks, vs, kw, vw):
    bsz, seq, _ = qc.shape
    tq = min(TQ, seq)
    tk = min(TK, seq)
    n_cmp_pad = kk.shape[1]
    rows = NSA_HEADS * tq
    assert tk == 2 * tq and min(WINDOW + tq, seq) == 3 * tq, "mask patterns assume 2 / 3 query-tile-sized key blocks"
    r_idx = lax.broadcasted_iota(jnp.int32, (tq, rows), 0)
    tau = lax.broadcasted_iota(jnp.int32, (tq, rows), 1) & (tq - 1)
    zero = jnp.zeros((tq, rows), jnp.float32)
    masks = jnp.stack([zero, jnp.where(r_idx <= tau, 0.0, NEG_INF), zero + NEG_INF,
                       jnp.where(r_idx > tau, 0.0, NEG_INF)])
    tile = lambda c: pl.BlockSpec((None, tq, c), lambda b, i: (b, i, 0))
    per_b = lambda r, c: pl.BlockSpec((None, r, c), lambda b, i: (b, 0, 0))
    per_b4 = lambda a: pl.BlockSpec((None,) + a.shape[1:], lambda b, i: (b, 0, 0, 0))
    const = lambda a: pl.BlockSpec(a.shape, lambda b, i: (0,) * a.ndim, pipeline_mode=pl.Buffered(1))
    return pl.pallas_call(
        functools.partial(_nsa_kernel, tq=tq, tk=tk, seq=seq),
        grid=(bsz, seq // tq),
        in_specs=[tile(256), tile(256), tile(LANES),
                  per_b(n_cmp_pad, LANES), per_b(V_ROWS, n_cmp_pad), const(mt), const(masks),
                  per_b(seq, 256), per_b4(vs), per_b(seq, LANES), per_b4(vw)],
        out_specs=tile(256),
        out_shape=jax.ShapeDtypeStruct((bsz, seq, 256), MXU_DTYPE),
        scratch_shapes=[pltpu.VMEM((1, rows), jnp.float32), pltpu.VMEM((V_ROWS, rows), jnp.float32),
                        pltpu.VMEM((tk, rows), jnp.float32), pltpu.VMEM((tk, rows), jnp.float32),
                        pltpu.VMEM((1, rows), jnp.float32), pltpu.VMEM((1, rows), jnp.float32),
                        pltpu.VMEM((V_ROWS, rows), jnp.float32), pltpu.VMEM((LANES, tq), jnp.float32)],
        compiler_params=_cparams(2),
        name="nsa",
    )(qc, qr, gate, kk, vv, mt, masks, ks, vs, kw, vw)


def _mix_out_kernel(x_ref, za_ref, zah_ref, zc_ref, zd_ref, zdh_ref, ynsa_ref,
                    cw_ref, cb_ref, clg_ref, clb_ref, slg_ref, slb_ref, sw_ref, sb_ref,
                    pw_ref, ps_ref, wout_ref, gpost_ref, out_ref,
                    uext_ref, ushift_ref, ycat_ref, pext_ref, *, tm):
    i = pl.program_id(1)
    not_first = jnp.where(i > 0, 1.0, 0.0)

    zc = jax.nn.gelu(zc_ref[...], approximate=True)
    u_s = zc[:, :256]
    v_s = _layer_norm(zc[:, 256:], slg_ref[...], slb_ref[...])
    t_i = lax.broadcasted_iota(jnp.int32, (SGU_CHUNK, SGU_GROUPS * SGU_CHUNK), 0)
    s_i = lax.rem(lax.broadcasted_iota(jnp.int32, (SGU_CHUNK, SGU_GROUPS * SGU_CHUNK), 1), SGU_CHUNK)
    w_cat = _bf(jnp.where(s_i <= t_i, sw_ref[...], 0.0))
    grp = lax.broadcasted_iota(jnp.int32, (SGU_CHUNK, 256), 1) >> 6
    ysgu = []
    for c in range(tm // SGU_CHUNK):
        vc = v_s[c * SGU_CHUNK:(c + 1) * SGU_CHUNK]
        stacked = _bf(jnp.concatenate([jnp.where(grp == g, vc, 0.0) for g in range(SGU_GROUPS)], axis=0))
        f = jnp.dot(w_cat, stacked, preferred_element_type=jnp.float32) + sb_ref[...]
        ysgu.append(u_s[c * SGU_CHUNK:(c + 1) * SGU_CHUNK] * f)
    ycat_ref[:, 512:768] = _bf(jnp.concatenate(ysgu, axis=0))

    pext_ref[0:POOL_HALO, :] = zdh_ref[...] * not_first
    pext_ref[POOL_HALO:, :] = zd_ref[...]
    n_ext = tm + POOL_HALO
    sums = []
    for shift in (1, 2, 4, 8):
        n = n_ext - shift
        nxt = pext_ref[pl.ds(shift, n), :] + pext_ref[pl.ds(0, n), :]
        pext_ref[pl.ds(shift, n), :] = nxt
        sums.append(pext_ref[pl.ds(POOL_HALO, tm), :])
    lane_grp = lax.broadcasted_iota(jnp.int32, (tm, 256), 1) >> 6
    pos = i * tm + lax.broadcasted_iota(jnp.int32, (tm, 256), 0)
    wsum = jnp.where(lane_grp == 0, sums[0], jnp.where(lane_grp == 1, sums[1],
                     jnp.where(lane_grp == 2, sums[2], sums[3])))
    width = jnp.where(lane_grp == 0, POOL_WINDOWS[0], jnp.where(lane_grp == 1, POOL_WINDOWS[1],
                      jnp.where(lane_grp == 2, POOL_WINDOWS[2], POOL_WINDOWS[3])))
    cnt = jnp.minimum(pos + 1, width).astype(jnp.float32)
    pooled = wsum / cnt - zd_ref[...]
    ycat_ref[:, 768:1024] = _bf(jnp.dot(_bf(pooled), pw_ref[...], preferred_element_type=jnp.float32) * ps_ref[...])

    ycat_ref[:, 256:512] = ynsa_ref[...]
    y_rest = (jnp.dot(ycat_ref[:, 256:512], wout_ref[256:512, :], preferred_element_type=jnp.float32)
              + jnp.dot(ycat_ref[:, 512:768], wout_ref[512:768, :], preferred_element_type=jnp.float32)
              + jnp.dot(ycat_ref[:, 768:1024], wout_ref[768:1024, :], preferred_element_type=jnp.float32))

    def glu(z):
        return z[:, :256] * jax.nn.sigmoid(z[:, 256:])

    uext_ref[0:CONV_HALO, :] = glu(zah_ref[...]) * not_first
    uext_ref[CONV_HALO:, :] = glu(za_ref[...])
    first = CONV_HALO - (CONV_K - 1)
    n_shift = tm + CONV_HALO - SUBLANES
    for r in range(SUBLANES):
        n = n_shift if first + r + n_shift <= tm + CONV_HALO else n_shift - SUBLANES
        ushift_ref[r, 0:n, :] = uext_ref[pl.ds(first + r, n), :]
    rc = 64
    for r0 in range(0, tm, rc):
        acc = jnp.zeros((rc, 256), jnp.float32)
        for k in range(CONV_K):
            acc = acc + cw_ref[k:k + 1, :] * ushift_ref[k % SUBLANES, pl.ds(r0 + k - k % SUBLANES, rc), :]
        yc = _layer_norm(acc + cb_ref[...], clg_ref[...], clb_ref[...])
        ycat_ref[r0:r0 + rc, 0:256] = _bf(yc * jax.nn.sigmoid(yc))

    y = y_rest + jnp.dot(ycat_ref[:, 0:256], wout_ref[0:256, :], preferred_element_type=jnp.float32)
    out_ref[...] = x_ref[...] + _rms(y, gpost_ref[...])


def _mix_out(l, x, za, zc, zd, ynsa, cw, cb, clg, clb, slg, slb, sw_cat, sb_exp, pw_bd, ps, wout, gpost):
    bsz, seq, d = x.shape
    tm = min(TM_PROJ, seq)
    tok = lambda c: pl.BlockSpec((None, tm, c), lambda b, i: (b, i, 0))
    halo = lambda c, h: pl.BlockSpec((None, h, c), lambda b, i: (b, jnp.maximum(i * (tm // h) - 1, 0), 0))
    weights = [cw, cb, clg, clb, slg, slb, sw_cat, sb_exp, pw_bd, ps, wout, gpost]
    return pl.pallas_call(
        functools.partial(_mix_out_kernel, tm=tm),
        grid=(bsz, seq // tm),
        in_specs=[tok(d), tok(512), halo(512, CONV_HALO), tok(512), tok(256), halo(256, POOL_HALO), tok(256)]
                 + [_layer_spec(w, l) for w in weights],
        out_specs=tok(d),
        out_shape=jax.ShapeDtypeStruct((bsz, seq, d), jnp.float32),
        scratch_shapes=[pltpu.VMEM((tm + CONV_HALO, 256), jnp.float32),
                        pltpu.VMEM((SUBLANES, tm + CONV_HALO - SUBLANES, 256), jnp.float32),
                        pltpu.VMEM((tm, 4 * 256), MXU_DTYPE),
                        pltpu.VMEM((tm + POOL_HALO, 256), jnp.float32)],
        compiler_params=_cparams(2),
        name="mix_out",
    )(x, za, za, zc, zd, zd, ynsa, *weights)


def _ffn_kernel(x_ref, xh_ref, gpre_ref, wup_ref, cw_ref, cb_ref, wdn_ref, gpost_ref, out_ref,
                hext_ref, g0_ref, u0_ref, g1_ref, u1_ref, acc_ref, *, tm, n_chunks, chunk):
    i = pl.program_id(1)
    not_first = jnp.where(i > 0, 1.0, 0.0)
    x = x_ref[...]
    hext_ref[0:FFN_HALO, :] = _bf(_rms(xh_ref[...], gpre_ref[...]) * not_first)
    hext_ref[FFN_HALO:, :] = _bf(_rms(x, gpre_ref[...]))

    def cols(c):
        return pl.ds(pl.multiple_of(c * chunk, chunk), chunk)

    def up_to(g_ref, u_ref, c):
        h_ext = hext_ref[...]
        g_ref[...] = jnp.dot(h_ext, wup_ref[:, cols(c)], preferred_element_type=jnp.float32)
        u_ref[...] = jnp.dot(h_ext, wup_ref[:, cols(n_chunks + c)], preferred_element_type=jnp.float32)

    def conv(ext_ref, c):
        w = cw_ref[:, cols(c)]
        return (w[0:1] * ext_ref[pl.ds(FFN_HALO - 2, tm), :] + w[1:2] * ext_ref[pl.ds(FFN_HALO - 1, tm), :]
                + w[2:3] * ext_ref[pl.ds(FFN_HALO, tm), :] + cb_ref[:, cols(c)])

    def down_from(g_ref, u_ref, c):
        act = jax.nn.gelu(conv(g_ref, c), approximate=True) * conv(u_ref, n_chunks + c)
        acc_ref[...] += jnp.dot(_bf(act), wdn_ref[c], preferred_element_type=jnp.float32)

    assert n_chunks % 2 == 1
    acc_ref[...] = jnp.zeros_like(acc_ref)
    up_to(g0_ref, u0_ref, 0)

    def pair(j, carry):
        c = 2 * j
        up_to(g1_ref, u1_ref, c + 1)
        down_from(g0_ref, u0_ref, c)
        up_to(g0_ref, u0_ref, c + 2)
        down_from(g1_ref, u1_ref, c + 1)
        return carry

    lax.fori_loop(0, (n_chunks - 1) // 2, pair, 0)
    down_from(g0_ref, u0_ref, n_chunks - 1)
    out_ref[...] = x + _rms(acc_ref[...], gpost_ref[...])


def _ffn(l, x, gpre, wup, cw, cb, wdn, gpost):
    bsz, seq, d = x.shape
    tm = min(TM_FFN, seq)
    n_chunks = wdn.shape[1]
    chunk = wdn.shape[2]
    tok = pl.BlockSpec((None, tm, d), lambda b, i: (b, i, 0))
    halo = pl.BlockSpec((None, FFN_HALO, d), lambda b, i: (b, jnp.maximum(i * (tm // FFN_HALO) - 1, 0), 0))
    weights = [gpre, wup, cw, cb, wdn, gpost]
    return pl.pallas_call(
        functools.partial(_ffn_kernel, tm=tm, n_chunks=n_chunks, chunk=chunk),
        grid=(bsz, seq // tm),
        in_specs=[tok, halo] + [_layer_spec(w, l) for w in weights],
        out_specs=tok,
        out_shape=jax.ShapeDtypeStruct((bsz, seq, d), jnp.float32),
        scratch_shapes=[pltpu.VMEM((tm + FFN_HALO, d), MXU_DTYPE)]
                       + [pltpu.VMEM((tm + FFN_HALO, chunk), jnp.float32)] * 4
                       + [pltpu.VMEM((tm, d), jnp.float32)],
        compiler_params=_cparams(2),
        name="ffn",
    )(x, x, *weights)


def _prep_w_in(w):
    n_in = w.shape[-1]
    n_pad = -(-n_in // 256) * 256
    return _bf(jnp.pad(w, ((0, 0), (0, 0), (0, n_pad - n_in))))


def _prep_compress(pe_k, pe_v, ck_w1, ck_w2, cv_w1, cv_w2):
    half = CMP_LEN // 2

    def w1_part(w1k, w1v, lo):
        wk = w1k.reshape(CMP_LEN, HEAD_DIM, HEAD_DIM)[lo:lo + half]
        wv = w1v.reshape(CMP_LEN, HEAD_DIM, HEAD_DIM)[lo:lo + half]
        z = jnp.zeros_like(wk)
        top = jnp.concatenate([wk, z], axis=-1)
        bot = jnp.concatenate([z, wv], axis=-1)
        return jnp.concatenate([top, bot], axis=1).reshape(half * 2 * HEAD_DIM, 2 * HEAD_DIM)

    def pe_part(lo):
        return jnp.concatenate([pe_k[lo:lo + half], pe_v[lo:lo + half]], axis=1).reshape(1, half * 2 * HEAD_DIM)

    z = jnp.zeros_like(ck_w2)
    w2 = jnp.concatenate([jnp.concatenate([ck_w2, z], axis=1), jnp.concatenate([z, cv_w2], axis=1)], axis=0)
    return (pe_part(0), pe_part(half), _bf(w1_part(ck_w1, cv_w1, 0)), _bf(w1_part(ck_w1, cv_w1, half)), _bf(w2))


def _importance_matrix(seq):
    n_cmp = (seq - CMP_LEN) // CMP_STRIDE + 1
    n_sel = seq // SEL_LEN
    ratio = SEL_LEN // CMP_STRIDE
    n_ov = CMP_LEN // CMP_STRIDE
    n_pad = -(-n_cmp // LANES) * LANES
    assert n_sel <= LANES, "selection blocks must fit one lane group"
    assert min(SEL_TOPK, n_sel) > 3, "top-k must exceed the three forced blocks"
    mt = np.zeros((LANES, n_pad), np.float32)
    for j in range(n_sel):
        for m in range(ratio):
            for n in range(n_ov):
                c = ratio * j + m - n
                if 0 <= c < n_cmp:
                    mt[j, c] += 1.0
    return jnp.asarray(mt, MXU_DTYPE)


def _rope_table(seq):
    half = HEAD_DIM // 2
    inv = ROPE_THETA ** (-jnp.arange(half, dtype=jnp.float32) * 2.0 / HEAD_DIM)
    ang = jnp.arange(seq, dtype=jnp.float32)[:, None] * inv[None, :]
    cos, sin = jnp.cos(ang), jnp.sin(ang)
    return jnp.concatenate([cos, cos, sin, sin], axis=1)


def _block_diag(w):
    g, c, d = w.shape
    out = jnp.zeros((g * c, g * d), w.dtype)
    for j in range(g):
        out = out.at[j * c:(j + 1) * c, j * d:(j + 1) * d].set(w[j])
    return out


def kernel(x, norm_mix_pre, norm_mix_post, norm_ffn_pre, norm_ffn_post, w_in, w_out, conv_dw_w, conv_dw_b, conv_ln_g, conv_ln_b, nsa_pe_k, nsa_pe_v, nsa_ck_w1, nsa_ck_w2, nsa_cv_w1, nsa_cv_w2, sgu_ln_g, sgu_ln_b, sgu_w, sgu_b, pool_w, pool_scale, ffn_up, ffn_conv_w, ffn_conv_b, ffn_down):
    bsz, seq, d = x.shape
    depth = w_in.shape[0]
    ffn_dim = ffn_down.shape[1]
    cs = _rope_table(seq)
    mt = _importance_matrix(seq)
    rows = lambda v: v[:, None, :]

    w_ext = _prep_w_in(w_in)
    cmp_w = jax.vmap(_prep_compress)(nsa_pe_k, nsa_pe_v, nsa_ck_w1, nsa_ck_w2, nsa_cv_w1, nsa_cv_w2)
    sw_cat = jnp.transpose(sgu_w, (0, 2, 1, 3)).reshape(depth, SGU_CHUNK, SGU_GROUPS * SGU_CHUNK)
    sb_exp = jnp.repeat(jnp.swapaxes(sgu_b, 1, 2), 256 // SGU_GROUPS, axis=2)
    mix_w = [conv_dw_w, rows(conv_dw_b), rows(conv_ln_g), rows(conv_ln_b), rows(sgu_ln_g), rows(sgu_ln_b),
             sw_cat, sb_exp, _bf(jax.vmap(_block_diag)(pool_w)), rows(pool_scale), _bf(w_out), rows(norm_mix_post)]
    ffn_w = [rows(norm_ffn_pre), _bf(ffn_up), ffn_conv_w, rows(ffn_conv_b),
             _bf(ffn_down).reshape(depth, ffn_dim // FFN_CHUNK, FFN_CHUNK, d), rows(norm_ffn_post)]
    g_pre = rows(norm_mix_pre)

    for l in range(depth):
        za, zc, zd, qc, qr, a_cmp, ks, vs, kw, vw, gate = _in_proj(l, x, g_pre, w_ext, cs)
        kk, vv = _compress(l, a_cmp, *cmp_w)
        y_nsa = _nsa(qc, qr, gate, kk, vv, mt, ks, vs, kw, vw)
        x = _mix_out(l, x, za, zc, zd, y_nsa, *mix_w)
        x = _ffn(l, x, *ffn_w)
    return x
```

```python
import functools

import numpy as np
import jax
import jax.numpy as jnp
from jax import lax
from jax.experimental import pallas as pl
from jax.experimental.pallas import tpu as pltpu

HEAD_DIM = 64
NSA_HEADS = 4
CONV_K = 31
CMP_LEN = 32
CMP_STRIDE = 16
SEL_LEN = 64
SEL_TOPK = 16
WINDOW = 512
SGU_CHUNK = 128
SGU_GROUPS = 4
POOL_WINDOWS = (2, 4, 8, 16)
FFN_CONV_K = 3
ROPE_THETA = 10000.0
RMS_EPS = 1e-6
LN_EPS = 1e-5
NEG_INF = -1e30
FORCE_SCORE = 1e6
REMOVED = -3e38
LOG2E = 1.4426950408889634

LANES = 128
SUBLANES = 8
V_ROWS = 80
MXU_DTYPE = jnp.bfloat16
VMEM_LIMIT = 56 * 1024 * 1024

TM_IN = 1024
TM_MIX = 512
TM_FFN = 1024
TQ = 256
TK = 512
FFN_CHUNK = 256
CONV_HALO = 32
POOL_HALO = 16
FFN_HALO = 16

_NT = (((1,), (1,)), ((), ()))


def _cparams(n_axes, flags=None):
    return pltpu.CompilerParams(dimension_semantics=("arbitrary",) * n_axes,
                                vmem_limit_bytes=VMEM_LIMIT, flags=flags)


def _layer_spec(a, l):
    return pl.BlockSpec((None,) + a.shape[1:], lambda *_: (l,) + (0,) * (a.ndim - 1),
                        pipeline_mode=pl.Buffered(1))


def _rms(x, g):
    return x * lax.rsqrt(jnp.mean(x * x, axis=-1, keepdims=True) + RMS_EPS) * g


def _layer_norm(x, g, b):
    mu = jnp.mean(x, axis=-1, keepdims=True)
    var = jnp.mean(jnp.square(x - mu), axis=-1, keepdims=True)
    return (x - mu) * lax.rsqrt(var + LN_EPS) * g + b


def _bf(x):
    return x.astype(MXU_DTYPE)


def _swap64(x):
    return pltpu.roll(x, 64, axis=1)


def _in_proj_kernel(x_ref, g_ref, w_ref, cs_ref,
                    za_ref, zc_ref, zd_ref, qc_ref, qr_ref, kv_ref,
                    ks_ref, vs_ref, kw_ref, vw_ref, gate_ref, kvt_ref, *, tm, tk):
    i = pl.program_id(1)
    h = _bf(_rms(x_ref[...], g_ref[...]))

    def proj(lo, hi):
        return jnp.dot(h, w_ref[:, lo:hi], preferred_element_type=jnp.float32)

    lane = lax.broadcasted_iota(jnp.int32, (tm, LANES), 1)
    q = proj(512, 768)
    kv01 = proj(768, 1024)
    kv2t = proj(1024, 1280)
    kvt_ref[...] = kv01[:, :LANES]
    group = CMP_LEN // 2
    for j in range(group):
        kv_ref[:, j * LANES:(j + 1) * LANES] = kvt_ref[pl.ds(j, tm // group, stride=group), :]
    kvs, kvw, tail0 = kv01[:, LANES:], kv2t[:, :LANES], kv2t[:, LANES:]
    kx = jnp.where(lane < 64, kvs, _swap64(kvw))
    vsw = jnp.where(lane < 64, _swap64(kvs), kvw)
    gate_ref[...] = jax.nn.sigmoid(tail0)

    cs = cs_ref[...]
    sc = _swap64(cs)
    cos2 = jnp.where(lane < 64, cs, sc)
    sin2 = jnp.where(lane < 64, sc, cs)
    first_half = (lane & (HEAD_DIM - 1)) < HEAD_DIM // 2

    def rope(x):
        swapped = jnp.where(first_half, -pltpu.roll(x, LANES - HEAD_DIM // 2, axis=1),
                            pltpu.roll(x, HEAD_DIM // 2, axis=1))
        return x * cos2 + swapped * sin2

    scale = HEAD_DIM ** -0.5 * LOG2E
    qc_ref[...] = _bf(q * scale)
    for pair in range(NSA_HEADS // 2):
        c = slice(pair * LANES, (pair + 1) * LANES)
        qr_ref[:, c] = _bf(rope(q[:, c]) * scale)

    kx_rot = rope(kx)
    kx_swap = _swap64(kx_rot)
    k_rot2 = jnp.where(lane < 64, kx_rot, kx_swap)
    kw_rot2 = jnp.where(lane < 64, kx_swap, kx_rot)
    pos = i * tm + lax.broadcasted_iota(jnp.int32, (tm, LANES), 0)
    onehot =jnp.where((pos >> 6) == lane, 1.0, 0.0)
    ks_ref[:, 0:LANES] = _bf(onehot)
    ks_ref[:, LANES:2 * LANES] = _bf(k_rot2)
    kw_ref[...] = _bf(kw_rot2)
    ones_col = jnp.where(lane == 64, 1.0, 0.0)
    vs = jnp.where(lane < 64, vsw, ones_col)
    vw = jnp.where(lane < 64, _swap64(vsw), ones_col)
    for j in range(tm // tk):
        vs_ref[j] = _bf(vs[j * tk:(j + 1) * tk].T[0:V_ROWS])
    for j in range(tm // TQ):
        vw_ref[j] = _bf(vw[j * TQ:(j + 1) * TQ].T[0:V_ROWS])

    tail = jnp.concatenate([tail0, proj(1280, 2048)], axis=1)
    n_gate = 3 * NSA_HEADS
    zc_ref[...] = tail[:, n_gate:n_gate + 512]
    zd_ref[...] = tail[:, n_gate + 512:n_gate + 768]
    za_ref[...] = proj(0, 512)


def _in_proj(l, x, g, w_ext, cs):
    bsz, seq, d = x.shape
    tm = min(TM_IN, seq)
    tk = min(TK, seq)
    tok = lambda c: pl.BlockSpec((None, tm, c), lambda b, i: (b, i, 0))
    f32, bf16 = jnp.float32, MXU_DTYPE

    def tok_out(c, dt):
        return tok(c), jax.ShapeDtypeStruct((bsz, seq, c), dt)

    def keytile_out(t):
        return (pl.BlockSpec((None, tm // t, V_ROWS, t), lambda b, i: (b, i, 0, 0)),
                jax.ShapeDtypeStruct((bsz, seq // t, V_ROWS, t), bf16))

    group = CMP_LEN // 2
    cmp_out = (pl.BlockSpec((None, tm // group, group * LANES), lambda b, i: (b, i, 0)),
               jax.ShapeDtypeStruct((bsz, seq // group, group * LANES), f32))
    outs = [tok_out(512, f32), tok_out(512, f32), tok_out(256, f32), tok_out(256, bf16), tok_out(256, bf16),
            cmp_out, tok_out(256, bf16), keytile_out(tk), tok_out(128, bf16), keytile_out(TQ),
            tok_out(128, f32)]
    return pl.pallas_call(
        functools.partial(_in_proj_kernel, tm=tm, tk=tk),
        grid=(bsz, seq // tm),
        in_specs=[tok(d), _layer_spec(g, l), _layer_spec(w_ext, l),
                  pl.BlockSpec((tm, LANES), lambda b, i: (i, 0))],
        out_specs=[spec for spec, _ in outs],
        out_shape=[shape for _, shape in outs],
        scratch_shapes=[pltpu.VMEM((tm, LANES), f32)],
        compiler_params=_cparams(2),
        name="in_proj",
    )(x, g, w_ext, cs)


def _compress_kernel(a_ref, pe_lo_ref, pe_hi_ref, w_lo_ref, w_hi_ref, w2_ref, kk_ref, vv_ref):
    a = a_ref[...]
    u = jnp.dot(_bf(a + pe_lo_ref[...]), w_lo_ref[...], preferred_element_type=jnp.float32)
    v = jnp.dot(_bf(a + pe_hi_ref[...]), w_hi_ref[...], preferred_element_type=jnp.float32)
    n = a.shape[0]
    pre = u + pltpu.roll(v, n - 1, axis=0)
    act = jax.nn.gelu(pre, approximate=True)
    kv = jnp.dot(_bf(act), w2_ref[...], preferred_element_type=jnp.float32)
    vk = _swap64(kv)
    lane = lax.broadcasted_iota(jnp.int32, (n, LANES), 1)
    kk_ref[...] = _bf(jnp.where(lane < 64, kv, vk))
    vv = jnp.where(lane < 64, vk, jnp.where(lane == 64, 1.0, 0.0))
    vv_ref[...] = _bf(vv.T[0:V_ROWS])


def _compress(l, a, pe_lo, pe_hi, w_lo, w_hi, w2):
    bsz, n, width = a.shape
    per_b = lambda c: pl.BlockSpec((None, n, c), lambda b: (b, 0, 0))
    return pl.pallas_call(
        _compress_kernel,
        grid=(bsz,),
        in_specs=[per_b(width)] + [_layer_spec(w, l) for w in (pe_lo, pe_hi, w_lo, w_hi, w2)],
        out_specs=[per_b(LANES), pl.BlockSpec((None, V_ROWS, n), lambda b: (b, 0, 0))],
        out_shape=[jax.ShapeDtypeStruct((bsz, n, LANES), MXU_DTYPE),
                   jax.ShapeDtypeStruct((bsz, V_ROWS, n), MXU_DTYPE)],
        compiler_params=_cparams(1),
        name="compress",
    )(a, pe_lo, pe_hi, w_lo, w_hi, w2)


def _nsa_kernel(qc_ref, qr_ref, gate_ref, kk_ref, vv_ref, mt_ref, mask_ref, ks_ref, vs_ref, kw_ref, vw_ref,
                out_ref, m_ref, acc_ref, s0_ref, s1_ref, mx0_ref, mx1_ref, ocmp_ref, imp_ref, *, tq, tk, seq):
    qi = pl.program_id(1)
    t0 = qi * tq
    rows = NSA_HEADS * tq
    n_cmp_pad = kk_ref.shape[0]
    top_n = min(SEL_TOPK, seq // SEL_LEN)
    n_sel = LANES
    win_keys = min(WINDOW + tq, seq)

    lane =lax.broadcasted_iota(jnp.int32, (tq, LANES), 1)

    def head_rows(q_ref):
        parts = []
        for hd in range(NSA_HEADS):
            pair = q_ref[:, (hd // 2) * LANES:(hd // 2 + 1) * LANES]
            keep = (lane < 64) if hd % 2 == 0 else (lane >= 64)
            parts.append(jnp.where(keep, pair, jnp.zeros_like(pair)))
        return jnp.concatenate(parts, axis=0)

    q_c = head_rows(qc_ref)
    q_r = head_rows(qr_ref)
    pos_l = t0 + (lax.broadcasted_iota(jnp.int32, (1, rows), 1) & (tq - 1))

    last_blk = (pos_l - (CMP_LEN - 1)) >> 4

    def compressed(nr):
        s = lax.dot_general(kk_ref[0:nr, :], q_c, _NT, preferred_element_type=jnp.float32)
        n_idx = lax.broadcasted_iota(jnp.int32, (nr, rows), 0)
        sm = jnp.where(n_idx <= last_blk, s, NEG_INF)
        mx = jnp.max(sm, axis=0, keepdims=True)
        e = jnp.exp2(sm - mx)
        den = jnp.sum(e, axis=0, keepdims=True)
        p_cmp = e * jnp.where(last_blk >= 0, 1.0 / den, 0.0)
        ocmp_ref[...] = jnp.dot(vv_ref[:, 0:nr], _bf(p_cmp), preferred_element_type=jnp.float32)
        p_sum = p_cmp[:, 0:tq] + p_cmp[:, tq:2 * tq] + p_cmp[:, 2 * tq:3 * tq] + p_cmp[:, 3 * tq:4 * tq]
        hi = _bf(p_sum)
        r1 = p_sum - hi.astype(jnp.float32)
        mid = _bf(r1)
        lo = _bf(r1 - mid.astype(jnp.float32))
        mt = mt_ref[:, 0:nr]
        imp_ref[...] = (jnp.dot(mt, hi, preferred_element_type=jnp.float32)
                        + jnp.dot(mt, mid, preferred_element_type=jnp.float32)
                        + jnp.dot(mt, lo, preferred_element_type=jnp.float32))

    n_var = n_cmp_pad // LANES
    tile_last = jnp.maximum((t0 + tq - CMP_LEN) >> 4, 0)
    need = jnp.minimum(tile_last // LANES, n_var - 1)
    for v in range(n_var):
        pl.when(need == v)(functools.partial(compressed, (v + 1) * LANES))
    o_cmp = ocmp_ref[...]
    imp = imp_ref[...]

    start = pl.multiple_of(jnp.maximum(t0 + tq - win_keys, 0), tq)
    sw = lax.dot_general(kw_ref[pl.ds(start, win_keys), :], q_r, _NT, preferred_element_type=jnp.float32)
    w_case = jnp.minimum(t0 // tq, 2)
    m_idx = (jnp.where(w_case == 0, 1, jnp.where(w_case == 1, 0, 3)),
             jnp.where(w_case == 0, 2, jnp.where(w_case == 1, 1, 0)),
             jnp.where(w_case == 2, 1, 2))
    sw = jnp.concatenate([sw[j * tq:(j + 1) * tq] + mask_ref[m_idx[j]] for j in range(3)], axis=0)
    mw = jnp.max(sw, axis=0, keepdims=True)
    pw = _bf(jnp.exp2(sw - mw))
    wt0 = start // tq
    ow = jnp.dot(vw_ref[wt0], pw[0:tq], preferred_element_type=jnp.float32)
    for j in range(1, win_keys // tq):
        ow = ow + jnp.dot(vw_ref[wt0 + j], pw[j * tq:(j + 1) * tq], preferred_element_type=jnp.float32)
    o_win = ow / ow[64:65, :]

    blk = lax.broadcasted_iota(jnp.int32, (n_sel, tq), 0)
    blk_f = blk.astype(jnp.float32)
    pos_t = t0 + lax.broadcasted_iota(jnp.int32, (n_sel, tq), 1)
    cur = pos_t >> 6
    forced = (blk == 0) | (blk == cur) | (blk == cur - 1)
    valid = (blk * SEL_LEN) <= pos_t
    val = jnp.where(valid & jnp.logical_not(forced), imp, NEG_INF)
    for _ in range(top_n - 3):
        best = jnp.max(val, axis=0, keepdims=True)
        first = jnp.min(jnp.where(val == best, blk_f, float(n_sel)), axis=0, keepdims=True)
        val = jnp.where(blk_f == first, REMOVED, val)
    bias_t = jnp.where((forced | (val == REMOVED)) & valid, 0.0, NEG_INF)
    bias = _bf(bias_t.T)

    l_sel = jnp.concatenate(
        [jnp.concatenate([bias, q_r[hd * tq:(hd + 1) * tq]], axis=1) for hd in range(NSA_HEADS)], axis=0)
    m_ref[...] = jnp.full((1, rows), NEG_INF, jnp.float32)
    acc_ref[...] = jnp.zeros((V_ROWS, rows), jnp.float32)

    def scores_to(buf, kt, causal):
        s_ref, mx_ref = buf
        k0 = pl.multiple_of(kt * tk, tk)
        sc = lax.dot_general(ks_ref[pl.ds(k0, tk), :], l_sel, _NT, preferred_element_type=jnp.float32)
        if causal:
            own = (t0 - k0) // tq
            lo = sc[0:tq] + mask_ref[1 - own]
            hi = sc[tq:2 * tq] + mask_ref[2 - own]
            s_ref[0:tq, :] = lo
            s_ref[tq:2 * tq, :] = hi
            mx_ref[...] = jnp.maximum(jnp.max(lo, axis=0, keepdims=True), jnp.max(hi, axis=0, keepdims=True))
        else:
            s_ref[...] = sc
            mx_ref[...] = jnp.max(sc, axis=0, keepdims=True)

    def softmax_from(buf, kt):
        s_ref, mx_ref = buf
        sc = s_ref[...]
        m_old = m_ref[...]
        m_new = jnp.maximum(m_old, mx_ref[...])
        alpha = jnp.exp2(m_old - m_new)
        p = jnp.exp2(sc - m_new)
        acc_ref[...] = alpha * acc_ref[...] + jnp.dot(vs_ref[kt], _bf(p), preferred_element_type=jnp.float32)
        m_ref[...] = m_new

    buf0, buf1 = (s0_ref, mx0_ref), (s1_ref, mx1_ref)
    n_full = t0 // tk
    n_pairs = n_full // 2
    scores_to(buf0, n_full, True)

    def pair(j, carry):
        k = 2 * j
        scores_to(buf1, k, False)
        softmax_from(buf0, jnp.where(j == 0, n_full, k - 1))
        scores_to(buf0, k + 1, False)
        softmax_from(buf1, k)
        return carry

    lax.fori_loop(0, n_pairs, pair, 0)
    pending = jnp.where(n_pairs == 0, n_full, 2 * n_pairs - 1)

    @pl.when(n_full % 2 == 1)
    def _():
        scores_to(buf1, n_full - 1, False)
        softmax_from(buf0, pending)
        softmax_from(buf1, n_full - 1)

    @pl.when(n_full % 2 == 0)
    def _():
        softmax_from(buf0, pending)

    acc = acc_ref[...]
    o_slc = acc / acc[64:65, :]

    gate_t = gate_ref[...].T
    y_t = []
    for hd in range(NSA_HEADS):
        c = slice(hd * tq, (hd + 1) * tq)
        y_t.append(gate_t[3 * hd:3 * hd + 1] * o_cmp[0:64, c]
                   + gate_t[3 * hd + 1:3 * hd + 2] * o_slc[0:64, c]
                   + gate_t[3 * hd + 2:3 * hd + 3] * o_win[0:64, c])
    out_ref[:, 0:LANES] = _bf(jnp.concatenate(y_t[0:2], axis=0).T)
    out_ref[:, LANES:2 * LANES] = _bf(jnp.concatenate(y_t[2:4], axis=0).T)


def _nsa(qc, qr, gate, kk, vv, mt, ks, vs, kw, vw):
    bsz, seq, _ = qc.shape
    tq = min(TQ, seq)
    tk = min(TK, seq)
    n_cmp_pad = kk.shape[1]
    rows = NSA_HEADS * tq
    assert tk == 2 * tq and min(WINDOW + tq, seq) == 3 * tq, "mask patterns assume 2 / 3 query-tile-sized key blocks"
    r_idx = lax.broadcasted_iota(jnp.int32, (tq, rows), 0)
    tau = lax.broadcasted_iota(jnp.int32, (tq, rows), 1) & (tq - 1)
    zero = jnp.zeros((tq, rows), jnp.float32)
    masks = jnp.stack([zero, jnp.where(r_idx <= tau, 0.0, NEG_INF), zero + NEG_INF,
                       jnp.where(r_idx > tau, 0.0, NEG_INF)])
    tile = lambda c: pl.BlockSpec((None, tq, c), lambda b, i: (b, i, 0))
    per_b = lambda r, c: pl.BlockSpec((None, r, c), lambda b, i: (b, 0, 0))
    per_b4 = lambda a: pl.BlockSpec((None,) + a.shape[1:], lambda b, i: (b, 0, 0, 0))
    const = lambda a: pl.BlockSpec(a.shape, lambda b, i: (0,) * a.ndim, pipeline_mode=pl.Buffered(1))
    return pl.pallas_call(
        functools.partial(_nsa_kernel, tq=tq, tk=tk, seq=seq),
        grid=(bsz, seq // tq),
        in_specs=[tile(256), tile(256), tile(LANES),
                  per_b(n_cmp_pad, LANES), per_b(V_ROWS, n_cmp_pad), const(mt), const(masks),
                  per_b(seq, 256), per_b4(vs), per_b(seq, LANES), per_b4(vw)],
        out_specs=tile(256),
        out_shape=jax.ShapeDtypeStruct((bsz, seq, 256), MXU_DTYPE),
        scratch_shapes=[pltpu.VMEM((1, rows), jnp.float32), pltpu.VMEM((V_ROWS, rows), jnp.float32),
                        pltpu.VMEM((tk, rows), jnp.float32), pltpu.VMEM((tk, rows), jnp.float32),
                        pltpu.VMEM((1, rows), jnp.float32), pltpu.VMEM((1, rows), jnp.float32),
                        pltpu.VMEM((V_ROWS, rows), jnp.float32), pltpu.VMEM((LANES, tq), jnp.float32)],
        compiler_params=_cparams(2),
        name="nsa",
    )(qc, qr, gate, kk, vv, mt, masks, ks, vs, kw, vw)


def _mix_out_kernel(x_ref, za_ref, zah_ref, zc_ref, zd_ref, zdh_ref, ynsa_ref,
                    cw_ref, cb_ref, clg_ref, clb_ref, slg_ref, slb_ref, sw_ref, sb_ref,
                    pw_ref, ps_ref, wout_ref, gpost_ref, out_ref,
                    uext_ref, ushift_ref, ycat_ref, pext_ref, *, tm):
    i = pl.program_id(1)
    not_first = jnp.where(i > 0, 1.0, 0.0)

    zc = jax.nn.gelu(zc_ref[...], approximate=True)
    u_s = zc[:, :256]
    v_s = _layer_norm(zc[:, 256:], slg_ref[...], slb_ref[...])
    t_i = lax.broadcasted_iota(jnp.int32, (SGU_CHUNK, SGU_GROUPS * SGU_CHUNK), 0)
    s_i = lax.rem(lax.broadcasted_iota(jnp.int32, (SGU_CHUNK, SGU_GROUPS * SGU_CHUNK), 1), SGU_CHUNK)
    w_cat = _bf(jnp.where(s_i <= t_i, sw_ref[...], 0.0))
    grp = lax.broadcasted_iota(jnp.int32, (SGU_CHUNK, 256), 1) >> 6
    ysgu = []
    for c in range(tm // SGU_CHUNK):
        vc = v_s[c * SGU_CHUNK:(c + 1) * SGU_CHUNK]
        stacked = _bf(jnp.concatenate([jnp.where(grp == g, vc, 0.0) for g in range(SGU_GROUPS)], axis=0))
        f = jnp.dot(w_cat, stacked, preferred_element_type=jnp.float32) + sb_ref[...]
        ysgu.append(u_s[c * SGU_CHUNK:(c + 1) * SGU_CHUNK] * f)
    ycat_ref[:, 512:768] = _bf(jnp.concatenate(ysgu, axis=0))

    pext_ref[0:POOL_HALO, :] = zdh_ref[...] * not_first
    pext_ref[POOL_HALO:, :] = zd_ref[...]
    n_ext = tm + POOL_HALO
    sums = []
    for shift in (1, 2, 4, 8):
        n = n_ext - shift
        nxt = pext_ref[pl.ds(shift, n), :] + pext_ref[pl.ds(0, n), :]
        pext_ref[pl.ds(shift, n), :] = nxt
        sums.append(pext_ref[pl.ds(POOL_HALO, tm), :])
    lane_grp = lax.broadcasted_iota(jnp.int32, (tm, 256), 1) >> 6
    pos = i * tm + lax.broadcasted_iota(jnp.int32, (tm, 256), 0)
    wsum = jnp.where(lane_grp == 0, sums[0], jnp.where(lane_grp == 1, sums[1],
                     jnp.where(lane_grp == 2, sums[2], sums[3])))
    width = jnp.where(lane_grp == 0, POOL_WINDOWS[0], jnp.where(lane_grp == 1, POOL_WINDOWS[1],
                      jnp.where(lane_grp == 2, POOL_WINDOWS[2], POOL_WINDOWS[3])))
    cnt = jnp.minimum(pos + 1, width).astype(jnp.float32)
    pooled = wsum / cnt - zd_ref[...]
    ycat_ref[:, 768:1024] = _bf(jnp.dot(_bf(pooled), pw_ref[...], preferred_element_type=jnp.float32) * ps_ref[...])

    ycat_ref[:, 256:512] = ynsa_ref[...]
    y_rest = (jnp.dot(ycat_ref[:, 256:512], wout_ref[256:512, :], preferred_element_type=jnp.float32)
              + jnp.dot(ycat_ref[:, 512:768], wout_ref[512:768, :], preferred_element_type=jnp.float32)
              + jnp.dot(ycat_ref[:, 768:1024], wout_ref[768:1024, :], preferred_element_type=jnp.float32))

    def glu(z):
        return z[:, :256] * jax.nn.sigmoid(z[:, 256:])

    uext_ref[0:CONV_HALO, :] = glu(zah_ref[...]) * not_first
    uext_ref[CONV_HALO:, :] = glu(za_ref[...])
    first = CONV_HALO - (CONV_K - 1)
    n_shift = tm + CONV_HALO - SUBLANES
    for r in range(SUBLANES):
        n = n_shift if first + r + n_shift <= tm + CONV_HALO else n_shift - SUBLANES
        ushift_ref[r, 0:n, :] = uext_ref[pl.ds(first + r, n), :]
    rc = 64
    for r0 in range(0, tm, rc):
        acc = jnp.zeros((rc, 256), jnp.float32)
        for k in range(CONV_K):
            acc = acc + cw_ref[k:k + 1, :] * ushift_ref[k % SUBLANES, pl.ds(r0 + k - k % SUBLANES, rc), :]
        yc = _layer_norm(acc + cb_ref[...], clg_ref[...], clb_ref[...])
        ycat_ref[r0:r0 + rc, 0:256] = _bf(yc * jax.nn.sigmoid(yc))

    y = y_rest + jnp.dot(ycat_ref[:, 0:256], wout_ref[0:256, :], preferred_element_type=jnp.float32)
    out_ref[...] = x_ref[...] + _rms(y, gpost_ref[...])


def _mix_out(l, x, za, zc, zd, ynsa, cw, cb, clg, clb, slg, slb, sw_cat, sb_exp, pw_bd, ps, wout, gpost):
    bsz, seq, d = x.shape
    tm = min(TM_MIX, seq)
    tok = lambda c: pl.BlockSpec((None, tm, c), lambda b, i: (b, i, 0))
    halo = lambda c, h: pl.BlockSpec((None, h, c), lambda b, i: (b, jnp.maximum(i * (tm // h) - 1, 0), 0))
    weights = [cw, cb, clg, clb, slg, slb, sw_cat, sb_exp, pw_bd, ps, wout, gpost]
    return pl.pallas_call(
        functools.partial(_mix_out_kernel, tm=tm),
        grid=(bsz, seq // tm),
        in_specs=[tok(d), tok(512), halo(512, CONV_HALO), tok(512), tok(256), halo(256, POOL_HALO), tok(256)]
                 + [_layer_spec(w, l) for w in weights],
        out_specs=tok(d),
        out_shape=jax.ShapeDtypeStruct((bsz, seq, d), jnp.float32),
        scratch_shapes=[pltpu.VMEM((tm + CONV_HALO, 256), jnp.float32),
                        pltpu.VMEM((SUBLANES, tm + CONV_HALO - SUBLANES, 256), jnp.float32),
                        pltpu.VMEM((tm, 4 * 256), MXU_DTYPE),
                        pltpu.VMEM((tm + POOL_HALO, 256), jnp.float32)],
        compiler_params=_cparams(2),
        name="mix_out",
    )(x, za, za, zc, zd, zd, ynsa, *weights)


def _ffn_kernel(x_ref, xh_ref, gpre_ref, wup_ref, cw_ref, cb_ref, wdn_ref, gpost_ref, out_ref,
                hext_ref, g0_ref, u0_ref, g1_ref, u1_ref, acc_ref, *, tm, n_chunks, chunk):
    i = pl.program_id(1)
    not_first = jnp.where(i > 0, 1.0, 0.0)
    x = x_ref[...]
    hext_ref[0:FFN_HALO, :] = _bf(_rms(xh_ref[...], gpre_ref[...]) * not_first)
    hext_ref[FFN_HALO:, :] = _bf(_rms(x, gpre_ref[...]))

    def cols(c):
        return pl.ds(pl.multiple_of(c * chunk, chunk), chunk)

    def up_to(g_ref, u_ref, c):
        h_ext = hext_ref[...]
        g_ref[...] = jnp.dot(h_ext, wup_ref[:, cols(c)], preferred_element_type=jnp.float32)
        u_ref[...] = jnp.dot(h_ext, wup_ref[:, cols(n_chunks + c)], preferred_element_type=jnp.float32)

    def conv(ext_ref, c):
        w = cw_ref[:, cols(c)]
        return (w[0:1] * ext_ref[pl.ds(FFN_HALO - 2, tm), :] + w[1:2] * ext_ref[pl.ds(FFN_HALO - 1, tm), :]
                + w[2:3] * ext_ref[pl.ds(FFN_HALO, tm), :] + cb_ref[:, cols(c)])

    def down_from(g_ref, u_ref, c):
        act = jax.nn.gelu(conv(g_ref, c), approximate=True) * conv(u_ref, n_chunks + c)
        acc_ref[...] += jnp.dot(_bf(act), wdn_ref[c], preferred_element_type=jnp.float32)

    assert n_chunks % 2 == 1
    acc_ref[...] = jnp.zeros_like(acc_ref)
    up_to(g0_ref, u0_ref, 0)

    def pair(j, carry):
        c = 2 * j
        up_to(g1_ref, u1_ref, c + 1)
        down_from(g0_ref, u0_ref, c)
        up_to(g0_ref, u0_ref, c + 2)
        down_from(g1_ref, u1_ref, c + 1)
        return carry

    lax.fori_loop(0, (n_chunks - 1) // 2, pair, 0)
    down_from(g0_ref, u0_ref, n_chunks - 1)
    out_ref[...] = x + _rms(acc_ref[...], gpost_ref[...])


def _ffn(l, x, gpre, wup, cw, cb, wdn, gpost):
    bsz, seq, d = x.shape
    tm = min(TM_FFN, seq)
    n_chunks = wdn.shape[1]
    chunk = wdn.shape[2]
    tok = pl.BlockSpec((None, tm, d), lambda b, i: (b, i, 0))
    halo = pl.BlockSpec((None, FFN_HALO, d), lambda b, i: (b, jnp.maximum(i * (tm // FFN_HALO) - 1, 0), 0))
    weights = [gpre, wup, cw, cb, wdn, gpost]
    return pl.pallas_call(
        functools.partial(_ffn_kernel, tm=tm, n_chunks=n_chunks, chunk=chunk),
        grid=(bsz, seq // tm),
        in_specs=[tok, halo] + [_layer_spec(w, l) for w in weights],
        out_specs=tok,
        out_shape=jax.ShapeDtypeStruct((bsz, seq, d), jnp.float32),
        scratch_shapes=[pltpu.VMEM((tm + FFN_HALO, d), MXU_DTYPE)]
                       + [pltpu.VMEM((tm + FFN_HALO, chunk), jnp.float32)] * 4
                       + [pltpu.VMEM((tm, d), jnp.float32)],
        compiler_params=_cparams(2),
        name="ffn",
    )(x, x, *weights)


def _prep_w_in(w):
    n_in = w.shape[-1]
    n_pad = -(-n_in // 256) * 256
    return jnp.pad(_bf(w), ((0, 0), (0, 0), (0, n_pad - n_in)))


def _prep_compress(pe_k, pe_v, ck_w1, ck_w2, cv_w1, cv_w2):
    half = CMP_LEN // 2

    def w1_part(w1k, w1v, lo):
        wk = w1k.reshape(CMP_LEN, HEAD_DIM, HEAD_DIM)[lo:lo + half]
        wv = w1v.reshape(CMP_LEN, HEAD_DIM, HEAD_DIM)[lo:lo + half]
        z = jnp.zeros_like(wk)
        top = jnp.concatenate([wk, z], axis=-1)
        bot = jnp.concatenate([z, wv], axis=-1)
        return jnp.concatenate([top, bot], axis=1).reshape(half * 2 * HEAD_DIM, 2 * HEAD_DIM)

    def pe_part(lo):
        return jnp.concatenate([pe_k[lo:lo + half], pe_v[lo:lo + half]], axis=1).reshape(1, half * 2 * HEAD_DIM)

    z = jnp.zeros_like(ck_w2)
    w2 = jnp.concatenate([jnp.concatenate([ck_w2, z], axis=1), jnp.concatenate([z, cv_w2], axis=1)], axis=0)
    return (pe_part(0), pe_part(half), _bf(w1_part(ck_w1, cv_w1, 0)), _bf(w1_part(ck_w1, cv_w1, half)), _bf(w2))


def _importance_matrix(seq):
    n_cmp = (seq - CMP_LEN) // CMP_STRIDE + 1
    n_sel = seq // SEL_LEN
    ratio = SEL_LEN // CMP_STRIDE
    n_ov = CMP_LEN // CMP_STRIDE
    n_pad = -(-n_cmp // LANES) * LANES
    assert n_sel <= LANES, "selection blocks must fit one lane group"
    assert min(SEL_TOPK, n_sel) > 3, "top-k must exceed the three forced blocks"
    mt = np.zeros((LANES, n_pad), np.float32)
    for j in range(n_sel):
        for m in range(ratio):
            for n in range(n_ov):
                c = ratio * j + m - n
                if 0 <= c < n_cmp:
                    mt[j, c] += 1.0
    return jnp.asarray(mt, MXU_DTYPE)


def _rope_table(seq):
    half = HEAD_DIM // 2
    per_row = LANES // half
    inv = ROPE_THETA ** (-jnp.arange(half, dtype=jnp.float32) * 2.0 / HEAD_DIM)
    pos = jnp.arange(seq, dtype=jnp.float32).reshape(seq // per_row, per_row, 1)
    ang = (pos * inv[None, None, :]).reshape(seq // per_row, LANES)
    cos, sin = jnp.cos(ang).reshape(seq, half), jnp.sin(ang).reshape(seq, half)
    return jnp.concatenate([cos, cos, sin, sin], axis=1)


def _block_diag(w):
    g, c, d = w.shape
    out = jnp.zeros((g * c, g * d), w.dtype)
    for j in range(g):
        out = out.at[j * c:(j + 1) * c, j * d:(j + 1) * d].set(w[j])
    return out


def kernel(x, norm_mix_pre, norm_mix_post, norm_ffn_pre, norm_ffn_post, w_in, w_out, conv_dw_w, conv_dw_b, conv_ln_g, conv_ln_b, nsa_pe_k, nsa_pe_v, nsa_ck_w1, nsa_ck_w2, nsa_cv_w1, nsa_cv_w2, sgu_ln_g, sgu_ln_b, sgu_w, sgu_b, pool_w, pool_scale, ffn_up, ffn_conv_w, ffn_conv_b, ffn_down):
    bsz, seq, d = x.shape
    depth = w_in.shape[0]
    ffn_dim = ffn_down.shape[1]
    cs = _rope_table(seq)
    mt = _importance_matrix(seq)
    rows = lambda v: v[:, None, :]

    w_ext = _prep_w_in(w_in)
    cmp_w = jax.vmap(_prep_compress)(nsa_pe_k, nsa_pe_v, nsa_ck_w1, nsa_ck_w2, nsa_cv_w1, nsa_cv_w2)
    sw_cat = jnp.transpose(sgu_w, (0, 2, 1, 3)).reshape(depth, SGU_CHUNK, SGU_GROUPS * SGU_CHUNK)
    sb_exp = jnp.repeat(jnp.swapaxes(sgu_b, 1, 2), 256 // SGU_GROUPS, axis=2)
    mix_w = [conv_dw_w, rows(conv_dw_b), rows(conv_ln_g), rows(conv_ln_b), rows(sgu_ln_g), rows(sgu_ln_b),
             sw_cat, sb_exp, _bf(jax.vmap(_block_diag)(pool_w)), rows(pool_scale), _bf(w_out), rows(norm_mix_post)]
    ffn_w = [rows(norm_ffn_pre), _bf(ffn_up), ffn_conv_w, rows(ffn_conv_b),
             _bf(ffn_down).reshape(depth, ffn_dim // FFN_CHUNK, FFN_CHUNK, d), rows(norm_ffn_post)]
    g_pre = rows(norm_mix_pre)

    for l in range(depth):
        za, zc, zd, qc, qr, a_cmp, ks, vs, kw, vw, gate = _in_proj(l, x, g_pre, w_ext, cs)
        kk, vv = _compress(l, a_cmp, *cmp_w)
        y_nsa = _nsa(qc, qr, gate, kk, vv, mt, ks, vs, kw, vw)
        x = _mix_out(l, x, za, zc, zd, y_nsa, *mix_w)
        x = _ffn(l, x, *ffn_w)
    return x
```

```python
import functools

import numpy as np
import jax
import jax.numpy as jnp
from jax import lax
from jax.experimental import pallas as pl
from jax.experimental.pallas import tpu as pltpu

HEAD_DIM = 64
NSA_HEADS = 4
CONV_K = 31
CMP_LEN = 32
CMP_STRIDE = 16
SEL_LEN = 64
SEL_TOPK = 16
WINDOW = 512
SGU_CHUNK = 128
SGU_GROUPS = 4
POOL_WINDOWS = (2, 4, 8, 16)
FFN_CONV_K = 3
ROPE_THETA = 10000.0
RMS_EPS = 1e-6
LN_EPS = 1e-5
NEG_INF = -1e30
FORCE_SCORE = 1e6
REMOVED = -3e38
LOG2E = 1.4426950408889634

LANES = 128
SUBLANES = 8
V_ROWS = 80
MXU_DTYPE = jnp.bfloat16
VMEM_LIMIT = 56 * 1024 * 1024

TM_IN = 1024
TM_MIX = 512
TM_FFN = 1024
TQ = 256
TK = 512
FFN_CHUNK = 256
CONV_HALO = 32
POOL_HALO = 16
FFN_HALO = 16

_NT = (((1,), (1,)), ((), ()))


def _cparams(n_axes, flags=None):
    return pltpu.CompilerParams(dimension_semantics=("arbitrary",) * n_axes,
                                vmem_limit_bytes=VMEM_LIMIT, flags=flags)


def _layer_spec(a, l):
    return pl.BlockSpec((None,) + a.shape[1:], lambda *_: (l,) + (0,) * (a.ndim - 1),
                        pipeline_mode=pl.Buffered(1))


def _rms(x, g):
    return x * lax.rsqrt(jnp.mean(x * x, axis=-1, keepdims=True) + RMS_EPS) * g


def _layer_norm(x, g, b):
    mu = jnp.mean(x, axis=-1, keepdims=True)
    var = jnp.mean(jnp.square(x - mu), axis=-1, keepdims=True)
    return (x - mu) * lax.rsqrt(var + LN_EPS) * g + b


def _bf(x):
    return x.astype(MXU_DTYPE)


def _swap64(x):
    return pltpu.roll(x, 64, axis=1)


def _in_proj_kernel(x_ref, g_ref, w_ref, cs_ref,
                    za_ref, zc_ref, zd_ref, qc_ref, qr_ref, kv_ref,
                    ks_ref, vs_ref, kw_ref, vw_ref, gate_ref, kvt_ref, *, tm, tk):
    i = pl.program_id(1)
    h = _bf(_rms(x_ref[...], g_ref[...]))

    def proj(lo, hi):
        return jnp.dot(h, w_ref[:, lo:hi], preferred_element_type=jnp.float32)

    lane = lax.broadcasted_iota(jnp.int32, (tm, LANES), 1)
    q = proj(512, 768)
    kv01 = proj(768, 1024)
    kv2t = proj(1024, 1280)
    kvt_ref[...] = kv01[:, :LANES]
    group = CMP_LEN // 2
    for j in range(group):
        kv_ref[:, j * LANES:(j + 1) * LANES] = kvt_ref[pl.ds(j, tm // group, stride=group), :]
    kvs, kvw, tail0 = kv01[:, LANES:], kv2t[:, :LANES], kv2t[:, LANES:]
    kx = jnp.where(lane < 64, kvs, _swap64(kvw))
    vsw = jnp.where(lane < 64, _swap64(kvs), kvw)
    gate_ref[...] = jax.nn.sigmoid(tail0)

    cs = cs_ref[...]
    sc = _swap64(cs)
    cos2 = jnp.where(lane < 64, cs, sc)
    sin2 = jnp.where(lane < 64, sc, cs)
    first_half = (lane & (HEAD_DIM - 1)) < HEAD_DIM // 2

    def rope(x):
        swapped = jnp.where(first_half, -pltpu.roll(x, LANES - HEAD_DIM // 2, axis=1),
                            pltpu.roll(x, HEAD_DIM // 2, axis=1))
        return x * cos2 + swapped * sin2

    scale = HEAD_DIM ** -0.5 * LOG2E
    qc_ref[...] = _bf(q * scale)
    for pair in range(NSA_HEADS // 2):
        c = slice(pair * LANES, (pair + 1) * LANES)
        qr_ref[:, c] = _bf(rope(q[:, c]) * scale)

    kx_rot = rope(kx)
    kx_swap = _swap64(kx_rot)
    k_rot2 = jnp.where(lane < 64, kx_rot, kx_swap)
    kw_rot2 = jnp.where(lane < 64, kx_swap, kx_rot)
    pos = i * tm + lax.broadcasted_iota(jnp.int32, (tm, LANES), 0)
    onehot =jnp.where((pos >> 6) == lane, 1.0, 0.0)
    ks_ref[:, 0:LANES] = _bf(onehot)
    ks_ref[:, LANES:2 * LANES] = _bf(k_rot2)
    kw_ref[...] = _bf(kw_rot2)
    ones_col = jnp.where(lane == 64, 1.0, 0.0)
    vs = jnp.where(lane < 64, vsw, ones_col)
    vw = jnp.where(lane < 64, _swap64(vsw), ones_col)
    for j in range(tm // tk):
        vs_ref[j] = _bf(vs[j * tk:(j + 1) * tk].T[0:V_ROWS])
    for j in range(tm // TQ):
        vw_ref[j] = _bf(vw[j * TQ:(j + 1) * TQ].T[0:V_ROWS])

    tail = jnp.concatenate([tail0, proj(1280, 2048)], axis=1)
    n_gate = 3 * NSA_HEADS
    zc_ref[...] = tail[:, n_gate:n_gate + 512]
    zd_ref[...] = tail[:, n_gate + 512:n_gate + 768]
    za_ref[...] = proj(0, 512)


def _in_proj(l, x, g, w_ext, cs):
    bsz, seq, d = x.shape
    tm = min(TM_IN, seq)
    tk = min(TK, seq)
    tok = lambda c: pl.BlockSpec((None, tm, c), lambda b, i: (b, i, 0))
    f32, bf16 = jnp.float32, MXU_DTYPE

    def tok_out(c, dt):
        return tok(c), jax.ShapeDtypeStruct((bsz, seq, c), dt)

    def keytile_out(t):
        return (pl.BlockSpec((None, tm // t, V_ROWS, t), lambda b, i: (b, i, 0, 0)),
                jax.ShapeDtypeStruct((bsz, seq // t, V_ROWS, t), bf16))

    group = CMP_LEN // 2
    cmp_out = (pl.BlockSpec((None, tm // group, group * LANES), lambda b, i: (b, i, 0)),
               jax.ShapeDtypeStruct((bsz, seq // group, group * LANES), f32))
    outs = [tok_out(512, f32), tok_out(512, f32), tok_out(256, f32), tok_out(256, bf16), tok_out(256, bf16),
            cmp_out, tok_out(256, bf16), keytile_out(tk), tok_out(128, bf16), keytile_out(TQ),
            tok_out(128, f32)]
    return pl.pallas_call(
        functools.partial(_in_proj_kernel, tm=tm, tk=tk),
        grid=(bsz, seq // tm),
        in_specs=[tok(d), _layer_spec(g, l), _layer_spec(w_ext, l),
                  pl.BlockSpec((tm, LANES), lambda b, i: (i, 0))],
        out_specs=[spec for spec, _ in outs],
        out_shape=[shape for _, shape in outs],
        scratch_shapes=[pltpu.VMEM((tm, LANES), f32)],
        compiler_params=_cparams(2),
        name="in_proj",
    )(x, g, w_ext, cs)


def _compress_kernel(a_ref, pe_lo_ref, pe_hi_ref, w_lo_ref, w_hi_ref, w2_ref, kk_ref, vv_ref):
    a = a_ref[...]
    u = jnp.dot(_bf(a + pe_lo_ref[...]), w_lo_ref[...], preferred_element_type=jnp.float32)
    v = jnp.dot(_bf(a + pe_hi_ref[...]), w_hi_ref[...], preferred_element_type=jnp.float32)
    n = a.shape[0]
    pre = u + pltpu.roll(v, n - 1, axis=0)
    act = jax.nn.gelu(pre, approximate=True)
    kv = jnp.dot(_bf(act), w2_ref[...], preferred_element_type=jnp.float32)
    vk = _swap64(kv)
    lane = lax.broadcasted_iota(jnp.int32, (n, LANES), 1)
    kk_ref[...] = _bf(jnp.where(lane < 64, kv, vk))
    vv = jnp.where(lane < 64, vk, jnp.where(lane == 64, 1.0, 0.0))
    vv_ref[...] = _bf(vv.T[0:V_ROWS])


def _compress(l, a, pe_lo, pe_hi, w_lo, w_hi, w2):
    bsz, n, width = a.shape
    per_b = lambda c: pl.BlockSpec((None, n, c), lambda b: (b, 0, 0))
    return pl.pallas_call(
        _compress_kernel,
        grid=(bsz,),
        in_specs=[per_b(width)] + [_layer_spec(w, l) for w in (pe_lo, pe_hi, w_lo, w_hi, w2)],
        out_specs=[per_b(LANES), pl.BlockSpec((None, V_ROWS, n), lambda b: (b, 0, 0))],
        out_shape=[jax.ShapeDtypeStruct((bsz, n, LANES), MXU_DTYPE),
                   jax.ShapeDtypeStruct((bsz, V_ROWS, n), MXU_DTYPE)],
        compiler_params=_cparams(1),
        name="compress",
    )(a, pe_lo, pe_hi, w_lo, w_hi, w2)


def _nsa_kernel(qc_ref, qr_ref, gate_ref, kk_ref, vv_ref, mt_ref, mask_ref, ks_ref, vs_ref, kw_ref, vw_ref,
                out_ref, m_ref, acc_ref, s0_ref, s1_ref, mx0_ref, mx1_ref, ocmp_ref, imp_ref, *, tq, tk, seq):
    qi = pl.program_id(1)
    t0 = qi * tq
    rows = NSA_HEADS * tq
    n_cmp_pad = kk_ref.shape[0]
    top_n = min(SEL_TOPK, seq // SEL_LEN)
    n_sel = LANES
    win_keys = min(WINDOW + tq, seq)

    lane =lax.broadcasted_iota(jnp.int32, (tq, LANES), 1)

    def head_rows(q_ref):
        parts = []
        for hd in range(NSA_HEADS):
            pair = q_ref[:, (hd // 2) * LANES:(hd // 2 + 1) * LANES]
            keep = (lane < 64) if hd % 2 == 0 else (lane >= 64)
            parts.append(jnp.where(keep, pair, jnp.zeros_like(pair)))
        return jnp.concatenate(parts, axis=0)

    q_c = head_rows(qc_ref)
    q_r = head_rows(qr_ref)
    pos_l = t0 + (lax.broadcasted_iota(jnp.int32, (1, rows), 1) & (tq - 1))

    last_blk = (pos_l - (CMP_LEN - 1)) >> 4

    def compressed(nr):
        s = lax.dot_general(kk_ref[0:nr, :], q_c, _NT, preferred_element_type=jnp.float32)
        n_idx = lax.broadcasted_iota(jnp.int32, (nr, rows), 0)
        sm = jnp.where(n_idx <= last_blk, s, NEG_INF)
        mx = jnp.max(sm, axis=0, keepdims=True)
        e = jnp.exp2(sm - mx)
        den = jnp.sum(e, axis=0, keepdims=True)
        p_cmp = e * jnp.where(last_blk >= 0, 1.0 / den, 0.0)
        ocmp_ref[...] = jnp.dot(vv_ref[:, 0:nr], _bf(p_cmp), preferred_element_type=jnp.float32)
        p_sum = p_cmp[:, 0:tq] + p_cmp[:, tq:2 * tq] + p_cmp[:, 2 * tq:3 * tq] + p_cmp[:, 3 * tq:4 * tq]
        hi = _bf(p_sum)
        r1 = p_sum - hi.astype(jnp.float32)
        mid = _bf(r1)
        lo = _bf(r1 - mid.astype(jnp.float32))
        mt = mt_ref[:, 0:nr]
        imp_ref[...] = (jnp.dot(mt, hi, preferred_element_type=jnp.float32)
                        + jnp.dot(mt, mid, preferred_element_type=jnp.float32)
                        + jnp.dot(mt, lo, preferred_element_type=jnp.float32))

    n_var = n_cmp_pad // LANES
    tile_last = jnp.maximum((t0 + tq - CMP_LEN) >> 4, 0)
    need = jnp.minimum(tile_last // LANES, n_var - 1)
    for v in range(n_var):
        pl.when(need == v)(functools.partial(compressed, (v + 1) * LANES))
    o_cmp = ocmp_ref[...]
    imp = imp_ref[...]

    start = pl.multiple_of(jnp.maximum(t0 + tq - win_keys, 0), tq)
    sw = lax.dot_general(kw_ref[pl.ds(start, win_keys), :], q_r, _NT, preferred_element_type=jnp.float32)
    w_case = jnp.minimum(t0 // tq, 2)
    m_idx = (jnp.where(w_case == 0, 1, jnp.where(w_case == 1, 0, 3)),
             jnp.where(w_case == 0, 2, jnp.where(w_case == 1, 1, 0)),
             jnp.where(w_case == 2, 1, 2))
    sw = jnp.concatenate([sw[j * tq:(j + 1) * tq] + mask_ref[m_idx[j]] for j in range(3)], axis=0)
    mw = jnp.max(sw, axis=0, keepdims=True)
    pw = _bf(jnp.exp2(sw - mw))
    wt0 = start // tq
    ow = jnp.dot(vw_ref[wt0], pw[0:tq], preferred_element_type=jnp.float32)
    for j in range(1, win_keys // tq):
        ow = ow + jnp.dot(vw_ref[wt0 + j], pw[j * tq:(j + 1) * tq], preferred_element_type=jnp.float32)
    o_win = ow / ow[64:65, :]

    blk = lax.broadcasted_iota(jnp.int32, (n_sel, tq), 0)
    blk_f = blk.astype(jnp.float32)
    pos_t = t0 + lax.broadcasted_iota(jnp.int32, (n_sel, tq), 1)
    cur = pos_t >> 6
    forced = (blk == 0) | (blk == cur) | (blk == cur - 1)
    valid = (blk * SEL_LEN) <= pos_t
    val = jnp.where(valid & jnp.logical_not(forced), imp, NEG_INF)
    for _ in range(top_n - 3):
        best = jnp.max(val, axis=0, keepdims=True)
        first = jnp.min(jnp.where(val == best, blk_f, float(n_sel)), axis=0, keepdims=True)
        val = jnp.where(blk_f == first, REMOVED, val)
    bias_t = jnp.where((forced | (val == REMOVED)) & valid, 0.0, NEG_INF)
    bias = _bf(bias_t.T)

    l_sel = jnp.concatenate(
        [jnp.concatenate([bias, q_r[hd * tq:(hd + 1) * tq]], axis=1) for hd in range(NSA_HEADS)], axis=0)
    m_ref[...] = jnp.full((1, rows), NEG_INF, jnp.float32)
    acc_ref[...] = jnp.zeros((V_ROWS, rows), jnp.float32)

    def scores_to(buf, kt, causal):
        s_ref, mx_ref = buf
        k0 = pl.multiple_of(kt * tk, tk)
        sc = lax.dot_general(ks_ref[pl.ds(k0, tk), :], l_sel, _NT, preferred_element_type=jnp.float32)
        if causal:
            own = (t0 - k0) // tq
            lo = sc[0:tq] + mask_ref[1 - own]
            hi = sc[tq:2 * tq] + mask_ref[2 - own]
            s_ref[0:tq, :] = lo
            s_ref[tq:2 * tq, :] = hi
            mx_ref[...] = jnp.maximum(jnp.max(lo, axis=0, keepdims=True), jnp.max(hi, axis=0, keepdims=True))
        else:
            s_ref[...] = sc
            mx_ref[...] = jnp.max(sc, axis=0, keepdims=True)

    def softmax_from(buf, kt):
        s_ref, mx_ref = buf
        sc = s_ref[...]
        m_old = m_ref[...]
        m_new = jnp.maximum(m_old, mx_ref[...])
        alpha = jnp.exp2(m_old - m_new)
        p = jnp.exp2(sc - m_new)
        acc_ref[...] = alpha * acc_ref[...] + jnp.dot(vs_ref[kt], _bf(p), preferred_element_type=jnp.float32)
        m_ref[...] = m_new

    buf0, buf1 = (s0_ref, mx0_ref), (s1_ref, mx1_ref)
    n_full = t0 // tk
    n_pairs = n_full // 2
    scores_to(buf0, n_full, True)

    def pair(j, carry):
        k = 2 * j
        scores_to(buf1, k, False)
        softmax_from(buf0, jnp.where(j == 0, n_full, k - 1))
        scores_to(buf0, k + 1, False)
        softmax_from(buf1, k)
        return carry

    lax.fori_loop(0, n_pairs, pair, 0)
    pending = jnp.where(n_pairs == 0, n_full, 2 * n_pairs - 1)

    @pl.when(n_full % 2 == 1)
    def _():
        scores_to(buf1, n_full - 1, False)
        softmax_from(buf0, pending)
        softmax_from(buf1, n_full - 1)

    @pl.when(n_full % 2 == 0)
    def _():
        softmax_from(buf0, pending)

    acc = acc_ref[...]
    o_slc = acc / acc[64:65, :]

    gate_t = gate_ref[...].T
    y_t = []
    for hd in range(NSA_HEADS):
        c = slice(hd * tq, (hd + 1) * tq)
        y_t.append(gate_t[3 * hd:3 * hd + 1] * o_cmp[0:64, c]
                   + gate_t[3 * hd + 1:3 * hd + 2] * o_slc[0:64, c]
                   + gate_t[3 * hd + 2:3 * hd + 3] * o_win[0:64, c])
    out_ref[:, 0:LANES] = _bf(jnp.concatenate(y_t[0:2], axis=0).T)
    out_ref[:, LANES:2 * LANES] = _bf(jnp.concatenate(y_t[2:4], axis=0).T)


def _nsa(qc, qr, gate, kk, vv, mt, ks, vs, kw, vw):
    bsz, seq, _ = qc.shape
    tq = min(TQ, seq)
    tk = min(TK, seq)
    n_cmp_pad = kk.shape[1]
    rows = NSA_HEADS * tq
    assert tk == 2 * tq and min(WINDOW + tq, seq) == 3 * tq, "mask patterns assume 2 / 3 query-tile-sized key blocks"
    r_idx = lax.broadcasted_iota(jnp.int32, (tq, rows), 0)
    tau = lax.broadcasted_iota(jnp.int32, (tq, rows), 1) & (tq - 1)
    zero = jnp.zeros((tq, rows), jnp.float32)
    masks = jnp.stack([zero, jnp.where(r_idx <= tau, 0.0, NEG_INF), zero + NEG_INF,
                       jnp.where(r_idx > tau, 0.0, NEG_INF)])
    tile = lambda c: pl.BlockSpec((None, tq, c), lambda b, i: (b, i, 0))
    per_b = lambda r, c: pl.BlockSpec((None, r, c), lambda b, i: (b, 0, 0))
    per_b4 = lambda a: pl.BlockSpec((None,) + a.shape[1:], lambda b, i: (b, 0, 0, 0))
    const = lambda a: pl.BlockSpec(a.shape, lambda b, i: (0,) * a.ndim, pipeline_mode=pl.Buffered(1))
    return pl.pallas_call(
        functools.partial(_nsa_kernel, tq=tq, tk=tk, seq=seq),
        grid=(bsz, seq // tq),
        in_specs=[tile(256), tile(256), tile(LANES),
                  per_b(n_cmp_pad, LANES), per_b(V_ROWS, n_cmp_pad), const(mt), const(masks),
                  per_b(seq, 256), per_b4(vs), per_b(seq, LANES), per_b4(vw)],
        out_specs=tile(256),
        out_shape=jax.ShapeDtypeStruct((bsz, seq, 256), MXU_DTYPE),
        scratch_shapes=[pltpu.VMEM((1, rows), jnp.float32), pltpu.VMEM((V_ROWS, rows), jnp.float32),
                        pltpu.VMEM((tk, rows), jnp.float32), pltpu.VMEM((tk, rows), jnp.float32),
                        pltpu.VMEM((1, rows), jnp.float32), pltpu.VMEM((1, rows), jnp.float32),
                        pltpu.VMEM((V_ROWS, rows), jnp.float32), pltpu.VMEM((LANES, tq), jnp.float32)],
        compiler_params=_cparams(2),
        name="nsa",
    )(qc, qr, gate, kk, vv, mt, masks, ks, vs, kw, vw)


def _mix_out_kernel(x_ref, za_ref, zah_ref, zc_ref, zd_ref, zdh_ref, ynsa_ref,
                    cw_ref, cb_ref, clg_ref, clb_ref, slg_ref, slb_ref, sw_ref, sb_ref,
                    pw_ref, ps_ref, wout_ref, gpost_ref, out_ref,
                    uext_ref, ushift_ref, ycat_ref, pext_ref, *, tm):
    i = pl.program_id(1)
    not_first = jnp.where(i > 0, 1.0, 0.0)

    zc = jax.nn.gelu(zc_ref[...], approximate=True)
    u_s = zc[:, :256]
    v_s = _layer_norm(zc[:, 256:], slg_ref[...], slb_ref[...])
    t_i = lax.broadcasted_iota(jnp.int32, (SGU_CHUNK, SGU_GROUPS * SGU_CHUNK), 0)
    s_i = lax.rem(lax.broadcasted_iota(jnp.int32, (SGU_CHUNK, SGU_GROUPS * SGU_CHUNK), 1), SGU_CHUNK)
    w_cat = _bf(jnp.where(s_i <= t_i, sw_ref[...], 0.0))
    grp = lax.broadcasted_iota(jnp.int32, (SGU_CHUNK, 256), 1) >> 6
    ysgu = []
    for c in range(tm // SGU_CHUNK):
        vc = v_s[c * SGU_CHUNK:(c + 1) * SGU_CHUNK]
        stacked = _bf(jnp.concatenate([jnp.where(grp == g, vc, 0.0) for g in range(SGU_GROUPS)], axis=0))
        f = jnp.dot(w_cat, stacked, preferred_element_type=jnp.float32) + sb_ref[...]
        ysgu.append(u_s[c * SGU_CHUNK:(c + 1) * SGU_CHUNK] * f)
    ycat_ref[:, 512:768] = _bf(jnp.concatenate(ysgu, axis=0))

    pext_ref[0:POOL_HALO, :] = zdh_ref[...] * not_first
    pext_ref[POOL_HALO:, :] = zd_ref[...]
    n_ext = tm + POOL_HALO
    sums = []
    for shift in (1, 2, 4, 8):
        n = n_ext - shift
        nxt = pext_ref[pl.ds(shift, n), :] + pext_ref[pl.ds(0, n), :]
        pext_ref[pl.ds(shift, n), :] = nxt
        sums.append(pext_ref[pl.ds(POOL_HALO, tm), :])
    lane_grp = lax.broadcasted_iota(jnp.int32, (tm, 256), 1) >> 6
    pos = i * tm + lax.broadcasted_iota(jnp.int32, (tm, 256), 0)
    wsum = jnp.where(lane_grp == 0, sums[0], jnp.where(lane_grp == 1, sums[1],
                     jnp.where(lane_grp == 2, sums[2], sums[3])))
    width = jnp.where(lane_grp == 0, POOL_WINDOWS[0], jnp.where(lane_grp == 1, POOL_WINDOWS[1],
                      jnp.where(lane_grp == 2, POOL_WINDOWS[2], POOL_WINDOWS[3])))
    cnt = jnp.minimum(pos + 1, width).astype(jnp.float32)
    pooled = wsum / cnt - zd_ref[...]
    ycat_ref[:, 768:1024] = _bf(jnp.dot(_bf(pooled), pw_ref[...], preferred_element_type=jnp.float32) * ps_ref[...])

    ycat_ref[:, 256:512] = ynsa_ref[...]
    y_rest = (jnp.dot(ycat_ref[:, 256:512], wout_ref[256:512, :], preferred_element_type=jnp.float32)
              + jnp.dot(ycat_ref[:, 512:768], wout_ref[512:768, :], preferred_element_type=jnp.float32)
              + jnp.dot(ycat_ref[:, 768:1024], wout_ref[768:1024, :], preferred_element_type=jnp.float32))

    def glu(z):
        return z[:, :256] * jax.nn.sigmoid(z[:, 256:])

    uext_ref[0:CONV_HALO, :] = glu(zah_ref[...]) * not_first
    uext_ref[CONV_HALO:, :] = glu(za_ref[...])
    first = CONV_HALO - (CONV_K - 1)
    n_shift = tm + CONV_HALO - SUBLANES
    for r in range(SUBLANES):
        n = n_shift if first + r + n_shift <= tm + CONV_HALO else n_shift - SUBLANES
        ushift_ref[r, 0:n, :] = uext_ref[pl.ds(first + r, n), :]
    rc = 64
    for r0 in range(0, tm, rc):
        acc = jnp.zeros((rc, 256), jnp.float32)
        for k in range(CONV_K):
            acc = acc + cw_ref[k:k + 1, :] * ushift_ref[k % SUBLANES, pl.ds(r0 + k - k % SUBLANES, rc), :]
        yc = _layer_norm(acc + cb_ref[...], clg_ref[...], clb_ref[...])
        ycat_ref[r0:r0 + rc, 0:256] = _bf(yc * jax.nn.sigmoid(yc))

    y = y_rest + jnp.dot(ycat_ref[:, 0:256], wout_ref[0:256, :], preferred_element_type=jnp.float32)
    out_ref[...] = x_ref[...] + _rms(y, gpost_ref[...])


def _mix_out(l, x, za, zc, zd, ynsa, cw, cb, clg, clb, slg, slb, sw_cat, sb_exp, pw_bd, ps, wout, gpost):
    bsz, seq, d = x.shape
    tm = min(TM_MIX, seq)
    tok = lambda c: pl.BlockSpec((None, tm, c), lambda b, i: (b, i, 0))
    halo = lambda c, h: pl.BlockSpec((None, h, c), lambda b, i: (b, jnp.maximum(i * (tm // h) - 1, 0), 0))
    weights = [cw, cb, clg, clb, slg, slb, sw_cat, sb_exp, pw_bd, ps, wout, gpost]
    return pl.pallas_call(
        functools.partial(_mix_out_kernel, tm=tm),
        grid=(bsz, seq // tm),
        in_specs=[tok(d), tok(512), halo(512, CONV_HALO), tok(512), tok(256), halo(256, POOL_HALO), tok(256)]
                 + [_layer_spec(w, l) for w in weights],
        out_specs=tok(d),
        out_shape=jax.ShapeDtypeStruct((bsz, seq, d), jnp.float32),
        scratch_shapes=[pltpu.VMEM((tm + CONV_HALO, 256), jnp.float32),
                        pltpu.VMEM((SUBLANES, tm + CONV_HALO - SUBLANES, 256), jnp.float32),
                        pltpu.VMEM((tm, 4 * 256), MXU_DTYPE),
                        pltpu.VMEM((tm + POOL_HALO, 256), jnp.float32)],
        compiler_params=_cparams(2),
        name="mix_out",
    )(x, za, za, zc, zd, zd, ynsa, *weights)


def _ffn_kernel(x_ref, xh_ref, gpre_ref, wup_ref, cw_ref, cb_ref, wdn_ref, gpost_ref, out_ref,
                hext_ref, g0_ref, u0_ref, g1_ref, u1_ref, acc_ref, *, tm, n_chunks, chunk):
    i = pl.program_id(1)
    not_first = jnp.where(i > 0, 1.0, 0.0)
    x = x_ref[...]
    hext_ref[0:FFN_HALO, :] = _bf(_rms(xh_ref[...], gpre_ref[...]) * not_first)
    hext_ref[FFN_HALO:, :] = _bf(_rms(x, gpre_ref[...]))

    def cols(c):
        return pl.ds(pl.multiple_of(c * chunk, chunk), chunk)

    def up_to(g_ref, u_ref, c):
        h_ext = hext_ref[...]
        g_ref[...] = jnp.dot(h_ext, wup_ref[:, cols(c)], preferred_element_type=jnp.float32)
        u_ref[...] = jnp.dot(h_ext, wup_ref[:, cols(n_chunks + c)], preferred_element_type=jnp.float32)

    def conv(ext_ref, c):
        w = cw_ref[:, cols(c)]
        return (w[0:1] * ext_ref[pl.ds(FFN_HALO - 2, tm), :] + w[1:2] * ext_ref[pl.ds(FFN_HALO - 1, tm), :]
                + w[2:3] * ext_ref[pl.ds(FFN_HALO, tm), :] + cb_ref[:, cols(c)])

    def down_from(g_ref, u_ref, c):
        act = jax.nn.gelu(conv(g_ref, c), approximate=True) * conv(u_ref, n_chunks + c)
        acc_ref[...] += jnp.dot(_bf(act), wdn_ref[c], preferred_element_type=jnp.float32)

    assert n_chunks % 2 == 1
    acc_ref[...] = jnp.zeros_like(acc_ref)
    up_to(g0_ref, u0_ref, 0)

    def pair(j, carry):
        c = 2 * j
        up_to(g1_ref, u1_ref, c + 1)
        down_from(g0_ref, u0_ref, c)
        up_to(g0_ref, u0_ref, c + 2)
        down_from(g1_ref, u1_ref, c + 1)
        return carry

    lax.fori_loop(0, (n_chunks - 1) // 2, pair, 0)
    down_from(g0_ref, u0_ref, n_chunks - 1)
    out_ref[...] = x + _rms(acc_ref[...], gpost_ref[...])


def _ffn(l, x, gpre, wup, cw, cb, wdn, gpost):
    bsz, seq, d = x.shape
    tm = min(TM_FFN, seq)
    n_chunks = wdn.shape[1]
    chunk = wdn.shape[2]
    tok = pl.BlockSpec((None, tm, d), lambda b, i: (b, i, 0))
    halo = pl.BlockSpec((None, FFN_HALO, d), lambda b, i: (b, jnp.maximum(i * (tm // FFN_HALO) - 1, 0), 0))
    weights = [gpre, wup, cw, cb, wdn, gpost]
    return pl.pallas_call(
        functools.partial(_ffn_kernel, tm=tm, n_chunks=n_chunks, chunk=chunk),
        grid=(bsz, seq // tm),
        in_specs=[tok, halo] + [_layer_spec(w, l) for w in weights],
        out_specs=tok,
        out_shape=jax.ShapeDtypeStruct((bsz, seq, d), jnp.float32),
        scratch_shapes=[pltpu.VMEM((tm + FFN_HALO, d), MXU_DTYPE)]
                       + [pltpu.VMEM((tm + FFN_HALO, chunk), jnp.float32)] * 4
                       + [pltpu.VMEM((tm, d), jnp.float32)],
        compiler_params=_cparams(2),
        name="ffn",
    )(x, x, *weights)


def _prep_w_in(w):
    n_in = w.shape[-1]
    n_pad = -(-n_in // 256) * 256
    return jnp.pad(_bf(w), ((0, 0), (0, 0), (0, n_pad - n_in)))


def _prep_compress(pe_k, pe_v, ck_w1, ck_w2, cv_w1, cv_w2):
    half = CMP_LEN // 2

    def w1_part(w1k, w1v, lo):
        wk = w1k.reshape(CMP_LEN, HEAD_DIM, HEAD_DIM)[lo:lo + half]
        wv = w1v.reshape(CMP_LEN, HEAD_DIM, HEAD_DIM)[lo:lo + half]
        z = jnp.zeros_like(wk)
        top = jnp.concatenate([wk, z], axis=-1)
        bot = jnp.concatenate([z, wv], axis=-1)
        return jnp.concatenate([top, bot], axis=1).reshape(half * 2 * HEAD_DIM, 2 * HEAD_DIM)

    def pe_part(lo):
        return jnp.concatenate([pe_k[lo:lo + half], pe_v[lo:lo + half]], axis=1).reshape(1, half * 2 * HEAD_DIM)

    z = jnp.zeros_like(ck_w2)
    w2 = jnp.concatenate([jnp.concatenate([ck_w2, z], axis=1), jnp.concatenate([z, cv_w2], axis=1)], axis=0)
    return (pe_part(0), pe_part(half), _bf(w1_part(ck_w1, cv_w1, 0)), _bf(w1_part(ck_w1, cv_w1, half)), _bf(w2))


def _importance_matrix(seq):
    n_cmp = (seq - CMP_LEN) // CMP_STRIDE + 1
    n_sel = seq // SEL_LEN
    ratio = SEL_LEN // CMP_STRIDE
    n_ov = CMP_LEN // CMP_STRIDE
    n_pad = -(-n_cmp // LANES) * LANES
    assert n_sel <= LANES, "selection blocks must fit one lane group"
    assert min(SEL_TOPK, n_sel) > 3, "top-k must exceed the three forced blocks"
    mt = np.zeros((LANES, n_pad), np.float32)
    for j in range(n_sel):
        for m in range(ratio):
            for n in range(n_ov):
                c = ratio * j + m - n
                if 0 <= c < n_cmp:
                    mt[j, c] += 1.0
    return jnp.asarray(mt, MXU_DTYPE)


def _rope_table(seq):
    half = HEAD_DIM // 2
    per_row = LANES // half
    inv = ROPE_THETA ** (-jnp.arange(half, dtype=jnp.float32) * 2.0 / HEAD_DIM)
    pos = jnp.arange(seq, dtype=jnp.float32).reshape(seq // per_row, per_row, 1)
    ang = (pos * inv[None, None, :]).reshape(seq // per_row, LANES)
    cos, sin = lax.optimization_barrier((jnp.cos(ang), jnp.sin(ang)))
    cos, sin = cos.reshape(seq, half), sin.reshape(seq, half)
    return jnp.concatenate([cos, cos, sin, sin], axis=1)


def _block_diag(w):
    g, c, d = w.shape
    out = jnp.zeros((g * c, g * d), w.dtype)
    for j in range(g):
        out = out.at[j * c:(j + 1) * c, j * d:(j + 1) * d].set(w[j])
    return out


def kernel(x, norm_mix_pre, norm_mix_post, norm_ffn_pre, norm_ffn_post, w_in, w_out, conv_dw_w, conv_dw_b, conv_ln_g, conv_ln_b, nsa_pe_k, nsa_pe_v, nsa_ck_w1, nsa_ck_w2, nsa_cv_w1, nsa_cv_w2, sgu_ln_g, sgu_ln_b, sgu_w, sgu_b, pool_w, pool_scale, ffn_up, ffn_conv_w, ffn_conv_b, ffn_down):
    bsz, seq, d = x.shape
    depth = w_in.shape[0]
    ffn_dim = ffn_down.shape[1]
    cs = _rope_table(seq)
    mt = _importance_matrix(seq)
    rows = lambda v: v[:, None, :]

    w_ext = _prep_w_in(w_in)
    cmp_w = jax.vmap(_prep_compress)(nsa_pe_k, nsa_pe_v, nsa_ck_w1, nsa_ck_w2, nsa_cv_w1, nsa_cv_w2)
    sw_cat = jnp.transpose(sgu_w, (0, 2, 1, 3)).reshape(depth, SGU_CHUNK, SGU_GROUPS * SGU_CHUNK)
    sb_exp = jnp.repeat(jnp.swapaxes(sgu_b, 1, 2), 256 // SGU_GROUPS, axis=2)
    mix_w = [conv_dw_w, rows(conv_dw_b), rows(conv_ln_g), rows(conv_ln_b), rows(sgu_ln_g), rows(sgu_ln_b),
             sw_cat, sb_exp, _bf(jax.vmap(_block_diag)(pool_w)), rows(pool_scale), _bf(w_out), rows(norm_mix_post)]
    ffn_w = [rows(norm_ffn_pre), _bf(ffn_up), ffn_conv_w, rows(ffn_conv_b),
             _bf(ffn_down).reshape(depth, ffn_dim // FFN_CHUNK, FFN_CHUNK, d), rows(norm_ffn_post)]
    g_pre = rows(norm_mix_pre)

    for l in range(depth):
        za, zc, zd, qc, qr, a_cmp, ks, vs, kw, vw, gate = _in_proj(l, x, g_pre, w_ext, cs)
        kk, vv = _compress(l, a_cmp, *cmp_w)
        y_nsa = _nsa(qc, qr, gate, kk, vv, mt, ks, vs, kw, vw)
        x = _mix_out(l, x, za, zc, zd, y_nsa, *mix_w)
        x = _ffn(l, x, *ffn_w)
    return x
```

```python
import functools

import numpy as np
import jax
import jax.numpy as jnp
from jax import lax
from jax.experimental import pallas as pl
from jax.experimental.pallas import tpu as pltpu

HEAD_DIM = 64
NSA_HEADS = 4
CONV_K = 31
CMP_LEN = 32
CMP_STRIDE = 16
SEL_LEN = 64
SEL_TOPK = 16
WINDOW = 512
SGU_CHUNK = 128
SGU_GROUPS = 4
POOL_WINDOWS = (2, 4, 8, 16)
FFN_CONV_K = 3
ROPE_THETA = 10000.0
RMS_EPS = 1e-6
LN_EPS = 1e-5
NEG_INF = -1e30
FORCE_SCORE = 1e6
REMOVED = -3e38
LOG2E = 1.4426950408889634

LANES = 128
SUBLANES = 8
V_ROWS = 80
MXU_DTYPE = jnp.bfloat16
VMEM_LIMIT = 56 * 1024 * 1024

TM_IN = 1024
TM_MIX = 512
TM_FFN = 1024
TQ = 256
TK = 512
FFN_CHUNK = 256
CONV_HALO = 32
POOL_HALO = 16
FFN_HALO = 16

_NT = (((1,), (1,)), ((), ()))


def _cparams(n_axes, flags=None):
    return pltpu.CompilerParams(dimension_semantics=("arbitrary",) * n_axes,
                                vmem_limit_bytes=VMEM_LIMIT, flags=flags)


def _layer_spec(a, l):
    return pl.BlockSpec((None,) + a.shape[1:], lambda *_: (l,) + (0,) * (a.ndim - 1),
                        pipeline_mode=pl.Buffered(1))


def _rms(x, g):
    return x * lax.rsqrt(jnp.mean(x * x, axis=-1, keepdims=True) + RMS_EPS) * g


def _layer_norm(x, g, b):
    mu = jnp.mean(x, axis=-1, keepdims=True)
    var = jnp.mean(jnp.square(x - mu), axis=-1, keepdims=True)
    return (x - mu) * lax.rsqrt(var + LN_EPS) * g + b


def _bf(x):
    return x.astype(MXU_DTYPE)


def _swap64(x):
    return pltpu.roll(x, 64, axis=1)


def _in_proj_kernel(x_ref, g_ref, w_ref, cs_ref,
                    za_ref, zc_ref, zd_ref, qc_ref, qr_ref, kv_ref,
                    ks_ref, vs_ref, kw_ref, vw_ref, gate_ref, kvt_ref, *, tm, tk):
    i = pl.program_id(1)
    h = _bf(_rms(x_ref[...], g_ref[...]))

    def proj(lo, hi):
        return jnp.dot(h, w_ref[:, lo:hi], preferred_element_type=jnp.float32)

    lane = lax.broadcasted_iota(jnp.int32, (tm, LANES), 1)
    q = proj(512, 768)
    kv01 = proj(768, 1024)
    kv2t = proj(1024, 1280)
    kvt_ref[...] = kv01[:, :LANES]
    group = CMP_LEN // 2
    for j in range(group):
        kv_ref[:, j * LANES:(j + 1) * LANES] = kvt_ref[pl.ds(j, tm // group, stride=group), :]
    kvs, kvw, tail0 = kv01[:, LANES:], kv2t[:, :LANES], kv2t[:, LANES:]
    kx = jnp.where(lane < 64, kvs, _swap64(kvw))
    vsw = jnp.where(lane < 64, _swap64(kvs), kvw)
    gate_ref[...] = jax.nn.sigmoid(tail0)

    cs = cs_ref[...]
    sc = _swap64(cs)
    cos2 = jnp.where(lane < 64, cs, sc)
    sin2 = jnp.where(lane < 64, sc, cs)
    first_half = (lane & (HEAD_DIM - 1)) < HEAD_DIM // 2

    def rope(x):
        swapped = jnp.where(first_half, -pltpu.roll(x, LANES - HEAD_DIM // 2, axis=1),
                            pltpu.roll(x, HEAD_DIM // 2, axis=1))
        return x * cos2 + swapped * sin2

    scale = HEAD_DIM ** -0.5 * LOG2E
    qc_ref[...] = _bf(q * scale)
    for pair in range(NSA_HEADS // 2):
        c = slice(pair * LANES, (pair + 1) * LANES)
        qr_ref[:, c] = _bf(rope(q[:, c]) * scale)

    kx_rot = rope(kx)
    kx_swap = _swap64(kx_rot)
    k_rot2 = jnp.where(lane < 64, kx_rot, kx_swap)
    kw_rot2 = jnp.where(lane < 64, kx_swap, kx_rot)
    pos = i * tm + lax.broadcasted_iota(jnp.int32, (tm, LANES), 0)
    onehot =jnp.where((pos >> 6) == lane, 1.0, 0.0)
    ks_ref[:, 0:LANES] = _bf(onehot)
    ks_ref[:, LANES:2 * LANES] = _bf(k_rot2)
    kw_ref[...] = _bf(kw_rot2)
    ones_col = jnp.where(lane == 64, 1.0, 0.0)
    vs = jnp.where(lane < 64, vsw, ones_col)
    vw = jnp.where(lane < 64, _swap64(vsw), ones_col)
    for j in range(tm // tk):
        vs_ref[j] = _bf(vs[j * tk:(j + 1) * tk].T[0:V_ROWS])
    for j in range(tm // TQ):
        vw_ref[j] = _bf(vw[j * TQ:(j + 1) * TQ].T[0:V_ROWS])

    tail = jnp.concatenate([tail0, proj(1280, 2048)], axis=1)
    n_gate = 3 * NSA_HEADS
    zc_ref[...] = tail[:, n_gate:n_gate + 512]
    zd_ref[...] = tail[:, n_gate + 512:n_gate + 768]
    za_ref[...] = proj(0, 512)


def _in_proj(l, x, g, w_ext, cs):
    bsz, seq, d = x.shape
    tm = min(TM_IN, seq)
    tk = min(TK, seq)
    tok = lambda c: pl.BlockSpec((None, tm, c), lambda b, i: (b, i, 0))
    f32, bf16 = jnp.float32, MXU_DTYPE

    def tok_out(c, dt):
        return tok(c), jax.ShapeDtypeStruct((bsz, seq, c), dt)

    def keytile_out(t):
        return (pl.BlockSpec((None, tm // t, V_ROWS, t), lambda b, i: (b, i, 0, 0)),
                jax.ShapeDtypeStruct((bsz, seq // t, V_ROWS, t), bf16))

    group = CMP_LEN // 2
    cmp_out = (pl.BlockSpec((None, tm // group, group * LANES), lambda b, i: (b, i, 0)),
               jax.ShapeDtypeStruct((bsz, seq // group, group * LANES), f32))
    outs = [tok_out(512, f32), tok_out(512, f32), tok_out(256, f32), tok_out(256, bf16), tok_out(256, bf16),
            cmp_out, tok_out(256, bf16), keytile_out(tk), tok_out(128, bf16), keytile_out(TQ),
            tok_out(128, f32)]
    return pl.pallas_call(
        functools.partial(_in_proj_kernel, tm=tm, tk=tk),
        grid=(bsz, seq // tm),
        in_specs=[tok(d), _layer_spec(g, l), _layer_spec(w_ext, l),
                  pl.BlockSpec((tm, LANES), lambda b, i: (i, 0))],
        out_specs=[spec for spec, _ in outs],
        out_shape=[shape for _, shape in outs],
        scratch_shapes=[pltpu.VMEM((tm, LANES), f32)],
        compiler_params=_cparams(2),
        name="in_proj",
    )(x, g, w_ext, cs)


def _compress_kernel(a_ref, pe_lo_ref, pe_hi_ref, w_lo_ref, w_hi_ref, w2_ref, kk_ref, vv_ref):
    a = a_ref[...]
    u = jnp.dot(_bf(a + pe_lo_ref[...]), w_lo_ref[...], preferred_element_type=jnp.float32)
    v = jnp.dot(_bf(a + pe_hi_ref[...]), w_hi_ref[...], preferred_element_type=jnp.float32)
    n = a.shape[0]
    pre = u + pltpu.roll(v, n - 1, axis=0)
    act = jax.nn.gelu(pre, approximate=True)
    kv = jnp.dot(_bf(act), w2_ref[...], preferred_element_type=jnp.float32)
    vk = _swap64(kv)
    lane = lax.broadcasted_iota(jnp.int32, (n, LANES), 1)
    kk_ref[...] = _bf(jnp.where(lane < 64, kv, vk))
    vv = jnp.where(lane < 64, vk, jnp.where(lane == 64, 1.0, 0.0))
    vv_ref[...] = _bf(vv.T[0:V_ROWS])


def _compress(l, a, pe_lo, pe_hi, w_lo, w_hi, w2):
    bsz, n, width = a.shape
    per_b = lambda c: pl.BlockSpec((None, n, c), lambda b: (b, 0, 0))
    return pl.pallas_call(
        _compress_kernel,
        grid=(bsz,),
        in_specs=[per_b(width)] + [_layer_spec(w, l) for w in (pe_lo, pe_hi, w_lo, w_hi, w2)],
        out_specs=[per_b(LANES), pl.BlockSpec((None, V_ROWS, n), lambda b: (b, 0, 0))],
        out_shape=[jax.ShapeDtypeStruct((bsz, n, LANES), MXU_DTYPE),
                   jax.ShapeDtypeStruct((bsz, V_ROWS, n), MXU_DTYPE)],
        compiler_params=_cparams(1),
        name="compress",
    )(a, pe_lo, pe_hi, w_lo, w_hi, w2)


def _nsa_kernel(qc_ref, qr_ref, gate_ref, kk_ref, vv_ref, mt_ref, mask_ref, ks_ref, vs_ref, kw_ref, vw_ref,
                out_ref, acc_ref, s0_ref, s1_ref, ocmp_ref, imp_ref, m_ref, mx0_ref, mx1_ref, *, tq, tk, seq):
    qi = pl.program_id(1)
    t0 = qi * tq
    rows = NSA_HEADS * tq
    n_cmp_pad = kk_ref.shape[0]
    top_n = min(SEL_TOPK, seq // SEL_LEN)
    n_sel = LANES
    win_keys = min(WINDOW + tq, seq)

    lane =lax.broadcasted_iota(jnp.int32, (tq, LANES), 1)

    def head_rows(q_ref):
        parts = []
        for hd in range(NSA_HEADS):
            pair = q_ref[:, (hd // 2) * LANES:(hd // 2 + 1) * LANES]
            keep = (lane < 64) if hd % 2 == 0 else (lane >= 64)
            parts.append(jnp.where(keep, pair, jnp.zeros_like(pair)))
        return jnp.concatenate(parts, axis=0)

    q_c = head_rows(qc_ref)
    q_r = head_rows(qr_ref)
    pos_l = t0 + (lax.broadcasted_iota(jnp.int32, (1, rows), 1) & (tq - 1))

    last_blk = (pos_l - (CMP_LEN - 1)) >> 4

    def compressed(nr):
        s = lax.dot_general(kk_ref[0:nr, :], q_c, _NT, preferred_element_type=jnp.float32)
        n_idx = lax.broadcasted_iota(jnp.int32, (nr, rows), 0)
        sm = jnp.where(n_idx <= last_blk, s, NEG_INF)
        mx = jnp.max(sm, axis=0, keepdims=True)
        e = jnp.exp2(sm - mx)
        den = jnp.sum(e, axis=0, keepdims=True)
        p_cmp = e * jnp.where(last_blk >= 0, 1.0 / den, 0.0)
        ocmp_ref[...] = jnp.dot(vv_ref[:, 0:nr], _bf(p_cmp), preferred_element_type=jnp.float32)
        p_sum = p_cmp[:, 0:tq] + p_cmp[:, tq:2 * tq] + p_cmp[:, 2 * tq:3 * tq] + p_cmp[:, 3 * tq:4 * tq]
        hi = _bf(p_sum)
        r1 = p_sum - hi.astype(jnp.float32)
        mid = _bf(r1)
        lo = _bf(r1 - mid.astype(jnp.float32))
        mt = mt_ref[:, 0:nr]
        imp_ref[...] = (jnp.dot(mt, hi, preferred_element_type=jnp.float32)
                        + jnp.dot(mt, mid, preferred_element_type=jnp.float32)
                        + jnp.dot(mt, lo, preferred_element_type=jnp.float32))

    n_var = n_cmp_pad // LANES
    tile_last = jnp.maximum((t0 + tq - CMP_LEN) >> 4, 0)
    need = jnp.minimum(tile_last // LANES, n_var - 1)
    for v in range(n_var):
        pl.when(need == v)(functools.partial(compressed, (v + 1) * LANES))
    o_cmp = ocmp_ref[...]
    imp = imp_ref[...]

    start = pl.multiple_of(jnp.maximum(t0 + tq - win_keys, 0), tq)
    sw = lax.dot_general(kw_ref[pl.ds(start, win_keys), :], q_r, _NT, preferred_element_type=jnp.float32)
    w_case = jnp.minimum(t0 // tq, 2)
    m_idx = (jnp.where(w_case == 0, 1, jnp.where(w_case == 1, 0, 3)),
             jnp.where(w_case == 0, 2, jnp.where(w_case == 1, 1, 0)),
             jnp.where(w_case == 2, 1, 2))
    sw = jnp.concatenate([sw[j * tq:(j + 1) * tq] + mask_ref[m_idx[j]] for j in range(3)], axis=0)
    mw = jnp.max(sw, axis=0, keepdims=True)
    pw = _bf(jnp.exp2(sw - mw))
    wt0 = start // tq
    ow = jnp.dot(vw_ref[wt0], pw[0:tq], preferred_element_type=jnp.float32)
    for j in range(1, win_keys // tq):
        ow = ow + jnp.dot(vw_ref[wt0 + j], pw[j * tq:(j + 1) * tq], preferred_element_type=jnp.float32)
    o_win = ow / ow[64:65, :]

    blk = lax.broadcasted_iota(jnp.int32, (n_sel, tq), 0)
    blk_f = blk.astype(jnp.float32)
    pos_t = t0 + lax.broadcasted_iota(jnp.int32, (n_sel, tq), 1)
    cur = pos_t >> 6
    forced = (blk == 0) | (blk == cur) | (blk == cur - 1)
    valid = (blk * SEL_LEN) <= pos_t
    val = jnp.where(valid & jnp.logical_not(forced), imp, NEG_INF)
    for _ in range(top_n - 3):
        best = jnp.max(val, axis=0, keepdims=True)
        first = jnp.min(jnp.where(val == best, blk_f, float(n_sel)), axis=0, keepdims=True)
        val = jnp.where(blk_f == first, REMOVED, val)
    bias_t = jnp.where((forced | (val == REMOVED)) & valid, 0.0, NEG_INF)
    bias = _bf(bias_t.T)

    l_sel = jnp.concatenate(
        [jnp.concatenate([bias, q_r[hd * tq:(hd + 1) * tq]], axis=1) for hd in range(NSA_HEADS)], axis=0)
    m_ref[...] = jnp.full((1, rows), NEG_INF, jnp.float32)
    acc_ref[...] = jnp.zeros((V_ROWS, rows), jnp.float32)

    def scores_to(buf, kt, causal):
        s_ref, mx_ref = buf
        k0 = pl.multiple_of(kt * tk, tk)
        sc = lax.dot_general(ks_ref[pl.ds(k0, tk), :], l_sel, _NT, preferred_element_type=jnp.float32)
        if causal:
            own = (t0 - k0) // tq
            lo = sc[0:tq] + mask_ref[1 - own]
            hi = sc[tq:2 * tq] + mask_ref[2 - own]
            s_ref[0:tq, :] = lo
            s_ref[tq:2 * tq, :] = hi
            mx_ref[...] = jnp.maximum(jnp.max(lo, axis=0, keepdims=True), jnp.max(hi, axis=0, keepdims=True))
        else:
            s_ref[...] = sc
            mx_ref[...] = jnp.max(sc, axis=0, keepdims=True)

    def softmax_from(buf, kt):
        s_ref, mx_ref = buf
        sc = s_ref[...]
        m_old = m_ref[...]
        m_new = jnp.maximum(m_old, mx_ref[...])
        alpha = jnp.exp2(m_old - m_new)
        p = jnp.exp2(sc - m_new)
        acc_ref[...] = alpha * acc_ref[...] + jnp.dot(vs_ref[kt], _bf(p), preferred_element_type=jnp.float32)
        m_ref[...] = m_new

    buf0, buf1 = (s0_ref, mx0_ref), (s1_ref, mx1_ref)
    n_full = t0 // tk
    n_pairs = n_full // 2
    scores_to(buf0, n_full, True)

    def pair(j, carry):
        k = 2 * j
        scores_to(buf1, k, False)
        softmax_from(buf0, jnp.where(j == 0, n_full, k - 1))
        scores_to(buf0, k + 1, False)
        softmax_from(buf1, k)
        return carry

    lax.fori_loop(0, n_pairs, pair, 0)
    pending = jnp.where(n_pairs == 0, n_full, 2 * n_pairs - 1)

    @pl.when(n_full % 2 == 1)
    def _():
        scores_to(buf1, n_full - 1, False)
        softmax_from(buf0, pending)
        softmax_from(buf1, n_full - 1)

    @pl.when(n_full % 2 == 0)
    def _():
        softmax_from(buf0, pending)

    acc = acc_ref[...]
    o_slc = acc / acc[64:65, :]

    gate_t = gate_ref[...].T
    y_t = []
    for hd in range(NSA_HEADS):
        c = slice(hd * tq, (hd + 1) * tq)
        y_t.append(gate_t[3 * hd:3 * hd + 1] * o_cmp[0:64, c]
                   + gate_t[3 * hd + 1:3 * hd + 2] * o_slc[0:64, c]
                   + gate_t[3 * hd + 2:3 * hd + 3] * o_win[0:64, c])
    out_ref[:, 0:LANES] = _bf(jnp.concatenate(y_t[0:2], axis=0).T)
    out_ref[:, LANES:2 * LANES] = _bf(jnp.concatenate(y_t[2:4], axis=0).T)


def _nsa(qc, qr, gate, kk, vv, mt, ks, vs, kw, vw):
    bsz, seq, _ = qc.shape
    tq = min(TQ, seq)
    tk = min(TK, seq)
    n_cmp_pad = kk.shape[1]
    rows = NSA_HEADS * tq
    assert tk == 2 * tq and min(WINDOW + tq, seq) == 3 * tq, "mask patterns assume 2 / 3 query-tile-sized key blocks"
    r_idx = lax.broadcasted_iota(jnp.int32, (tq, rows), 0)
    tau = lax.broadcasted_iota(jnp.int32, (tq, rows), 1) & (tq - 1)
    zero = jnp.zeros((tq, rows), jnp.float32)
    masks = jnp.stack([zero, jnp.where(r_idx <= tau, 0.0, NEG_INF), zero + NEG_INF,
                       jnp.where(r_idx > tau, 0.0, NEG_INF)])
    tile = lambda c: pl.BlockSpec((None, tq, c), lambda b, i: (b, i, 0))
    per_b = lambda r, c: pl.BlockSpec((None, r, c), lambda b, i: (b, 0, 0))
    per_b4 = lambda a: pl.BlockSpec((None,) + a.shape[1:], lambda b, i: (b, 0, 0, 0))
    const = lambda a: pl.BlockSpec(a.shape, lambda b, i: (0,) * a.ndim, pipeline_mode=pl.Buffered(1))
    return pl.pallas_call(
        functools.partial(_nsa_kernel, tq=tq, tk=tk, seq=seq),
        grid=(bsz, seq // tq),
        in_specs=[tile(256), tile(256), tile(LANES),
                  per_b(n_cmp_pad, LANES), per_b(V_ROWS, n_cmp_pad), const(mt), const(masks),
                  per_b(seq, 256), per_b4(vs), per_b(seq, LANES), per_b4(vw)],
        out_specs=tile(256),
        out_shape=jax.ShapeDtypeStruct((bsz, seq, 256), MXU_DTYPE),
        scratch_shapes=[pltpu.VMEM((V_ROWS, rows), jnp.float32),
                        pltpu.VMEM((tk, rows), jnp.float32), pltpu.VMEM((tk, rows), jnp.float32),
                        pltpu.VMEM((V_ROWS, rows), jnp.float32), pltpu.VMEM((LANES, tq), jnp.float32),
                        pltpu.VMEM((1, rows), jnp.float32), pltpu.VMEM((1, rows), jnp.float32),
                        pltpu.VMEM((1, rows), jnp.float32)],
        compiler_params=_cparams(2),
        name="nsa",
    )(qc, qr, gate, kk, vv, mt, masks, ks, vs, kw, vw)


def _mix_out_kernel(x_ref, za_ref, zah_ref, zc_ref, zd_ref, zdh_ref, ynsa_ref,
                    cw_ref, cb_ref, clg_ref, clb_ref, slg_ref, slb_ref, sw_ref, sb_ref,
                    pw_ref, ps_ref, wout_ref, gpost_ref, out_ref,
                    uext_ref, ushift_ref, ycat_ref, pext_ref, *, tm):
    i = pl.program_id(1)
    not_first = jnp.where(i > 0, 1.0, 0.0)

    zc = jax.nn.gelu(zc_ref[...], approximate=True)
    u_s = zc[:, :256]
    v_s = _layer_norm(zc[:, 256:], slg_ref[...], slb_ref[...])
    t_i = lax.broadcasted_iota(jnp.int32, (SGU_CHUNK, SGU_GROUPS * SGU_CHUNK), 0)
    s_i = lax.rem(lax.broadcasted_iota(jnp.int32, (SGU_CHUNK, SGU_GROUPS * SGU_CHUNK), 1), SGU_CHUNK)
    w_cat = _bf(jnp.where(s_i <= t_i, sw_ref[...], 0.0))
    grp = lax.broadcasted_iota(jnp.int32, (SGU_CHUNK, 256), 1) >> 6
    ysgu = []
    for c in range(tm // SGU_CHUNK):
        vc = v_s[c * SGU_CHUNK:(c + 1) * SGU_CHUNK]
        stacked = _bf(jnp.concatenate([jnp.where(grp == g, vc, 0.0) for g in range(SGU_GROUPS)], axis=0))
        f = jnp.dot(w_cat, stacked, preferred_element_type=jnp.float32) + sb_ref[...]
        ysgu.append(u_s[c * SGU_CHUNK:(c + 1) * SGU_CHUNK] * f)
    ycat_ref[:, 512:768] = _bf(jnp.concatenate(ysgu, axis=0))

    pext_ref[0:POOL_HALO, :] = zdh_ref[...] * not_first
    pext_ref[POOL_HALO:, :] = zd_ref[...]
    n_ext = tm + POOL_HALO
    sums = []
    for shift in (1, 2, 4, 8):
        n = n_ext - shift
        nxt = pext_ref[pl.ds(shift, n), :] + pext_ref[pl.ds(0, n), :]
        pext_ref[pl.ds(shift, n), :] = nxt
        sums.append(pext_ref[pl.ds(POOL_HALO, tm), :])
    lane_grp = lax.broadcasted_iota(jnp.int32, (tm, 256), 1) >> 6
    pos = i * tm + lax.broadcasted_iota(jnp.int32, (tm, 256), 0)
    wsum = jnp.where(lane_grp == 0, sums[0], jnp.where(lane_grp == 1, sums[1],
                     jnp.where(lane_grp == 2, sums[2], sums[3])))
    width = jnp.where(lane_grp == 0, POOL_WINDOWS[0], jnp.where(lane_grp == 1, POOL_WINDOWS[1],
                      jnp.where(lane_grp == 2, POOL_WINDOWS[2], POOL_WINDOWS[3])))
    cnt = jnp.minimum(pos + 1, width).astype(jnp.float32)
    pooled = wsum / cnt - zd_ref[...]
    ycat_ref[:, 768:1024] = _bf(jnp.dot(_bf(pooled), pw_ref[...], preferred_element_type=jnp.float32) * ps_ref[...])

    ycat_ref[:, 256:512] = ynsa_ref[...]
    y_rest = (jnp.dot(ycat_ref[:, 256:512], wout_ref[256:512, :], preferred_element_type=jnp.float32)
              + jnp.dot(ycat_ref[:, 512:768], wout_ref[512:768, :], preferred_element_type=jnp.float32)
              + jnp.dot(ycat_ref[:, 768:1024], wout_ref[768:1024, :], preferred_element_type=jnp.float32))

    def glu(z):
        return z[:, :256] * jax.nn.sigmoid(z[:, 256:])

    uext_ref[0:CONV_HALO, :] = glu(zah_ref[...]) * not_first
    uext_ref[CONV_HALO:, :] = glu(za_ref[...])
    first = CONV_HALO - (CONV_K - 1)
    n_shift = tm + CONV_HALO - SUBLANES
    for r in range(SUBLANES):
        n = n_shift if first + r + n_shift <= tm + CONV_HALO else n_shift - SUBLANES
        ushift_ref[r, 0:n, :] = uext_ref[pl.ds(first + r, n), :]
    rc = 64
    for r0 in range(0, tm, rc):
        acc = jnp.zeros((rc, 256), jnp.float32)
        for k in range(CONV_K):
            acc = acc + cw_ref[k:k + 1, :] * ushift_ref[k % SUBLANES, pl.ds(r0 + k - k % SUBLANES, rc), :]
        yc = _layer_norm(acc + cb_ref[...], clg_ref[...], clb_ref[...])
        ycat_ref[r0:r0 + rc, 0:256] = _bf(yc * jax.nn.sigmoid(yc))

    y = y_rest + jnp.dot(ycat_ref[:, 0:256], wout_ref[0:256, :], preferred_element_type=jnp.float32)
    out_ref[...] = x_ref[...] + _rms(y, gpost_ref[...])


def _mix_out(l, x, za, zc, zd, ynsa, cw, cb, clg, clb, slg, slb, sw_cat, sb_exp, pw_bd, ps, wout, gpost):
    bsz, seq, d = x.shape
    tm = min(TM_MIX, seq)
    tok = lambda c: pl.BlockSpec((None, tm, c), lambda b, i: (b, i, 0))
    halo = lambda c, h: pl.BlockSpec((None, h, c), lambda b, i: (b, jnp.maximum(i * (tm // h) - 1, 0), 0))
    weights = [cw, cb, clg, clb, slg, slb, sw_cat, sb_exp, pw_bd, ps, wout, gpost]
    return pl.pallas_call(
        functools.partial(_mix_out_kernel, tm=tm),
        grid=(bsz, seq // tm),
        in_specs=[tok(d), tok(512), halo(512, CONV_HALO), tok(512), tok(256), halo(256, POOL_HALO), tok(256)]
                 + [_layer_spec(w, l) for w in weights],
        out_specs=tok(d),
        out_shape=jax.ShapeDtypeStruct((bsz, seq, d), jnp.float32),
        scratch_shapes=[pltpu.VMEM((tm + CONV_HALO, 256), jnp.float32),
                        pltpu.VMEM((SUBLANES, tm + CONV_HALO - SUBLANES, 256), jnp.float32),
                        pltpu.VMEM((tm, 4 * 256), MXU_DTYPE),
                        pltpu.VMEM((tm + POOL_HALO, 256), jnp.float32)],
        compiler_params=_cparams(2),
        name="mix_out",
    )(x, za, za, zc, zd, zd, ynsa, *weights)


def _ffn_kernel(x_ref, xh_ref, gpre_ref, wup_ref, cw_ref, cb_ref, wdn_ref, gpost_ref, out_ref,
                hext_ref, g0_ref, u0_ref, g1_ref, u1_ref, acc_ref, *, tm, n_chunks, chunk):
    i = pl.program_id(1)
    not_first = jnp.where(i > 0, 1.0, 0.0)
    x = x_ref[...]
    hext_ref[0:FFN_HALO, :] = _bf(_rms(xh_ref[...], gpre_ref[...]) * not_first)
    hext_ref[FFN_HALO:, :] = _bf(_rms(x, gpre_ref[...]))

    def cols(c):
        return pl.ds(pl.multiple_of(c * chunk, chunk), chunk)

    def up_to(g_ref, u_ref, c):
        h_ext = hext_ref[...]
        g_ref[...] = jnp.dot(h_ext, wup_ref[:, cols(c)], preferred_element_type=jnp.float32)
        u_ref[...] = jnp.dot(h_ext, wup_ref[:, cols(n_chunks + c)], preferred_element_type=jnp.float32)

    def conv(ext_ref, c):
        w = cw_ref[:, cols(c)]
        return (w[0:1] * ext_ref[pl.ds(FFN_HALO - 2, tm), :] + w[1:2] * ext_ref[pl.ds(FFN_HALO - 1, tm), :]
                + w[2:3] * ext_ref[pl.ds(FFN_HALO, tm), :] + cb_ref[:, cols(c)])

    def down_from(g_ref, u_ref, c):
        act = jax.nn.gelu(conv(g_ref, c), approximate=True) * conv(u_ref, n_chunks + c)
        acc_ref[...] += jnp.dot(_bf(act), wdn_ref[c], preferred_element_type=jnp.float32)

    assert n_chunks % 2 == 1
    acc_ref[...] = jnp.zeros_like(acc_ref)
    up_to(g0_ref, u0_ref, 0)

    def pair(j, carry):
        c = 2 * j
        up_to(g1_ref, u1_ref, c + 1)
        down_from(g0_ref, u0_ref, c)
        up_to(g0_ref, u0_ref, c + 2)
        down_from(g1_ref, u1_ref, c + 1)
        return carry

    lax.fori_loop(0, (n_chunks - 1) // 2, pair, 0)
    down_from(g0_ref, u0_ref, n_chunks - 1)
    out_ref[...] = x + _rms(acc_ref[...], gpost_ref[...])


def _ffn(l, x, gpre, wup, cw, cb, wdn, gpost):
    bsz, seq, d = x.shape
    tm = min(TM_FFN, seq)
    n_chunks = wdn.shape[1]
    chunk = wdn.shape[2]
    tok = pl.BlockSpec((None, tm, d), lambda b, i: (b, i, 0))
    halo = pl.BlockSpec((None, FFN_HALO, d), lambda b, i: (b, jnp.maximum(i * (tm // FFN_HALO) - 1, 0), 0))
    weights = [gpre, wup, cw, cb, wdn, gpost]
    return pl.pallas_call(
        functools.partial(_ffn_kernel, tm=tm, n_chunks=n_chunks, chunk=chunk),
        grid=(bsz, seq // tm),
        in_specs=[tok, halo] + [_layer_spec(w, l) for w in weights],
        out_specs=tok,
        out_shape=jax.ShapeDtypeStruct((bsz, seq, d), jnp.float32),
        scratch_shapes=[pltpu.VMEM((tm + FFN_HALO, d), MXU_DTYPE)]
                       + [pltpu.VMEM((tm + FFN_HALO, chunk), jnp.float32)] * 4
                       + [pltpu.VMEM((tm, d), jnp.float32)],
        compiler_params=_cparams(2),
        name="ffn",
    )(x, x, *weights)


def _prep_w_in(w):
    n_in = w.shape[-1]
    n_pad = -(-n_in // 256) * 256
    return jnp.pad(_bf(w), ((0, 0), (0, 0), (0, n_pad - n_in)))


def _prep_compress(pe_k, pe_v, ck_w1, ck_w2, cv_w1, cv_w2):
    half = CMP_LEN // 2

    def w1_part(w1k, w1v, lo):
        wk = w1k.reshape(CMP_LEN, HEAD_DIM, HEAD_DIM)[lo:lo + half]
        wv = w1v.reshape(CMP_LEN, HEAD_DIM, HEAD_DIM)[lo:lo + half]
        z = jnp.zeros_like(wk)
        top = jnp.concatenate([wk, z], axis=-1)
        bot = jnp.concatenate([z, wv], axis=-1)
        return jnp.concatenate([top, bot], axis=1).reshape(half * 2 * HEAD_DIM, 2 * HEAD_DIM)

    def pe_part(lo):
        return jnp.concatenate([pe_k[lo:lo + half], pe_v[lo:lo + half]], axis=1).reshape(1, half * 2 * HEAD_DIM)

    z = jnp.zeros_like(ck_w2)
    w2 = jnp.concatenate([jnp.concatenate([ck_w2, z], axis=1), jnp.concatenate([z, cv_w2], axis=1)], axis=0)
    return (pe_part(0), pe_part(half), _bf(w1_part(ck_w1, cv_w1, 0)), _bf(w1_part(ck_w1, cv_w1, half)), _bf(w2))


def _importance_matrix(seq):
    n_cmp = (seq - CMP_LEN) // CMP_STRIDE + 1
    n_sel = seq // SEL_LEN
    ratio = SEL_LEN // CMP_STRIDE
    n_ov = CMP_LEN // CMP_STRIDE
    n_pad = -(-n_cmp // LANES) * LANES
    assert n_sel <= LANES, "selection blocks must fit one lane group"
    assert min(SEL_TOPK, n_sel) > 3, "top-k must exceed the three forced blocks"
    mt = np.zeros((LANES, n_pad), np.float32)
    for j in range(n_sel):
        for m in range(ratio):
            for n in range(n_ov):
                c = ratio * j + m - n
                if 0 <= c < n_cmp:
                    mt[j, c] += 1.0
    return jnp.asarray(mt, MXU_DTYPE)


def _rope_table(seq):
    half = HEAD_DIM // 2
    per_row = LANES // half
    inv = ROPE_THETA ** (-jnp.arange(half, dtype=jnp.float32) * 2.0 / HEAD_DIM)
    pos = jnp.arange(seq, dtype=jnp.float32).reshape(seq // per_row, per_row, 1)
    ang = (pos * inv[None, None, :]).reshape(seq // per_row, LANES)
    cos, sin = lax.optimization_barrier((jnp.cos(ang), jnp.sin(ang)))
    cos, sin = cos.reshape(seq, half), sin.reshape(seq, half)
    return jnp.concatenate([cos, cos, sin, sin], axis=1)


def _block_diag(w):
    g, c, d = w.shape
    out = jnp.zeros((g * c, g * d), w.dtype)
    for j in range(g):
        out = out.at[j * c:(j + 1) * c, j * d:(j + 1) * d].set(w[j])
    return out


def kernel(x, norm_mix_pre, norm_mix_post, norm_ffn_pre, norm_ffn_post, w_in, w_out, conv_dw_w, conv_dw_b, conv_ln_g, conv_ln_b, nsa_pe_k, nsa_pe_v, nsa_ck_w1, nsa_ck_w2, nsa_cv_w1, nsa_cv_w2, sgu_ln_g, sgu_ln_b, sgu_w, sgu_b, pool_w, pool_scale, ffn_up, ffn_conv_w, ffn_conv_b, ffn_down):
    bsz, seq, d = x.shape
    depth = w_in.shape[0]
    ffn_dim = ffn_down.shape[1]
    cs = _rope_table(seq)
    mt = _importance_matrix(seq)
    rows = lambda v: v[:, None, :]

    w_ext = _prep_w_in(w_in)
    cmp_w = jax.vmap(_prep_compress)(nsa_pe_k, nsa_pe_v, nsa_ck_w1, nsa_ck_w2, nsa_cv_w1, nsa_cv_w2)
    sw_cat = jnp.transpose(sgu_w, (0, 2, 1, 3)).reshape(depth, SGU_CHUNK, SGU_GROUPS * SGU_CHUNK)
    sb_exp = jnp.repeat(jnp.swapaxes(sgu_b, 1, 2), 256 // SGU_GROUPS, axis=2)
    mix_w = [conv_dw_w, rows(conv_dw_b), rows(conv_ln_g), rows(conv_ln_b), rows(sgu_ln_g), rows(sgu_ln_b),
             sw_cat, sb_exp, _bf(jax.vmap(_block_diag)(pool_w)), rows(pool_scale), _bf(w_out), rows(norm_mix_post)]
    ffn_w = [rows(norm_ffn_pre), _bf(ffn_up), ffn_conv_w, rows(ffn_conv_b),
             _bf(ffn_down).reshape(depth, ffn_dim // FFN_CHUNK, FFN_CHUNK, d), rows(norm_ffn_post)]
    g_pre = rows(norm_mix_pre)

    for l in range(depth):
        za, zc, zd, qc, qr, a_cmp, ks, vs, kw, vw, gate = _in_proj(l, x, g_pre, w_ext, cs)
        kk, vv = _compress(l, a_cmp, *cmp_w)
        y_nsa = _nsa(qc, qr, gate, kk, vv, mt, ks, vs, kw, vw)
        x = _mix_out(l, x, za, zc, zd, y_nsa, *mix_w)
        x = _ffn(l, x, *ffn_w)
    return x
```

```python
import functools

import numpy as np
import jax
import jax.numpy as jnp
from jax import lax
from jax.experimental import pallas as pl
from jax.experimental.pallas import tpu as pltpu

HEAD_DIM = 64
NSA_HEADS = 4
CONV_K = 31
CMP_LEN = 32
CMP_STRIDE = 16
SEL_LEN = 64
SEL_TOPK = 16
WINDOW = 512
SGU_CHUNK = 128
SGU_GROUPS = 4
POOL_WINDOWS = (2, 4, 8, 16)
FFN_CONV_K = 3
ROPE_THETA = 10000.0
RMS_EPS = 1e-6
LN_EPS = 1e-5
NEG_INF = -1e30
FORCE_SCORE = 1e6
REMOVED = -3e38
LOG2E = 1.4426950408889634

LANES = 128
SUBLANES = 8
V_ROWS = 80
MXU_DTYPE = jnp.bfloat16
VMEM_LIMIT = 56 * 1024 * 1024

TM_IN = 1024
TM_MIX = 512
TM_FFN = 1024
TQ = 256
TK = 512
FFN_CHUNK = 256
CONV_HALO = 32
POOL_HALO = 16
FFN_HALO = 16

_NT = (((1,), (1,)), ((), ()))


def _cparams(n_axes, flags=None):
    return pltpu.CompilerParams(dimension_semantics=("arbitrary",) * n_axes,
                                vmem_limit_bytes=VMEM_LIMIT, flags=flags)


def _layer_spec(a, l):
    return pl.BlockSpec((None,) + a.shape[1:], lambda *_: (l,) + (0,) * (a.ndim - 1),
                        pipeline_mode=pl.Buffered(1))


def _rms(x, g):
    return x * lax.rsqrt(jnp.mean(x * x, axis=-1, keepdims=True) + RMS_EPS) * g


def _layer_norm(x, g, b):
    mu = jnp.mean(x, axis=-1, keepdims=True)
    var = jnp.mean(jnp.square(x - mu), axis=-1, keepdims=True)
    return (x - mu) * lax.rsqrt(var + LN_EPS) * g + b


def _bf(x):
    return x.astype(MXU_DTYPE)


def _swap64(x):
    return pltpu.roll(x, 64, axis=1)


def _in_proj_kernel(x_ref, g_ref, w_ref, cs_ref,
                    za_ref, zc_ref, zd_ref, qc_ref, qr_ref, kv_ref,
                    ks_ref, vs_ref, kw_ref, vw_ref, gate_ref, kvt_ref, *, tm, tk):
    i = pl.program_id(1)
    h = _bf(_rms(x_ref[...], g_ref[...]))

    def proj(lo, hi):
        return jnp.dot(h, w_ref[:, lo:hi], preferred_element_type=jnp.float32)

    lane = lax.broadcasted_iota(jnp.int32, (tm, LANES), 1)
    q = proj(512, 768)
    kv01 = proj(768, 1024)
    kv2t = proj(1024, 1280)
    kvt_ref[...] = kv01[:, :LANES]
    group = CMP_LEN // 2
    for j in range(group):
        kv_ref[:, j * LANES:(j + 1) * LANES] = kvt_ref[pl.ds(j, tm // group, stride=group), :]
    kvs, kvw, tail0 = kv01[:, LANES:], kv2t[:, :LANES], kv2t[:, LANES:]
    kx = jnp.where(lane < 64, kvs, _swap64(kvw))
    vsw = jnp.where(lane < 64, _swap64(kvs), kvw)
    gate_ref[...] = jax.nn.sigmoid(tail0)

    cs = cs_ref[...]
    sc = _swap64(cs)
    cos2 = jnp.where(lane < 64, cs, sc)
    sin2 = jnp.where(lane < 64, sc, cs)
    first_half = (lane & (HEAD_DIM - 1)) < HEAD_DIM // 2

    def rope(x):
        swapped = jnp.where(first_half, -pltpu.roll(x, LANES - HEAD_DIM // 2, axis=1),
                            pltpu.roll(x, HEAD_DIM // 2, axis=1))
        return x * cos2 + swapped * sin2

    scale = HEAD_DIM ** -0.5 * LOG2E
    qc_ref[...] = _bf(q * scale)
    for pair in range(NSA_HEADS // 2):
        c = slice(pair * LANES, (pair + 1) * LANES)
        qr_ref[:, c] = _bf(rope(q[:, c]) * scale)

    kx_rot = rope(kx)
    kx_swap = _swap64(kx_rot)
    k_rot2 = jnp.where(lane < 64, kx_rot, kx_swap)
    kw_rot2 = jnp.where(lane < 64, kx_swap, kx_rot)
    pos = i * tm + lax.broadcasted_iota(jnp.int32, (tm, LANES), 0)
    onehot =jnp.where((pos >> 6) == lane, 1.0, 0.0)
    ks_ref[:, 0:LANES] = _bf(onehot)
    ks_ref[:, LANES:2 * LANES] = _bf(k_rot2)
    kw_ref[...] = _bf(kw_rot2)
    ones_col = jnp.where(lane == 64, 1.0, 0.0)
    vs = jnp.where(lane < 64, vsw, ones_col)
    vw = jnp.where(lane < 64, _swap64(vsw), ones_col)
    for j in range(tm // tk):
        vs_ref[j] = _bf(vs[j * tk:(j + 1) * tk].T[0:V_ROWS])
    for j in range(tm // TQ):
        vw_ref[j] = _bf(vw[j * TQ:(j + 1) * TQ].T[0:V_ROWS])

    tail = jnp.concatenate([tail0, proj(1280, 2048)], axis=1)
    n_gate = 3 * NSA_HEADS
    zc_ref[...] = tail[:, n_gate:n_gate + 512]
    zd_ref[...] = tail[:, n_gate + 512:n_gate + 768]
    za_ref[...] = proj(0, 512)


def _in_proj(l, x, g, w_ext, cs):
    bsz, seq, d = x.shape
    tm = min(TM_IN, seq)
    tk = min(TK, seq)
    tok = lambda c: pl.BlockSpec((None, tm, c), lambda b, i: (b, i, 0))
    f32, bf16 = jnp.float32, MXU_DTYPE

    def tok_out(c, dt):
        return tok(c), jax.ShapeDtypeStruct((bsz, seq, c), dt)

    def keytile_out(t):
        return (pl.BlockSpec((None, tm // t, V_ROWS, t), lambda b, i: (b, i, 0, 0)),
                jax.ShapeDtypeStruct((bsz, seq // t, V_ROWS, t), bf16))

    group = CMP_LEN // 2
    cmp_out = (pl.BlockSpec((None, tm // group, group * LANES), lambda b, i: (b, i, 0)),
               jax.ShapeDtypeStruct((bsz, seq // group, group * LANES), f32))
    outs = [tok_out(512, f32), tok_out(512, f32), tok_out(256, f32), tok_out(256, bf16), tok_out(256, bf16),
            cmp_out, tok_out(256, bf16), keytile_out(tk), tok_out(128, bf16), keytile_out(TQ),
            tok_out(128, f32)]
    return pl.pallas_call(
        functools.partial(_in_proj_kernel, tm=tm, tk=tk),
        grid=(bsz, seq // tm),
        in_specs=[tok(d), _layer_spec(g, l), _layer_spec(w_ext, l),
                  pl.BlockSpec((tm, LANES), lambda b, i: (i, 0))],
        out_specs=[spec for spec, _ in outs],
        out_shape=[shape for _, shape in outs],
        scratch_shapes=[pltpu.VMEM((tm, LANES), f32)],
        compiler_params=_cparams(2),
        name="in_proj",
    )(x, g, w_ext, cs)


def _compress_kernel(a_ref, pe_lo_ref, pe_hi_ref, w_lo_ref, w_hi_ref, w2_ref, kk_ref, vv_ref):
    a = a_ref[...]
    u = jnp.dot(_bf(a + pe_lo_ref[...]), w_lo_ref[...], preferred_element_type=jnp.float32)
    v = jnp.dot(_bf(a + pe_hi_ref[...]), w_hi_ref[...], preferred_element_type=jnp.float32)
    n = a.shape[0]
    pre = u + pltpu.roll(v, n - 1, axis=0)
    act = jax.nn.gelu(pre, approximate=True)
    kv = jnp.dot(_bf(act), w2_ref[...], preferred_element_type=jnp.float32)
    vk = _swap64(kv)
    lane = lax.broadcasted_iota(jnp.int32, (n, LANES), 1)
    kk_ref[...] = _bf(jnp.where(lane < 64, kv, vk))
    vv = jnp.where(lane < 64, vk, jnp.where(lane == 64, 1.0, 0.0))
    vv_ref[...] = _bf(vv.T[0:V_ROWS])


def _compress(l, a, pe_lo, pe_hi, w_lo, w_hi, w2):
    bsz, n, width = a.shape
    per_b = lambda c: pl.BlockSpec((None, n, c), lambda b: (b, 0, 0))
    return pl.pallas_call(
        _compress_kernel,
        grid=(bsz,),
        in_specs=[per_b(width)] + [_layer_spec(w, l) for w in (pe_lo, pe_hi, w_lo, w_hi, w2)],
        out_specs=[per_b(LANES), pl.BlockSpec((None, V_ROWS, n), lambda b: (b, 0, 0))],
        out_shape=[jax.ShapeDtypeStruct((bsz, n, LANES), MXU_DTYPE),
                   jax.ShapeDtypeStruct((bsz, V_ROWS, n), MXU_DTYPE)],
        compiler_params=_cparams(1),
        name="compress",
    )(a, pe_lo, pe_hi, w_lo, w_hi, w2)


def _nsa_kernel(qc_ref, qr_ref, gate_ref, kk_ref, vv_ref, mt_ref, mask_ref, ks_ref, vs_ref, kw_ref, vw_ref,
                out_ref, m_ref, acc_ref, s0_ref, s1_ref, mx0_ref, mx1_ref, ocmp_ref, imp_ref, *, tq, tk, seq):
    qi = pl.program_id(1)
    t0 = qi * tq
    rows = NSA_HEADS * tq
    n_cmp_pad = kk_ref.shape[0]
    top_n = min(SEL_TOPK, seq // SEL_LEN)
    n_sel = LANES
    win_keys = min(WINDOW + tq, seq)

    lane =lax.broadcasted_iota(jnp.int32, (tq, LANES), 1)

    def head_rows(q_ref):
        parts = []
        for hd in range(NSA_HEADS):
            pair = q_ref[:, (hd // 2) * LANES:(hd // 2 + 1) * LANES]
            keep = (lane < 64) if hd % 2 == 0 else (lane >= 64)
            parts.append(jnp.where(keep, pair, jnp.zeros_like(pair)))
        return jnp.concatenate(parts, axis=0)

    q_c = head_rows(qc_ref)
    q_r = head_rows(qr_ref)
    pos_l = t0 + (lax.broadcasted_iota(jnp.int32, (1, rows), 1) & (tq - 1))

    last_blk = (pos_l - (CMP_LEN - 1)) >> 4

    def compressed(nr):
        s = lax.dot_general(kk_ref[0:nr, :], q_c, _NT, preferred_element_type=jnp.float32)
        n_idx = lax.broadcasted_iota(jnp.int32, (nr, rows), 0)
        sm = jnp.where(n_idx <= last_blk, s, NEG_INF)
        mx = jnp.max(sm, axis=0, keepdims=True)
        e = jnp.exp2(sm - mx)
        den = jnp.sum(e, axis=0, keepdims=True)
        p_cmp = e * jnp.where(last_blk >= 0, 1.0 / den, 0.0)
        ocmp_ref[...] = jnp.dot(vv_ref[:, 0:nr], _bf(p_cmp), preferred_element_type=jnp.float32)
        p_sum = p_cmp[:, 0:tq] + p_cmp[:, tq:2 * tq] + p_cmp[:, 2 * tq:3 * tq] + p_cmp[:, 3 * tq:4 * tq]
        hi = _bf(p_sum)
        r1 = p_sum - hi.astype(jnp.float32)
        mid = _bf(r1)
        lo = _bf(r1 - mid.astype(jnp.float32))
        mt = mt_ref[:, 0:nr]
        imp_ref[...] = (jnp.dot(mt, hi, preferred_element_type=jnp.float32)
                        + jnp.dot(mt, mid, preferred_element_type=jnp.float32)
                        + jnp.dot(mt, lo, preferred_element_type=jnp.float32))

    n_var = n_cmp_pad // LANES
    tile_last = jnp.maximum((t0 + tq - CMP_LEN) >> 4, 0)
    need = jnp.minimum(tile_last // LANES, n_var - 1)
    for v in range(n_var):
        pl.when(need == v)(functools.partial(compressed, (v + 1) * LANES))
    o_cmp = ocmp_ref[...]
    imp = imp_ref[...]

    start = pl.multiple_of(jnp.maximum(t0 + tq - win_keys, 0), tq)
    sw = lax.dot_general(kw_ref[pl.ds(start, win_keys), :], q_r, _NT, preferred_element_type=jnp.float32)
    w_case = jnp.minimum(t0 // tq, 2)
    m_idx = (jnp.where(w_case == 0, 1, jnp.where(w_case == 1, 0, 3)),
             jnp.where(w_case == 0, 2, jnp.where(w_case == 1, 1, 0)),
             jnp.where(w_case == 2, 1, 2))
    sw = jnp.concatenate([sw[j * tq:(j + 1) * tq] + mask_ref[m_idx[j]] for j in range(3)], axis=0)
    mw = jnp.max(sw, axis=0, keepdims=True)
    pw = _bf(jnp.exp2(sw - mw))
    wt0 = start // tq
    ow = jnp.dot(vw_ref[wt0], pw[0:tq], preferred_element_type=jnp.float32)
    for j in range(1, win_keys // tq):
        ow = ow + jnp.dot(vw_ref[wt0 + j], pw[j * tq:(j + 1) * tq], preferred_element_type=jnp.float32)
    o_win = ow / ow[64:65, :]

    blk = lax.broadcasted_iota(jnp.int32, (n_sel, tq), 0)
    blk_f = blk.astype(jnp.float32)
    pos_t = t0 + lax.broadcasted_iota(jnp.int32, (n_sel, tq), 1)
    cur = pos_t >> 6
    forced = (blk == 0) | (blk == cur) | (blk == cur - 1)
    valid = (blk * SEL_LEN) <= pos_t
    val = jnp.where(valid & jnp.logical_not(forced), imp, NEG_INF)
    for _ in range(top_n - 3):
        best = jnp.max(val, axis=0, keepdims=True)
        first = jnp.min(jnp.where(val == best, blk_f, float(n_sel)), axis=0, keepdims=True)
        val = jnp.where(blk_f == first, REMOVED, val)
    bias_t = jnp.where((forced | (val == REMOVED)) & valid, 0.0, NEG_INF)
    bias = _bf(bias_t.T)

    l_sel = jnp.concatenate(
        [jnp.concatenate([bias, q_r[hd * tq:(hd + 1) * tq]], axis=1) for hd in range(NSA_HEADS)], axis=0)
    m_ref[...] = jnp.full((1, rows), NEG_INF, jnp.float32)
    acc_ref[...] = jnp.zeros((V_ROWS, rows), jnp.float32)

    def scores_to(buf, kt, causal):
        s_ref, mx_ref = buf
        k0 = pl.multiple_of(kt * tk, tk)
        sc = lax.dot_general(ks_ref[pl.ds(k0, tk), :], l_sel, _NT, preferred_element_type=jnp.float32)
        if causal:
            own = (t0 - k0) // tq
            lo = sc[0:tq] + mask_ref[1 - own]
            hi = sc[tq:2 * tq] + mask_ref[2 - own]
            s_ref[0:tq, :] = lo
            s_ref[tq:2 * tq, :] = hi
            mx_ref[...] = jnp.maximum(jnp.max(lo, axis=0, keepdims=True), jnp.max(hi, axis=0, keepdims=True))
        else:
            s_ref[...] = sc
            mx_ref[...] = jnp.max(sc, axis=0, keepdims=True)

    def softmax_from(buf, kt):
        s_ref, mx_ref = buf
        sc = s_ref[...]
        m_old = m_ref[...]
        m_new = jnp.maximum(m_old, mx_ref[...])
        alpha = jnp.exp2(m_old - m_new)
        p = jnp.exp2(sc - m_new)
        acc_ref[...] = alpha * acc_ref[...] + jnp.dot(vs_ref[kt], _bf(p), preferred_element_type=jnp.float32)
        m_ref[...] = m_new

    buf0, buf1 = (s0_ref, mx0_ref), (s1_ref, mx1_ref)
    n_full = t0 // tk
    n_pairs = n_full // 2
    scores_to(buf0, n_full, True)

    def pair(j, carry):
        k = 2 * j
        scores_to(buf1, k, False)
        softmax_from(buf0, jnp.where(j == 0, n_full, k - 1))
        scores_to(buf0, k + 1, False)
        softmax_from(buf1, k)
        return carry

    lax.fori_loop(0, n_pairs, pair, 0)
    pending = jnp.where(n_pairs == 0, n_full, 2 * n_pairs - 1)

    @pl.when(n_full % 2 == 1)
    def _():
        scores_to(buf1, n_full - 1, False)
        softmax_from(buf0, pending)
        softmax_from(buf1, n_full - 1)

    @pl.when(n_full % 2 == 0)
    def _():
        softmax_from(buf0, pending)

    acc = acc_ref[...]
    o_slc = acc / acc[64:65, :]

    gate_t = gate_ref[...].T
    y_t = []
    for hd in range(NSA_HEADS):
        c = slice(hd * tq, (hd + 1) * tq)
        y_t.append(gate_t[3 * hd:3 * hd + 1] * o_cmp[0:64, c]
                   + gate_t[3 * hd + 1:3 * hd + 2] * o_slc[0:64, c]
                   + gate_t[3 * hd + 2:3 * hd + 3] * o_win[0:64, c])
    out_ref[:, 0:LANES] = _bf(jnp.concatenate(y_t[0:2], axis=0).T)
    out_ref[:, LANES:2 * LANES] = _bf(jnp.concatenate(y_t[2:4], axis=0).T)


def _nsa(qc, qr, gate, kk, vv, mt, ks, vs, kw, vw):
    bsz, seq, _ = qc.shape
    tq = min(TQ, seq)
    tk = min(TK, seq)
    n_cmp_pad = kk.shape[1]
    rows = NSA_HEADS * tq
    assert tk == 2 * tq and min(WINDOW + tq, seq) == 3 * tq, "mask patterns assume 2 / 3 query-tile-sized key blocks"
    r_idx = lax.broadcasted_iota(jnp.int32, (tq, rows), 0)
    tau = lax.broadcasted_iota(jnp.int32, (tq, rows), 1) & (tq - 1)
    zero = jnp.zeros((tq, rows), jnp.float32)
    masks = jnp.stack([zero, jnp.where(r_idx <= tau, 0.0, NEG_INF), zero + NEG_INF,
                       jnp.where(r_idx > tau, 0.0, NEG_INF)])
    tile = lambda c: pl.BlockSpec((None, tq, c), lambda b, i: (b, i, 0))
    per_b = lambda r, c: pl.BlockSpec((None, r, c), lambda b, i: (b, 0, 0))
    per_b4 = lambda a: pl.BlockSpec((None,) + a.shape[1:], lambda b, i: (b, 0, 0, 0))
    const = lambda a: pl.BlockSpec(a.shape, lambda b, i: (0,) * a.ndim, pipeline_mode=pl.Buffered(1))
    return pl.pallas_call(
        functools.partial(_nsa_kernel, tq=tq, tk=tk, seq=seq),
        grid=(bsz, seq // tq),
        in_specs=[tile(256), tile(256), tile(LANES),
                  per_b(n_cmp_pad, LANES), per_b(V_ROWS, n_cmp_pad), const(mt), const(masks),
                  per_b(seq, 256), per_b4(vs), per_b(seq, LANES), per_b4(vw)],
        out_specs=tile(256),
        out_shape=jax.ShapeDtypeStruct((bsz, seq, 256), MXU_DTYPE),
        scratch_shapes=[pltpu.VMEM((1, rows), jnp.float32), pltpu.VMEM((V_ROWS, rows), jnp.float32),
                        pltpu.VMEM((tk, rows), jnp.float32), pltpu.VMEM((tk, rows), jnp.float32),
                        pltpu.VMEM((1, rows), jnp.float32), pltpu.VMEM((1, rows), jnp.float32),
                        pltpu.VMEM((V_ROWS, rows), jnp.float32), pltpu.VMEM((LANES, tq), jnp.float32)],
        compiler_params=_cparams(2),
        name="nsa",
    )(qc, qr, gate, kk, vv, mt, masks, ks, vs, kw, vw)


def _mix_out_kernel(x_ref, za_ref, zah_ref, zc_ref, zd_ref, zdh_ref, ynsa_ref,
                    cw_ref, cb_ref, clg_ref, clb_ref, slg_ref, slb_ref, sw_ref, sb_ref,
                    pw_ref, ps_ref, wout_ref, gpost_ref, out_ref,
                    uext_ref, ushift_ref, ycat_ref, pext_ref, *, tm):
    i = pl.program_id(1)
    not_first = jnp.where(i > 0, 1.0, 0.0)

    zc = jax.nn.gelu(zc_ref[...], approximate=True)
    u_s = zc[:, :256]
    v_s = _layer_norm(zc[:, 256:], slg_ref[...], slb_ref[...])
    t_i = lax.broadcasted_iota(jnp.int32, (SGU_CHUNK, SGU_GROUPS * SGU_CHUNK), 0)
    s_i = lax.rem(lax.broadcasted_iota(jnp.int32, (SGU_CHUNK, SGU_GROUPS * SGU_CHUNK), 1), SGU_CHUNK)
    w_cat = _bf(jnp.where(s_i <= t_i, sw_ref[...], 0.0))
    grp = lax.broadcasted_iota(jnp.int32, (SGU_CHUNK, 256), 1) >> 6
    ysgu = []
    for c in range(tm // SGU_CHUNK):
        vc = v_s[c * SGU_CHUNK:(c + 1) * SGU_CHUNK]
        stacked = _bf(jnp.concatenate([jnp.where(grp == g, vc, 0.0) for g in range(SGU_GROUPS)], axis=0))
        f = jnp.dot(w_cat, stacked, preferred_element_type=jnp.float32) + sb_ref[...]
        ysgu.append(u_s[c * SGU_CHUNK:(c + 1) * SGU_CHUNK] * f)
    ycat_ref[:, 512:768] = _bf(jnp.concatenate(ysgu, axis=0))

    pext_ref[0:POOL_HALO, :] = zdh_ref[...] * not_first
    pext_ref[POOL_HALO:, :] = zd_ref[...]
    n_ext = tm + POOL_HALO
    sums = []
    for shift in (1, 2, 4, 8):
        n = n_ext - shift
        nxt = pext_ref[pl.ds(shift, n), :] + pext_ref[pl.ds(0, n), :]
        pext_ref[pl.ds(shift, n), :] = nxt
        sums.append(pext_ref[pl.ds(POOL_HALO, tm), :])
    lane_grp = lax.broadcasted_iota(jnp.int32, (tm, 256), 1) >> 6
    pos = i * tm + lax.broadcasted_iota(jnp.int32, (tm, 256), 0)
    wsum = jnp.where(lane_grp == 0, sums[0], jnp.where(lane_grp == 1, sums[1],
                     jnp.where(lane_grp == 2, sums[2], sums[3])))
    width = jnp.where(lane_grp == 0, POOL_WINDOWS[0], jnp.where(lane_grp == 1, POOL_WINDOWS[1],
                      jnp.where(lane_grp == 2, POOL_WINDOWS[2], POOL_WINDOWS[3])))
    cnt = jnp.minimum(pos + 1, width).astype(jnp.float32)
    pooled = wsum / cnt - zd_ref[...]
    ycat_ref[:, 768:1024] = _bf(jnp.dot(_bf(pooled), pw_ref[...], preferred_element_type=jnp.float32) * ps_ref[...])

    ycat_ref[:, 256:512] = ynsa_ref[...]
    y_rest = (jnp.dot(ycat_ref[:, 256:512], wout_ref[256:512, :], preferred_element_type=jnp.float32)
              + jnp.dot(ycat_ref[:, 512:768], wout_ref[512:768, :], preferred_element_type=jnp.float32)
              + jnp.dot(ycat_ref[:, 768:1024], wout_ref[768:1024, :], preferred_element_type=jnp.float32))

    def glu(z):
        return z[:, :256] * jax.nn.sigmoid(z[:, 256:])

    uext_ref[0:CONV_HALO, :] = glu(zah_ref[...]) * not_first
    uext_ref[CONV_HALO:, :] = glu(za_ref[...])
    first = CONV_HALO - (CONV_K - 1)
    n_shift = tm + CONV_HALO - SUBLANES
    for r in range(SUBLANES):
        n = n_shift if first + r + n_shift <= tm + CONV_HALO else n_shift - SUBLANES
        ushift_ref[r, 0:n, :] = uext_ref[pl.ds(first + r, n), :]
    rc = 128
    for r0 in range(0, tm, rc):
        acc = jnp.zeros((rc, 256), jnp.float32)
        for k in range(CONV_K):
            acc = acc + cw_ref[k:k + 1, :] * ushift_ref[k % SUBLANES, pl.ds(r0 + k - k % SUBLANES, rc), :]
        yc = _layer_norm(acc + cb_ref[...], clg_ref[...], clb_ref[...])
        ycat_ref[r0:r0 + rc, 0:256] = _bf(yc * jax.nn.sigmoid(yc))

    y = y_rest + jnp.dot(ycat_ref[:, 0:256], wout_ref[0:256, :], preferred_element_type=jnp.float32)
    out_ref[...] = x_ref[...] + _rms(y, gpost_ref[...])


def _mix_out(l, x, za, zc, zd, ynsa, cw, cb, clg, clb, slg, slb, sw_cat, sb_exp, pw_bd, ps, wout, gpost):
    bsz, seq, d = x.shape
    tm = min(TM_MIX, seq)
    tok = lambda c: pl.BlockSpec((None, tm, c), lambda b, i: (b, i, 0))
    halo = lambda c, h: pl.BlockSpec((None, h, c), lambda b, i: (b, jnp.maximum(i * (tm // h) - 1, 0), 0))
    weights = [cw, cb, clg, clb, slg, slb, sw_cat, sb_exp, pw_bd, ps, wout, gpost]
    return pl.pallas_call(
        functools.partial(_mix_out_kernel, tm=tm),
        grid=(bsz, seq // tm),
        in_specs=[tok(d), tok(512), halo(512, CONV_HALO), tok(512), tok(256), halo(256, POOL_HALO), tok(256)]
                 + [_layer_spec(w, l) for w in weights],
        out_specs=tok(d),
        out_shape=jax.ShapeDtypeStruct((bsz, seq, d), jnp.float32),
        scratch_shapes=[pltpu.VMEM((tm + CONV_HALO, 256), jnp.float32),
                        pltpu.VMEM((SUBLANES, tm + CONV_HALO - SUBLANES, 256), jnp.float32),
                        pltpu.VMEM((tm, 4 * 256), MXU_DTYPE),
                        pltpu.VMEM((tm + POOL_HALO, 256), jnp.float32)],
        compiler_params=_cparams(2),
        name="mix_out",
    )(x, za, za, zc, zd, zd, ynsa, *weights)


def _ffn_kernel(x_ref, xh_ref, gpre_ref, wup_ref, cw_ref, cb_ref, wdn_ref, gpost_ref, out_ref,
                hext_ref, g0_ref, u0_ref, g1_ref, u1_ref, acc_ref, *, tm, n_chunks, chunk):
    i = pl.program_id(1)
    not_first = jnp.where(i > 0, 1.0, 0.0)
    x = x_ref[...]
    hext_ref[0:FFN_HALO, :] = _bf(_rms(xh_ref[...], gpre_ref[...]) * not_first)
    hext_ref[FFN_HALO:, :] = _bf(_rms(x, gpre_ref[...]))

    def cols(c):
        return pl.ds(pl.multiple_of(c * chunk, chunk), chunk)

    def up_to(g_ref, u_ref, c):
        h_ext = hext_ref[...]
        g_ref[...] = jnp.dot(h_ext, wup_ref[:, cols(c)], preferred_element_type=jnp.float32)
        u_ref[...] = jnp.dot(h_ext, wup_ref[:, cols(n_chunks + c)], preferred_element_type=jnp.float32)

    def conv(ext_ref, c):
        w = cw_ref[:, cols(c)]
        return (w[0:1] * ext_ref[pl.ds(FFN_HALO - 2, tm), :] + w[1:2] * ext_ref[pl.ds(FFN_HALO - 1, tm), :]
                + w[2:3] * ext_ref[pl.ds(FFN_HALO, tm), :] + cb_ref[:, cols(c)])

    def down_from(g_ref, u_ref, c):
        act = jax.nn.gelu(conv(g_ref, c), approximate=True) * conv(u_ref, n_chunks + c)
        acc_ref[...] += jnp.dot(_bf(act), wdn_ref[c], preferred_element_type=jnp.float32)

    assert n_chunks % 2 == 1
    acc_ref[...] = jnp.zeros_like(acc_ref)
    up_to(g0_ref, u0_ref, 0)

    def pair(j, carry):
        c = 2 * j
        up_to(g1_ref, u1_ref, c + 1)
        down_from(g0_ref, u0_ref, c)
        up_to(g0_ref, u0_ref, c + 2)
        down_from(g1_ref, u1_ref, c + 1)
        return carry

    lax.fori_loop(0, (n_chunks - 1) // 2, pair, 0)
    down_from(g0_ref, u0_ref, n_chunks - 1)
    out_ref[...] = x + _rms(acc_ref[...], gpost_ref[...])


def _ffn(l, x, gpre, wup, cw, cb, wdn, gpost):
    bsz, seq, d = x.shape
    tm = min(TM_FFN, seq)
    n_chunks = wdn.shape[1]
    chunk = wdn.shape[2]
    tok = pl.BlockSpec((None, tm, d), lambda b, i: (b, i, 0))
    halo = pl.BlockSpec((None, FFN_HALO, d), lambda b, i: (b, jnp.maximum(i * (tm // FFN_HALO) - 1, 0), 0))
    weights = [gpre, wup, cw, cb, wdn, gpost]
    return pl.pallas_call(
        functools.partial(_ffn_kernel, tm=tm, n_chunks=n_chunks, chunk=chunk),
        grid=(bsz, seq // tm),
        in_specs=[tok, halo] + [_layer_spec(w, l) for w in weights],
        out_specs=tok,
        out_shape=jax.ShapeDtypeStruct((bsz, seq, d), jnp.float32),
        scratch_shapes=[pltpu.VMEM((tm + FFN_HALO, d), MXU_DTYPE)]
                       + [pltpu.VMEM((tm + FFN_HALO, chunk), jnp.float32)] * 4
                       + [pltpu.VMEM((tm, d), jnp.float32)],
        compiler_params=_cparams(2),
        name="ffn",
    )(x, x, *weights)


def _prep_w_in(w):
    n_in = w.shape[-1]
    n_pad = -(-n_in // 256) * 256
    return jnp.pad(_bf(w), ((0, 0), (0, 0), (0, n_pad - n_in)))


def _prep_compress(pe_k, pe_v, ck_w1, ck_w2, cv_w1, cv_w2):
    half = CMP_LEN // 2

    def w1_part(w1k, w1v, lo):
        wk = w1k.reshape(CMP_LEN, HEAD_DIM, HEAD_DIM)[lo:lo + half]
        wv = w1v.reshape(CMP_LEN, HEAD_DIM, HEAD_DIM)[lo:lo + half]
        z = jnp.zeros_like(wk)
        top = jnp.concatenate([wk, z], axis=-1)
        bot = jnp.concatenate([z, wv], axis=-1)
        return jnp.concatenate([top, bot], axis=1).reshape(half * 2 * HEAD_DIM, 2 * HEAD_DIM)

    def pe_part(lo):
        return jnp.concatenate([pe_k[lo:lo + half], pe_v[lo:lo + half]], axis=1).reshape(1, half * 2 * HEAD_DIM)

    z = jnp.zeros_like(ck_w2)
    w2 = jnp.concatenate([jnp.concatenate([ck_w2, z], axis=1), jnp.concatenate([z, cv_w2], axis=1)], axis=0)
    return (pe_part(0), pe_part(half), _bf(w1_part(ck_w1, cv_w1, 0)), _bf(w1_part(ck_w1, cv_w1, half)), _bf(w2))


def _importance_matrix(seq):
    n_cmp = (seq - CMP_LEN) // CMP_STRIDE + 1
    n_sel = seq // SEL_LEN
    ratio = SEL_LEN // CMP_STRIDE
    n_ov = CMP_LEN // CMP_STRIDE
    n_pad = -(-n_cmp // LANES) * LANES
    assert n_sel <= LANES, "selection blocks must fit one lane group"
    assert min(SEL_TOPK, n_sel) > 3, "top-k must exceed the three forced blocks"
    mt = np.zeros((LANES, n_pad), np.float32)
    for j in range(n_sel):
        for m in range(ratio):
            for n in range(n_ov):
                c = ratio * j + m - n
                if 0 <= c < n_cmp:
                    mt[j, c] += 1.0
    return jnp.asarray(mt, MXU_DTYPE)


def _rope_table(seq):
    half = HEAD_DIM // 2
    per_row = LANES // half
    inv = ROPE_THETA ** (-jnp.arange(half, dtype=jnp.float32) * 2.0 / HEAD_DIM)
    pos = jnp.arange(seq, dtype=jnp.float32).reshape(seq // per_row, per_row, 1)
    ang = (pos * inv[None, None, :]).reshape(seq // per_row, LANES)
    cos, sin = lax.optimization_barrier((jnp.cos(ang), jnp.sin(ang)))
    cos, sin = cos.reshape(seq, half), sin.reshape(seq, half)
    return jnp.concatenate([cos, cos, sin, sin], axis=1)


def _block_diag(w):
    g, c, d = w.shape
    out = jnp.zeros((g * c, g * d), w.dtype)
    for j in range(g):
        out = out.at[j * c:(j + 1) * c, j * d:(j + 1) * d].set(w[j])
    return out


def kernel(x, norm_mix_pre, norm_mix_post, norm_ffn_pre, norm_ffn_post, w_in, w_out, conv_dw_w, conv_dw_b, conv_ln_g, conv_ln_b, nsa_pe_k, nsa_pe_v, nsa_ck_w1, nsa_ck_w2, nsa_cv_w1, nsa_cv_w2, sgu_ln_g, sgu_ln_b, sgu_w, sgu_b, pool_w, pool_scale, ffn_up, ffn_conv_w, ffn_conv_b, ffn_down):
    bsz, seq, d = x.shape
    depth = w_in.shape[0]
    ffn_dim = ffn_down.shape[1]
    cs = _rope_table(seq)
    mt = _importance_matrix(seq)
    rows = lambda v: v[:, None, :]

    w_ext = _prep_w_in(w_in)
    cmp_w = jax.vmap(_prep_compress)(nsa_pe_k, nsa_pe_v, nsa_ck_w1, nsa_ck_w2, nsa_cv_w1, nsa_cv_w2)
    sw_cat = jnp.transpose(sgu_w, (0, 2, 1, 3)).reshape(depth, SGU_CHUNK, SGU_GROUPS * SGU_CHUNK)
    sb_exp = jnp.repeat(jnp.swapaxes(sgu_b, 1, 2), 256 // SGU_GROUPS, axis=2)
    mix_w = [conv_dw_w, rows(conv_dw_b), rows(conv_ln_g), rows(conv_ln_b), rows(sgu_ln_g), rows(sgu_ln_b),
             sw_cat, sb_exp, _bf(jax.vmap(_block_diag)(pool_w)), rows(pool_scale), _bf(w_out), rows(norm_mix_post)]
    ffn_w = [rows(norm_ffn_pre), _bf(ffn_up), ffn_conv_w, rows(ffn_conv_b),
             _bf(ffn_down).reshape(depth, ffn_dim // FFN_CHUNK, FFN_CHUNK, d), rows(norm_ffn_post)]
    g_pre = rows(norm_mix_pre)

    for l in range(depth):
        za, zc, zd, qc, qr, a_cmp, ks, vs, kw, vw, gate = _in_proj(l, x, g_pre, w_ext, cs)
        kk, vv = _compress(l, a_cmp, *cmp_w)
        y_nsa = _nsa(qc, qr, gate, kk, vv, mt, ks, vs, kw, vw)
        x = _mix_out(l, x, za, zc, zd, y_nsa, *mix_w)
        x = _ffn(l, x, *ffn_w)
    return x
```

```python
import functools

import numpy as np
import jax
import jax.numpy as jnp
from jax import lax
from jax.experimental import pallas as pl
from jax.experimental.pallas import tpu as pltpu

HEAD_DIM = 64
NSA_HEADS = 4
CONV_K = 31
CMP_LEN = 32
CMP_STRIDE = 16
SEL_LEN = 64
SEL_TOPK = 16
WINDOW = 512
SGU_CHUNK = 128
SGU_GROUPS = 4
POOL_WINDOWS = (2, 4, 8, 16)
FFN_CONV_K = 3
ROPE_THETA = 10000.0
RMS_EPS = 1e-6
LN_EPS = 1e-5
NEG_INF = -1e30
FORCE_SCORE = 1e6
REMOVED = -3e38
LOG2E = 1.4426950408889634

LANES = 128
SUBLANES = 8
V_ROWS = 80
MXU_DTYPE = jnp.bfloat16
VMEM_LIMIT = 56 * 1024 * 1024

TM_IN = 1024
TM_MIX = 512
TM_FFN = 1024
TQ = 256
TK = 512
FFN_CHUNK = 256
CONV_HALO = 32
POOL_HALO = 16
FFN_HALO = 16

_NT = (((1,), (1,)), ((), ()))


def _cparams(n_axes, flags=None):
    return pltpu.CompilerParams(dimension_semantics=("arbitrary",) * n_axes,
                                vmem_limit_bytes=VMEM_LIMIT, flags=flags)


def _layer_spec(a, l):
    return pl.BlockSpec((None,) + a.shape[1:], lambda *_: (l,) + (0,) * (a.ndim - 1),
                        pipeline_mode=pl.Buffered(1))


def _rms(x, g):
    return x * lax.rsqrt(jnp.mean(x * x, axis=-1, keepdims=True) + RMS_EPS) * g


def _layer_norm(x, g, b):
    mu = jnp.mean(x, axis=-1, keepdims=True)
    var = jnp.mean(jnp.square(x - mu), axis=-1, keepdims=True)
    return (x - mu) * lax.rsqrt(var + LN_EPS) * g + b


def _bf(x):
    return x.astype(MXU_DTYPE)


def _swap64(x):
    return pltpu.roll(x, 64, axis=1)


def _in_proj_kernel(x_ref, g_ref, w_ref, cs_ref,
                    za_ref, zc_ref, zd_ref, qc_ref, qr_ref, kv_ref,
                    ks_ref, vs_ref, kw_ref, vw_ref, gate_ref, kvt_ref, *, tm, tk):
    i = pl.program_id(1)
    h = _bf(_rms(x_ref[...], g_ref[...]))

    def proj(lo, hi):
        return jnp.dot(h, w_ref[:, lo:hi], preferred_element_type=jnp.float32)

    lane = lax.broadcasted_iota(jnp.int32, (tm, LANES), 1)
    q = proj(512, 768)
    kv01 = proj(768, 1024)
    kv2t = proj(1024, 1280)
    kvt_ref[...] = kv01[:, :LANES]
    group = CMP_LEN // 2
    for j in range(group):
        kv_ref[:, j * LANES:(j + 1) * LANES] = kvt_ref[pl.ds(j, tm // group, stride=group), :]
    kvs, kvw, tail0 = kv01[:, LANES:], kv2t[:, :LANES], kv2t[:, LANES:]
    kx = jnp.where(lane < 64, kvs, _swap64(kvw))
    vsw = jnp.where(lane < 64, _swap64(kvs), kvw)
    gate_ref[...] = jax.nn.sigmoid(tail0)

    cs = cs_ref[...]
    sc = _swap64(cs)
    cos2 = jnp.where(lane < 64, cs, sc)
    sin2 = jnp.where(lane < 64, sc, cs)
    first_half = (lane & (HEAD_DIM - 1)) < HEAD_DIM // 2

    def rope(x):
        swapped = jnp.where(first_half, -pltpu.roll(x, LANES - HEAD_DIM // 2, axis=1),
                            pltpu.roll(x, HEAD_DIM // 2, axis=1))
        return x * cos2 + swapped * sin2

    scale = HEAD_DIM ** -0.5 * LOG2E
    qc_ref[...] = _bf(q * scale)
    for pair in range(NSA_HEADS // 2):
        c = slice(pair * LANES, (pair + 1) * LANES)
        qr_ref[:, c] = _bf(rope(q[:, c]) * scale)

    kx_rot = rope(kx)
    kx_swap = _swap64(kx_rot)
    k_rot2 = jnp.where(lane < 64, kx_rot, kx_swap)
    kw_rot2 = jnp.where(lane < 64, kx_swap, kx_rot)
    pos = i * tm + lax.broadcasted_iota(jnp.int32, (tm, LANES), 0)
    onehot =jnp.where((pos >> 6) == lane, 1.0, 0.0)
    ks_ref[:, 0:LANES] = _bf(onehot)
    ks_ref[:, LANES:2 * LANES] = _bf(k_rot2)
    kw_ref[...] = _bf(kw_rot2)
    ones_col = jnp.where(lane == 64, 1.0, 0.0)
    vs = jnp.where(lane < 64, vsw, ones_col)
    vw = jnp.where(lane < 64, _swap64(vsw), ones_col)
    for j in range(tm // tk):
        vs_ref[j] = _bf(vs[j * tk:(j + 1) * tk].T[0:V_ROWS])
    for j in range(tm // TQ):
        vw_ref[j] = _bf(vw[j * TQ:(j + 1) * TQ].T[0:V_ROWS])

    tail = jnp.concatenate([tail0, proj(1280, 2048)], axis=1)
    n_gate = 3 * NSA_HEADS
    zc_ref[...] = tail[:, n_gate:n_gate + 512]
    zd_ref[...] = tail[:, n_gate + 512:n_gate + 768]
    za_ref[...] = proj(0, 512)


def _in_proj(l, x, g, w_ext, cs):
    bsz, seq, d = x.shape
    tm = min(TM_IN, seq)
    tk = min(TK, seq)
    tok = lambda c: pl.BlockSpec((None, tm, c), lambda b, i: (b, i, 0))
    f32, bf16 = jnp.float32, MXU_DTYPE

    def tok_out(c, dt):
        return tok(c), jax.ShapeDtypeStruct((bsz, seq, c), dt)

    def keytile_out(t):
        return (pl.BlockSpec((None, tm // t, V_ROWS, t), lambda b, i: (b, i, 0, 0)),
                jax.ShapeDtypeStruct((bsz, seq // t, V_ROWS, t), bf16))

    group = CMP_LEN // 2
    cmp_out = (pl.BlockSpec((None, tm // group, group * LANES), lambda b, i: (b, i, 0)),
               jax.ShapeDtypeStruct((bsz, seq // group, group * LANES), f32))
    outs = [tok_out(512, f32), tok_out(512, f32), tok_out(256, f32), tok_out(256, bf16), tok_out(256, bf16),
            cmp_out, tok_out(256, bf16), keytile_out(tk), tok_out(128, bf16), keytile_out(TQ),
            tok_out(128, f32)]
    return pl.pallas_call(
        functools.partial(_in_proj_kernel, tm=tm, tk=tk),
        grid=(bsz, seq // tm),
        in_specs=[tok(d), _layer_spec(g, l), _layer_spec(w_ext, l),
                  pl.BlockSpec((tm, LANES), lambda b, i: (i, 0))],
        out_specs=[spec for spec, _ in outs],
        out_shape=[shape for _, shape in outs],
        scratch_shapes=[pltpu.VMEM((tm, LANES), f32)],
        compiler_params=_cparams(2),
        name="in_proj",
    )(x, g, w_ext, cs)


def _compress_kernel(a_ref, pe_lo_ref, pe_hi_ref, w_lo_ref, w_hi_ref, w2_ref, kk_ref, vv_ref):
    a = a_ref[...]
    u = jnp.dot(_bf(a + pe_lo_ref[...]), w_lo_ref[...], preferred_element_type=jnp.float32)
    v = jnp.dot(_bf(a + pe_hi_ref[...]), w_hi_ref[...], preferred_element_type=jnp.float32)
    n = a.shape[0]
    pre = u + pltpu.roll(v, n - 1, axis=0)
    act = jax.nn.gelu(pre, approximate=True)
    kv = jnp.dot(_bf(act), w2_ref[...], preferred_element_type=jnp.float32)
    vk = _swap64(kv)
    lane = lax.broadcasted_iota(jnp.int32, (n, LANES), 1)
    kk_ref[...] = _bf(jnp.where(lane < 64, kv, vk))
    vv = jnp.where(lane < 64, vk, jnp.where(lane == 64, 1.0, 0.0))
    vv_ref[...] = _bf(vv.T[0:V_ROWS])


def _compress(l, a, pe_lo, pe_hi, w_lo, w_hi, w2):
    bsz, n, width = a.shape
    per_b = lambda c: pl.BlockSpec((None, n, c), lambda b: (b, 0, 0))
    return pl.pallas_call(
        _compress_kernel,
        grid=(bsz,),
        in_specs=[per_b(width)] + [_layer_spec(w, l) for w in (pe_lo, pe_hi, w_lo, w_hi, w2)],
        out_specs=[per_b(LANES), pl.BlockSpec((None, V_ROWS, n), lambda b: (b, 0, 0))],
        out_shape=[jax.ShapeDtypeStruct((bsz, n, LANES), MXU_DTYPE),
                   jax.ShapeDtypeStruct((bsz, V_ROWS, n), MXU_DTYPE)],
        compiler_params=_cparams(1),
        name="compress",
    )(a, pe_lo, pe_hi, w_lo, w_hi, w2)


def _nsa_kernel(qc_ref, qr_ref, gate_ref, kk_ref, vv_ref, mt_ref, mask_ref, ks_ref, vs_ref, kw_ref, vw_ref,
                out_ref, m_ref, acc_ref, s0_ref, s1_ref, mx0_ref, mx1_ref, ocmp_ref, imp_ref, *, tq, tk, seq):
    qi = pl.program_id(1)
    t0 = qi * tq
    rows = NSA_HEADS * tq
    n_cmp_pad = kk_ref.shape[0]
    top_n = min(SEL_TOPK, seq // SEL_LEN)
    n_sel = LANES
    win_keys = min(WINDOW + tq, seq)

    lane =lax.broadcasted_iota(jnp.int32, (tq, LANES), 1)

    def head_rows(q_ref):
        parts = []
        for hd in range(NSA_HEADS):
            pair = q_ref[:, (hd // 2) * LANES:(hd // 2 + 1) * LANES]
            keep = (lane < 64) if hd % 2 == 0 else (lane >= 64)
            parts.append(jnp.where(keep, pair, jnp.zeros_like(pair)))
        return jnp.concatenate(parts, axis=0)

    q_c = head_rows(qc_ref)
    q_r = head_rows(qr_ref)
    pos_l = t0 + (lax.broadcasted_iota(jnp.int32, (1, rows), 1) & (tq - 1))

    last_blk = (pos_l - (CMP_LEN - 1)) >> 4

    def compressed(nr):
        s = lax.dot_general(kk_ref[0:nr, :], q_c, _NT, preferred_element_type=jnp.float32)
        n_idx = lax.broadcasted_iota(jnp.int32, (nr, rows), 0)
        sm = jnp.where(n_idx <= last_blk, s, NEG_INF)
        mx = jnp.max(sm, axis=0, keepdims=True)
        e = jnp.exp2(sm - mx)
        den = jnp.sum(e, axis=0, keepdims=True)
        p_cmp = e * jnp.where(last_blk >= 0, 1.0 / den, 0.0)
        ocmp_ref[...] = jnp.dot(vv_ref[:, 0:nr], _bf(p_cmp), preferred_element_type=jnp.float32)
        p_sum = p_cmp[:, 0:tq] + p_cmp[:, tq:2 * tq] + p_cmp[:, 2 * tq:3 * tq] + p_cmp[:, 3 * tq:4 * tq]
        hi = _bf(p_sum)
        r1 = p_sum - hi.astype(jnp.float32)
        mid = _bf(r1)
        lo = _bf(r1 - mid.astype(jnp.float32))
        mt = mt_ref[:, 0:nr]
        imp_ref[...] = (jnp.dot(mt, hi, preferred_element_type=jnp.float32)
                        + jnp.dot(mt, mid, preferred_element_type=jnp.float32)
                        + jnp.dot(mt, lo, preferred_element_type=jnp.float32))

    n_var = n_cmp_pad // LANES
    tile_last = jnp.maximum((t0 + tq - CMP_LEN) >> 4, 0)
    need = jnp.minimum(tile_last // LANES, n_var - 1)
    for v in range(n_var):
        pl.when(need == v)(functools.partial(compressed, (v + 1) * LANES))
    o_cmp = ocmp_ref[...]
    imp = imp_ref[...]

    start = pl.multiple_of(jnp.maximum(t0 + tq - win_keys, 0), tq)
    sw = lax.dot_general(kw_ref[pl.ds(start, win_keys), :], q_r, _NT, preferred_element_type=jnp.float32)
    w_case = jnp.minimum(t0 // tq, 2)
    m_idx = (jnp.where(w_case == 0, 1, jnp.where(w_case == 1, 0, 3)),
             jnp.where(w_case == 0, 2, jnp.where(w_case == 1, 1, 0)),
             jnp.where(w_case == 2, 1, 2))
    sw = jnp.concatenate([sw[j * tq:(j + 1) * tq] + mask_ref[m_idx[j]] for j in range(3)], axis=0)
    mw = jnp.max(sw, axis=0, keepdims=True)
    pw = _bf(jnp.exp2(sw - mw))
    wt0 = start // tq
    ow = jnp.dot(vw_ref[wt0], pw[0:tq], preferred_element_type=jnp.float32)
    for j in range(1, win_keys // tq):
        ow = ow + jnp.dot(vw_ref[wt0 + j], pw[j * tq:(j + 1) * tq], preferred_element_type=jnp.float32)
    o_win = ow / ow[64:65, :]

    blk = lax.broadcasted_iota(jnp.int32, (n_sel, tq), 0)
    blk_f = blk.astype(jnp.float32)
    pos_t = t0 + lax.broadcasted_iota(jnp.int32, (n_sel, tq), 1)
    cur = pos_t >> 6
    forced = (blk == 0) | (blk == cur) | (blk == cur - 1)
    valid = (blk * SEL_LEN) <= pos_t
    val = jnp.where(valid & jnp.logical_not(forced), imp, NEG_INF)
    for _ in range(top_n - 3):
        best = jnp.max(val, axis=0, keepdims=True)
        first = jnp.min(jnp.where(val == best, blk_f, float(n_sel)), axis=0, keepdims=True)
        val = jnp.where(blk_f == first, REMOVED, val)
    bias_t = jnp.where((forced | (val == REMOVED)) & valid, 0.0, NEG_INF)
    bias = _bf(bias_t.T)

    l_sel = jnp.concatenate(
        [jnp.concatenate([bias, q_r[hd * tq:(hd + 1) * tq]], axis=1) for hd in range(NSA_HEADS)], axis=0)
    m_ref[...] = jnp.full((1, rows), NEG_INF, jnp.float32)
    acc_ref[...] = jnp.zeros((V_ROWS, rows), jnp.float32)

    def scores_to(buf, kt, causal):
        s_ref, mx_ref = buf
        k0 = pl.multiple_of(kt * tk, tk)
        sc = lax.dot_general(ks_ref[pl.ds(k0, tk), :], l_sel, _NT, preferred_element_type=jnp.float32)
        if causal:
            own = (t0 - k0) // tq
            lo = sc[0:tq] + mask_ref[1 - own]
            hi = sc[tq:2 * tq] + mask_ref[2 - own]
            s_ref[0:tq, :] = lo
            s_ref[tq:2 * tq, :] = hi
            mx_ref[...] = jnp.maximum(jnp.max(lo, axis=0, keepdims=True), jnp.max(hi, axis=0, keepdims=True))
        else:
            s_ref[...] = sc
            mx_ref[...] = jnp.max(sc, axis=0, keepdims=True)

    def softmax_from(buf, kt):
        s_ref, mx_ref = buf
        sc = s_ref[...]
        m_old = m_ref[...]
        m_new = jnp.maximum(m_old, mx_ref[...])
        alpha = jnp.exp2(m_old - m_new)
        p = jnp.exp2(sc - m_new)
        acc_ref[...] = alpha * acc_ref[...] + jnp.dot(vs_ref[kt], _bf(p), preferred_element_type=jnp.float32)
        m_ref[...] = m_new

    buf0, buf1 = (s0_ref, mx0_ref), (s1_ref, mx1_ref)
    n_full = t0 // tk
    n_pairs = n_full // 2
    scores_to(buf0, n_full, True)

    def pair(j, carry):
        k = 2 * j
        scores_to(buf1, k, False)
        softmax_from(buf0, jnp.where(j == 0, n_full, k - 1))
        scores_to(buf0, k + 1, False)
        softmax_from(buf1, k)
        return carry

    lax.fori_loop(0, n_pairs, pair, 0)
    pending = jnp.where(n_pairs == 0, n_full, 2 * n_pairs - 1)

    @pl.when(n_full % 2 == 1)
    def _():
        scores_to(buf1, n_full - 1, False)
        softmax_from(buf0, pending)
        softmax_from(buf1, n_full - 1)

    @pl.when(n_full % 2 == 0)
    def _():
        softmax_from(buf0, pending)

    acc = acc_ref[...]
    o_slc = acc / acc[64:65, :]

    gate_t = gate_ref[...].T
    y_t = []
    for hd in range(NSA_HEADS):
        c = slice(hd * tq, (hd + 1) * tq)
        y_t.append(gate_t[3 * hd:3 * hd + 1] * o_cmp[0:64, c]
                   + gate_t[3 * hd + 1:3 * hd + 2] * o_slc[0:64, c]
                   + gate_t[3 * hd + 2:3 * hd + 3] * o_win[0:64, c])
    out_ref[:, 0:LANES] = _bf(jnp.concatenate(y_t[0:2], axis=0).T)
    out_ref[:, LANES:2 * LANES] = _bf(jnp.concatenate(y_t[2:4], axis=0).T)


def _nsa(qc, qr, gate, kk, vv, mt, ks, vs, kw, vw):
    bsz, seq, _ = qc.shape
    tq = min(TQ, seq)
    tk = min(TK, seq)
    n_cmp_pad = kk.shape[1]
    rows = NSA_HEADS * tq
    assert tk == 2 * tq and min(WINDOW + tq, seq) == 3 * tq, "mask patterns assume 2 / 3 query-tile-sized key blocks"
    r_idx = lax.broadcasted_iota(jnp.int32, (tq, rows), 0)
    tau = lax.broadcasted_iota(jnp.int32, (tq, rows), 1) & (tq - 1)
    zero = jnp.zeros((tq, rows), jnp.float32)
    masks = jnp.stack([zero, jnp.where(r_idx <= tau, 0.0, NEG_INF), zero + NEG_INF,
                       jnp.where(r_idx > tau, 0.0, NEG_INF)])
    tile = lambda c: pl.BlockSpec((None, tq, c), lambda b, i: (b, i, 0))
    per_b = lambda r, c: pl.BlockSpec((None, r, c), lambda b, i: (b, 0, 0))
    per_b4 = lambda a: pl.BlockSpec((None,) + a.shape[1:], lambda b, i: (b, 0, 0, 0))
    const = lambda a: pl.BlockSpec(a.shape, lambda b, i: (0,) * a.ndim, pipeline_mode=pl.Buffered(1))
    return pl.pallas_call(
        functools.partial(_nsa_kernel, tq=tq, tk=tk, seq=seq),
        grid=(bsz, seq // tq),
        in_specs=[tile(256), tile(256), tile(LANES),
                  per_b(n_cmp_pad, LANES), per_b(V_ROWS, n_cmp_pad), const(mt), const(masks),
                  per_b(seq, 256), per_b4(vs), per_b(seq, LANES), per_b4(vw)],
        out_specs=tile(256),
        out_shape=jax.ShapeDtypeStruct((bsz, seq, 256), MXU_DTYPE),
        scratch_shapes=[pltpu.VMEM((1, rows), jnp.float32), pltpu.VMEM((V_ROWS, rows), jnp.float32),
                        pltpu.VMEM((tk, rows), jnp.float32), pltpu.VMEM((tk, rows), jnp.float32),
                        pltpu.VMEM((1, rows), jnp.float32), pltpu.VMEM((1, rows), jnp.float32),
                        pltpu.VMEM((V_ROWS, rows), jnp.float32), pltpu.VMEM((LANES, tq), jnp.float32)],
        compiler_params=_cparams(2),
        name="nsa",
    )(qc, qr, gate, kk, vv, mt, masks, ks, vs, kw, vw)


def _mix_out_kernel(x_ref, za_ref, zah_ref, zc_ref, zd_ref, zdh_ref, ynsa_ref,
                    cw_ref, cb_ref, clg_ref, clb_ref, slg_ref, slb_ref, sw_ref, sb_ref,
                    pw_ref, ps_ref, wout_ref, gpost_ref, out_ref,
                    uext_ref, ushift_ref, ycat_ref, pext_ref, *, tm):
    i = pl.program_id(1)
    not_first = jnp.where(i > 0, 1.0, 0.0)

    zc = jax.nn.gelu(zc_ref[...], approximate=True)
    u_s = zc[:, :256]
    v_s = _layer_norm(zc[:, 256:], slg_ref[...], slb_ref[...])
    t_i = lax.broadcasted_iota(jnp.int32, (SGU_CHUNK, SGU_GROUPS * SGU_CHUNK), 0)
    s_i = lax.rem(lax.broadcasted_iota(jnp.int32, (SGU_CHUNK, SGU_GROUPS * SGU_CHUNK), 1), SGU_CHUNK)
    w_cat = _bf(jnp.where(s_i <= t_i, sw_ref[...], 0.0))
    grp = lax.broadcasted_iota(jnp.int32, (SGU_CHUNK, 256), 1) >> 6
    ysgu = []
    for c in range(tm // SGU_CHUNK):
        vc = v_s[c * SGU_CHUNK:(c + 1) * SGU_CHUNK]
        stacked = _bf(jnp.concatenate([jnp.where(grp == g, vc, 0.0) for g in range(SGU_GROUPS)], axis=0))
        f = jnp.dot(w_cat, stacked, preferred_element_type=jnp.float32) + sb_ref[...]
        ysgu.append(u_s[c * SGU_CHUNK:(c + 1) * SGU_CHUNK] * f)
    ycat_ref[:, 512:768] = _bf(jnp.concatenate(ysgu, axis=0))

    pext_ref[0:POOL_HALO, :] = zdh_ref[...] * not_first
    pext_ref[POOL_HALO:, :] = zd_ref[...]
    n_ext = tm + POOL_HALO
    sums = []
    for shift in (1, 2, 4, 8):
        n = n_ext - shift
        nxt = pext_ref[pl.ds(shift, n), :] + pext_ref[pl.ds(0, n), :]
        pext_ref[pl.ds(shift, n), :] = nxt
        sums.append(pext_ref[pl.ds(POOL_HALO, tm), :])
    lane_grp = lax.broadcasted_iota(jnp.int32, (tm, 256), 1) >> 6
    pos = i * tm + lax.broadcasted_iota(jnp.int32, (tm, 256), 0)
    wsum = jnp.where(lane_grp == 0, sums[0], jnp.where(lane_grp == 1, sums[1],
                     jnp.where(lane_grp == 2, sums[2], sums[3])))
    width = jnp.where(lane_grp == 0, POOL_WINDOWS[0], jnp.where(lane_grp == 1, POOL_WINDOWS[1],
                      jnp.where(lane_grp == 2, POOL_WINDOWS[2], POOL_WINDOWS[3])))
    cnt = jnp.minimum(pos + 1, width).astype(jnp.float32)
    pooled = wsum / cnt - zd_ref[...]
    ycat_ref[:, 768:1024] = _bf(jnp.dot(_bf(pooled), pw_ref[...], preferred_element_type=jnp.float32) * ps_ref[...])

    ycat_ref[:, 256:512] = ynsa_ref[...]
    y_rest = (jnp.dot(ycat_ref[:, 256:512], wout_ref[256:512, :], preferred_element_type=jnp.float32)
              + jnp.dot(ycat_ref[:, 512:768], wout_ref[512:768, :], preferred_element_type=jnp.float32)
              + jnp.dot(ycat_ref[:, 768:1024], wout_ref[768:1024, :], preferred_element_type=jnp.float32))

    def glu(z):
        return z[:, :256] * jax.nn.sigmoid(z[:, 256:])

    uext_ref[0:CONV_HALO, :] = glu(zah_ref[...]) * not_first
    uext_ref[CONV_HALO:, :] = glu(za_ref[...])
    first = CONV_HALO - (CONV_K - 1)
    n_shift = tm + CONV_HALO - SUBLANES
    for r in range(SUBLANES):
        n = n_shift if first + r + n_shift <= tm + CONV_HALO else n_shift - SUBLANES
        ushift_ref[r, 0:n, :] = uext_ref[pl.ds(first + r, n), :]
    rc = 128
    for r0 in range(0, tm, rc):
        acc = jnp.zeros((rc, 256), jnp.float32)
        for k in range(CONV_K):
            acc = acc + cw_ref[k:k + 1, :] * ushift_ref[k % SUBLANES, pl.ds(r0 + k - k % SUBLANES, rc), :]
        yc = _layer_norm(acc + cb_ref[...], clg_ref[...], clb_ref[...])
        ycat_ref[r0:r0 + rc, 0:256] = _bf(yc * jax.nn.sigmoid(yc))

    y = y_rest + jnp.dot(ycat_ref[:, 0:256], wout_ref[0:256, :], preferred_element_type=jnp.float32)
    out_ref[...] = x_ref[...] + _rms(y, gpost_ref[...])


def _mix_out(l, x, za, zc, zd, ynsa, cw, cb, clg, clb, slg, slb, sw_cat, sb_exp, pw_bd, ps, wout, gpost):
    bsz, seq, d = x.shape
    tm = min(TM_MIX, seq)
    tok = lambda c: pl.BlockSpec((None, tm, c), lambda b, i: (b, i, 0))
    halo = lambda c, h: pl.BlockSpec((None, h, c), lambda b, i: (b, jnp.maximum(i * (tm // h) - 1, 0), 0))
    weights = [cw, cb, clg, clb, slg, slb, sw_cat, sb_exp, pw_bd, ps, wout, gpost]
    return pl.pallas_call(
        functools.partial(_mix_out_kernel, tm=tm),
        grid=(bsz, seq // tm),
        in_specs=[tok(d), tok(512), halo(512, CONV_HALO), tok(512), tok(256), halo(256, POOL_HALO), tok(256)]
                 + [_layer_spec(w, l) for w in weights],
        out_specs=tok(d),
        out_shape=jax.ShapeDtypeStruct((bsz, seq, d), jnp.float32),
        scratch_shapes=[pltpu.VMEM((tm + CONV_HALO, 256), jnp.float32),
                        pltpu.VMEM((SUBLANES, tm + CONV_HALO - SUBLANES, 256), jnp.float32),
                        pltpu.VMEM((tm, 4 * 256), MXU_DTYPE),
                        pltpu.VMEM((tm + POOL_HALO, 256), jnp.float32)],
        compiler_params=_cparams(2),
        name="mix_out",
    )(x, za, za, zc, zd, zd, ynsa, *weights)


def _ffn_kernel(x_ref, xh_ref, gpre_ref, wup_ref, cw_ref, cb_ref, wdn_ref, gpost_ref, out_ref,
                hext_ref, g0_ref, u0_ref, g1_ref, u1_ref, acc_ref, *, tm, n_chunks, chunk):
    i = pl.program_id(1)
    not_first = jnp.where(i > 0, 1.0, 0.0)
    x = x_ref[...]
    hext_ref[0:FFN_HALO, :] = _bf(_rms(xh_ref[...], gpre_ref[...]) * not_first)
    hext_ref[FFN_HALO:, :] = _bf(_rms(x, gpre_ref[...]))

    def cols(c):
        return pl.ds(pl.multiple_of(c * chunk, chunk), chunk)

    def up_to(g_ref, u_ref, c):
        h_ext = hext_ref[...]
        g_ref[...] = jnp.dot(h_ext, wup_ref[:, cols(c)], preferred_element_type=jnp.float32)
        u_ref[...] = jnp.dot(h_ext, wup_ref[:, cols(n_chunks + c)], preferred_element_type=jnp.float32)

    def conv(ext_ref, c, lanes):
        w = cw_ref[:, cols(c)][:, lanes]
        return (w[0:1] * ext_ref[pl.ds(FFN_HALO - 2, tm), lanes] + w[1:2] * ext_ref[pl.ds(FFN_HALO - 1, tm), lanes]
                + w[2:3] * ext_ref[pl.ds(FFN_HALO, tm), lanes] + cb_ref[:, cols(c)][:, lanes])

    def down_from(g_ref, u_ref, c):
        halves = []
        for lanes in (slice(0, LANES), slice(LANES, 2 * LANES)):
            halves.append(_bf(jax.nn.gelu(conv(g_ref, c, lanes), approximate=True)
                              * conv(u_ref, n_chunks + c, lanes)))
        act = jnp.concatenate(halves, axis=1)
        acc_ref[...] += jnp.dot(act, wdn_ref[c], preferred_element_type=jnp.float32)

    assert n_chunks % 2 == 1
    acc_ref[...] = jnp.zeros_like(acc_ref)
    up_to(g0_ref, u0_ref, 0)

    def pair(j, carry):
        c = 2 * j
        up_to(g1_ref, u1_ref, c + 1)
        down_from(g0_ref, u0_ref, c)
        up_to(g0_ref, u0_ref, c + 2)
        down_from(g1_ref, u1_ref, c + 1)
        return carry

    lax.fori_loop(0, (n_chunks - 1) // 2, pair, 0)
    down_from(g0_ref, u0_ref, n_chunks - 1)
    out_ref[...] = x + _rms(acc_ref[...], gpost_ref[...])


def _ffn(l, x, gpre, wup, cw, cb, wdn, gpost):
    bsz, seq, d = x.shape
    tm = min(TM_FFN, seq)
    n_chunks = wdn.shape[1]
    chunk = wdn.shape[2]
    tok = pl.BlockSpec((None, tm, d), lambda b, i: (b, i, 0))
    halo = pl.BlockSpec((None, FFN_HALO, d), lambda b, i: (b, jnp.maximum(i * (tm // FFN_HALO) - 1, 0), 0))
    weights = [gpre, wup, cw, cb, wdn, gpost]
    return pl.pallas_call(
        functools.partial(_ffn_kernel, tm=tm, n_chunks=n_chunks, chunk=chunk),
        grid=(bsz, seq // tm),
        in_specs=[tok, halo] + [_layer_spec(w, l) for w in weights],
        out_specs=tok,
        out_shape=jax.ShapeDtypeStruct((bsz, seq, d), jnp.float32),
        scratch_shapes=[pltpu.VMEM((tm + FFN_HALO, d), MXU_DTYPE)]
                       + [pltpu.VMEM((tm + FFN_HALO, chunk), jnp.float32)] * 4
                       + [pltpu.VMEM((tm, d), jnp.float32)],
        compiler_params=_cparams(2),
        name="ffn",
    )(x, x, *weights)


def _prep_w_in(w):
    n_in = w.shape[-1]
    n_pad = -(-n_in // 256) * 256
    return jnp.pad(_bf(w), ((0, 0), (0, 0), (0, n_pad - n_in)))


def _prep_compress(pe_k, pe_v, ck_w1, ck_w2, cv_w1, cv_w2):
    half = CMP_LEN // 2

    def w1_part(w1k, w1v, lo):
        wk = w1k.reshape(CMP_LEN, HEAD_DIM, HEAD_DIM)[lo:lo + half]
        wv = w1v.reshape(CMP_LEN, HEAD_DIM, HEAD_DIM)[lo:lo + half]
        z = jnp.zeros_like(wk)
        top = jnp.concatenate([wk, z], axis=-1)
        bot = jnp.concatenate([z, wv], axis=-1)
        return jnp.concatenate([top, bot], axis=1).reshape(half * 2 * HEAD_DIM, 2 * HEAD_DIM)

    def pe_part(lo):
        return jnp.concatenate([pe_k[lo:lo + half], pe_v[lo:lo + half]], axis=1).reshape(1, half * 2 * HEAD_DIM)

    z = jnp.zeros_like(ck_w2)
    w2 = jnp.concatenate([jnp.concatenate([ck_w2, z], axis=1), jnp.concatenate([z, cv_w2], axis=1)], axis=0)
    return (pe_part(0), pe_part(half), _bf(w1_part(ck_w1, cv_w1, 0)), _bf(w1_part(ck_w1, cv_w1, half)), _bf(w2))


def _importance_matrix(seq):
    n_cmp = (seq - CMP_LEN) // CMP_STRIDE + 1
    n_sel = seq // SEL_LEN
    ratio = SEL_LEN // CMP_STRIDE
    n_ov = CMP_LEN // CMP_STRIDE
    n_pad = -(-n_cmp // LANES) * LANES
    assert n_sel <= LANES, "selection blocks must fit one lane group"
    assert min(SEL_TOPK, n_sel) > 3, "top-k must exceed the three forced blocks"
    mt = np.zeros((LANES, n_pad), np.float32)
    for j in range(n_sel):
        for m in range(ratio):
            for n in range(n_ov):
                c = ratio * j + m - n
                if 0 <= c < n_cmp:
                    mt[j, c] += 1.0
    return jnp.asarray(mt, MXU_DTYPE)


def _rope_table(seq):
    half = HEAD_DIM // 2
    per_row = LANES // half
    inv = ROPE_THETA ** (-jnp.arange(half, dtype=jnp.float32) * 2.0 / HEAD_DIM)
    pos = jnp.arange(seq, dtype=jnp.float32).reshape(seq // per_row, per_row, 1)
    ang = (pos * inv[None, None, :]).reshape(seq // per_row, LANES)
    cos, sin = lax.optimization_barrier((jnp.cos(ang), jnp.sin(ang)))
    cos, sin = cos.reshape(seq, half), sin.reshape(seq, half)
    return jnp.concatenate([cos, cos, sin, sin], axis=1)


def _block_diag(w):
    g, c, d = w.shape
    out = jnp.zeros((g * c, g * d), w.dtype)
    for j in range(g):
        out = out.at[j * c:(j + 1) * c, j * d:(j + 1) * d].set(w[j])
    return out


def kernel(x, norm_mix_pre, norm_mix_post, norm_ffn_pre, norm_ffn_post, w_in, w_out, conv_dw_w, conv_dw_b, conv_ln_g, conv_ln_b, nsa_pe_k, nsa_pe_v, nsa_ck_w1, nsa_ck_w2, nsa_cv_w1, nsa_cv_w2, sgu_ln_g, sgu_ln_b, sgu_w, sgu_b, pool_w, pool_scale, ffn_up, ffn_conv_w, ffn_conv_b, ffn_down):
    bsz, seq, d = x.shape
    depth = w_in.shape[0]
    ffn_dim = ffn_down.shape[1]
    cs = _rope_table(seq)
    mt = _importance_matrix(seq)
    rows = lambda v: v[:, None, :]

    w_ext = _prep_w_in(w_in)
    cmp_w = jax.vmap(_prep_compress)(nsa_pe_k, nsa_pe_v, nsa_ck_w1, nsa_ck_w2, nsa_cv_w1, nsa_cv_w2)
    sw_cat = jnp.transpose(sgu_w, (0, 2, 1, 3)).reshape(depth, SGU_CHUNK, SGU_GROUPS * SGU_CHUNK)
    sb_exp = jnp.repeat(jnp.swapaxes(sgu_b, 1, 2), 256 // SGU_GROUPS, axis=2)
    mix_w = [conv_dw_w, rows(conv_dw_b), rows(conv_ln_g), rows(conv_ln_b), rows(sgu_ln_g), rows(sgu_ln_b),
             sw_cat, sb_exp, _bf(jax.vmap(_block_diag)(pool_w)), rows(pool_scale), _bf(w_out), rows(norm_mix_post)]
    ffn_w = [rows(norm_ffn_pre), _bf(ffn_up), ffn_conv_w, rows(ffn_conv_b),
             _bf(ffn_down).reshape(depth, ffn_dim // FFN_CHUNK, FFN_CHUNK, d), rows(norm_ffn_post)]
    g_pre = rows(norm_mix_pre)

    for l in range(depth):
        za, zc, zd, qc, qr, a_cmp, ks, vs, kw, vw, gate = _in_proj(l, x, g_pre, w_ext, cs)
        kk, vv = _compress(l, a_cmp, *cmp_w)
        y_nsa = _nsa(qc, qr, gate, kk, vv, mt, ks, vs, kw, vw)
        x = _mix_out(l, x, za, zc, zd, y_nsa, *mix_w)
        x = _ffn(l, x, *ffn_w)
    return x
```
